```python
import jax, jax.numpy as jnp
from jax import lax
import numpy as np

D_MODEL = 1024
BATCH = 4
SEQ = 4096
DEPTH = 1
DEC_BATCH = 128
DEC_SEQ = 8
PAST_LEN = 2048
PAGE_SIZE = 128

RWKV_HEADS = 8
RWKV_HEAD = 64
RWKV_W = RWKV_HEADS * RWKV_HEAD
DECAY_RANK = 64
AAA_RANK = 64
GATE_RANK = 160
RWKV_GN_EPS = 64e-5

NSA_HEADS = 8
NSA_KV = 2
HEAD_DIM = 64
HPG = NSA_HEADS // NSA_KV
NSA_W = NSA_HEADS * HEAD_DIM
KV_W = NSA_KV * HEAD_DIM
L_CMP = 32
D_CMP = 16
CMP_HID = 128
L_SEL = 64
N_SEL = 16
WINDOW = 512
Q_BLK = 128
ROT_DIM = HEAD_DIM // 4
ROPE_THETA = 500000.0

D_FF = -(-8 * D_MODEL // (3 * 256)) * 256
NORM_EPS = 1e-6
NEG = -1e30
FORCE = 1e6

RWKV_COLS = 3 * RWKV_W + DECAY_RANK + AAA_RANK + GATE_RANK
NSA_COLS = NSA_W + 6 * KV_W + 3 * NSA_HEADS
IN_COLS = RWKV_COLS + NSA_COLS + 2 * D_MODEL
RWKV_SPLITS = (RWKV_W, RWKV_W + DECAY_RANK, 2 * RWKV_W + DECAY_RANK, 3 * RWKV_W + DECAY_RANK, 3 * RWKV_W + DECAY_RANK + AAA_RANK)
NSA_SPLITS = tuple(NSA_W + i * KV_W for i in range(7))
IN_SPLITS = (RWKV_COLS, RWKV_COLS + NSA_COLS, RWKV_COLS + NSA_COLS + D_MODEL)

kernel_name = 'rwkv7_nsa_gated_hybrid_step'


def rms_norm(x, g):
    xf = x.astype(jnp.float32)
    y = xf * lax.rsqrt(jnp.mean(xf * xf, axis=-1, keepdims=True) + NORM_EPS)
    return (y * g.astype(jnp.float32)).astype(x.dtype)


def partial_rope(x, pos):
    half = ROT_DIM // 2
    inv = ROPE_THETA ** (-jnp.arange(half, dtype=jnp.float32) / half)
    ang = pos.astype(jnp.float32)[:, None] * inv[None, :]
    cos = jnp.cos(ang)[None, :, None, :]
    sin = jnp.sin(ang)[None, :, None, :]
    xf = x.astype(jnp.float32)
    x1, x2 = xf[..., :half], xf[..., half:ROT_DIM]
    out = jnp.concatenate([x1 * cos - x2 * sin, x2 * cos + x1 * sin, xf[..., ROT_DIM:]], axis=-1)
    return out.astype(x.dtype)


def masked_softmax(s, mask):
    p = jax.nn.softmax(jnp.where(mask, s.astype(jnp.float32), NEG), axis=-1)
    return jnp.where(mask, p, 0.0)


def rwkv7_scan(S0, r, decay, k, v, a_vec, b_vec):
    def step(S, inp):
        r_t, w_t, k_t, v_t, a_t, b_t = inp
        sa = jnp.einsum('bhij,bhj->bhi', S, a_t)
        S = S * w_t[:, :, None, :] + sa[..., None] * b_t[:, :, None, :] + v_t[..., None] * k_t[:, :, None, :]
        return S, jnp.einsum('bhij,bhj->bhi', S, r_t)
    xs = tuple(jnp.moveaxis(z, 1, 0) for z in (r, decay, k, v, a_vec, b_vec))
    S, y = lax.scan(step, S0, xs)
    return jnp.moveaxis(y, 0, 1), S


def rwkv7_time_mix(p, prev_row, S0, lp):
    B, T, _ = p.shape
    f32 = jnp.float32
    shifted = jnp.concatenate([prev_row[:, None, :].astype(p.dtype), p[:, :-1]], axis=1)
    xs = p + (shifted - p) * lp['mu']
    r, wlo, k, v, alo, glo = jnp.split(xs, RWKV_SPLITS, axis=-1)
    w = -jax.nn.softplus(-(lp['w0'] + jnp.tanh(wlo) @ lp['w_decay'])) - 0.5
    decay = jnp.exp(-jnp.exp(w.astype(f32)))
    a = jax.nn.sigmoid(lp['a0'] + alo @ lp['w_aaa'])
    g = jax.nn.sigmoid(glo) @ lp['w_gate']
    hd = lambda z: z.astype(f32).reshape(B, T, RWKV_HEADS, RWKV_HEAD)
    kk = hd(k * lp['k_k'])
    kk = kk * lax.rsqrt(jnp.maximum(jnp.sum(kk * kk, axis=-1, keepdims=True), 1e-24))
    k_h = hd(k * (1 + (a - 1) * lp['k_a']))
    r_h, v_h, a_h = hd(r), hd(v), hd(a)
    y, S = rwkv7_scan(S0.astype(f32), r_h, hd(decay), k_h, v_h, -kk, kk * a_h)
    mean = jnp.mean(y, axis=-1, keepdims=True)
    var = jnp.mean(jnp.square(y - mean), axis=-1, keepdims=True)
    y = ((y - mean) * lax.rsqrt(var + RWKV_GN_EPS)).reshape(B, T, RWKV_W) * lp['ln_w'] + lp['ln_b']
    bonus = jnp.sum(r_h * k_h * lp['r_k'], axis=-1, keepdims=True) * v_h
    out = (y + bonus.reshape(B, T, RWKV_W)) * g
    return out.astype(p.dtype), p[:, -1], S


def compress_blocks(kv, pe, w1, w2):
    B, T = kv.shape[:2]
    n_chunk = T // D_CMP
    ch = kv[:, :n_chunk * D_CMP].reshape(B, n_chunk, D_CMP, 2, NSA_KV, HEAD_DIM)
    pe_t = jnp.swapaxes(pe, 0, 1)[:, :, None, :]
    h_first = jnp.einsum('bnlcgd,cldh->bncgh', ch + pe_t[:D_CMP], w1[:, :D_CMP])
    h_second = jnp.einsum('bnlcgd,cldh->bncgh', ch + pe_t[D_CMP:], w1[:, D_CMP:])
    h = jax.nn.silu(h_first[:, :-1] + h_second[:, 1:])
    out = jnp.einsum('bncgh,chd->bncgd', h, w2)
    return out[:, :, 0], out[:, :, 1]


def nsa_compress_select(q, pos, kv_cmp, lp):
    B, tq = q.shape[:2]
    T_all = kv_cmp.shape[1]
    kc, vc = compress_blocks(kv_cmp, lp['nsa_pe_cmp'], lp['nsa_w_cmp1'], lp['nsa_w_cmp2'])
    n_c = kc.shape[1]
    n_s = -(-T_all // L_SEL)
    qg = q.reshape(B, tq, NSA_KV, HPG, HEAD_DIM)
    s = jnp.einsum('btghd,bngd->bghtn', qg, kc) * HEAD_DIM ** -0.5
    ends = jnp.arange(n_c) * D_CMP + (L_CMP - 1)
    p = masked_softmax(s, ends[None, :] <= pos[:, None])
    o_c = jnp.einsum('bghtn,bngd->btghd', p, vc.astype(jnp.float32)).reshape(B, tq, NSA_HEADS, HEAD_DIM)
    ci = np.arange(n_c)[:, None]
    sj = np.arange(n_s)[None, :]
    overlap = jnp.asarray(((ci * D_CMP < (sj + 1) * L_SEL) & (ci * D_CMP + L_CMP > sj * L_SEL)).astype(np.float32))
    imp = jnp.einsum('bghtn,ns->bgts', p, overlap)
    cur = (pos // L_SEL)[None, None, :, None]
    j = jnp.arange(n_s)[None, None, None, :]
    score = jnp.where((j == 0) | (j == cur) | (j == cur - 1), FORCE, jnp.where(j <= cur, imp, -1.0))
    _, idx = lax.top_k(score, min(N_SEL, n_s))
    valid = idx <= cur
    return o_c, idx, valid, n_s


def selection_blocks(kv, n_s):
    B, T = kv.shape[:2]
    kv = jnp.pad(kv, ((0, 0), (0, n_s * L_SEL - T), (0, 0), (0, 0), (0, 0)))
    kv = kv.reshape(B, n_s, L_SEL, 2, NSA_KV, HEAD_DIM).transpose(3, 0, 4, 1, 2, 5)
    return kv[0], kv[1]


def selection_attn(q, pos, idx, valid, kb, vb):
    B, tq = q.shape[:2]
    qg = q.reshape(B, tq, NSA_KV, HPG, HEAD_DIM)
    bi = jnp.arange(B)[:, None, None, None]
    gi = jnp.arange(NSA_KV)[None, :, None, None]
    kg = kb[bi, gi, idx]
    vg = vb[bi, gi, idx]
    n_k = idx.shape[-1]
    s = jnp.einsum('btghd,bgtnld->bghtnl', qg, kg) * HEAD_DIM ** -0.5
    tok = idx[..., None] * L_SEL + jnp.arange(L_SEL)
    mask = (valid[..., None] & (tok <= pos[None, None, :, None, None]))[:, :, None]
    p = masked_softmax(s.reshape(B, NSA_KV, HPG, tq, n_k * L_SEL), mask.reshape(B, NSA_KV, 1, tq, n_k * L_SEL))
    o = jnp.einsum('bghtm,bgtmd->btghd', p, vg.reshape(B, NSA_KV, tq, n_k * L_SEL, HEAD_DIM).astype(jnp.float32))
    return o.reshape(B, tq, NSA_HEADS, HEAD_DIM)


def window_attn(q, pos, kv, kpos):
    B, tq = q.shape[:2]
    qg = q.reshape(B, tq, NSA_KV, HPG, HEAD_DIM)
    s = jnp.einsum('btghd,bsgd->bghts', qg, kv[:, :, 0]) * HEAD_DIM ** -0.5
    mask = (kpos[None, :] <= pos[:, None]) & (kpos[None, :] > pos[:, None] - WINDOW) & (kpos[None, :] >= 0)
    p = masked_softmax(s, mask)
    o = jnp.einsum('bghts,bsgd->btghd', p, kv[:, :, 1].astype(jnp.float32))
    return o.reshape(B, tq, NSA_HEADS, HEAD_DIM)


def nsa_prompt(q, pos, kv_cmp, kv_slc, kv_win, lp):
    B, T = q.shape[:2]
    o_c, idx, valid, n_s = nsa_compress_select(q, pos, kv_cmp, lp)
    q_r = partial_rope(q, pos)
    kb, vb = selection_blocks(kv_slc, n_s)
    kw = jnp.pad(kv_win, ((0, 0), (WINDOW, 0), (0, 0), (0, 0), (0, 0)))

    def query_block(nb):
        start = nb * Q_BLK
        qb = lax.dynamic_slice_in_dim(q_r, start, Q_BLK, axis=1)
        pb = start + jnp.arange(Q_BLK)
        o_s = selection_attn(qb, pb, lax.dynamic_slice_in_dim(idx, start, Q_BLK, axis=2),
                             lax.dynamic_slice_in_dim(valid, start, Q_BLK, axis=2), kb, vb)
        kwb = lax.dynamic_slice_in_dim(kw, start, WINDOW + Q_BLK, axis=1)
        o_w = window_attn(qb, pb, kwb, start - WINDOW + jnp.arange(WINDOW + Q_BLK))
        return o_s, o_w

    o_s, o_w = lax.map(query_block, jnp.arange(T // Q_BLK))
    unblock = lambda o: jnp.moveaxis(o, 0, 1).reshape(B, T, NSA_HEADS, HEAD_DIM)
    return o_c, unblock(o_s), unblock(o_w)


def nsa_sample(q, pos, kv_cmp, kv_slc, kv_win, win_pos, lp):
    o_c, idx, valid, n_s = nsa_compress_select(q, pos, kv_cmp, lp)
    q_r = partial_rope(q, pos)
    kb, vb = selection_blocks(kv_slc, n_s)
    return o_c, selection_attn(q_r, pos, idx, valid, kb, vb), window_attn(q_r, pos, kv_win, win_pos)


def gather_pages(cache, page_table):
    g = cache[page_table]
    return g.reshape(page_table.shape[0], -1, *cache.shape[2:])


def trunk_layer(x, pos, lp, S0, shift0, past):
    B, T, _ = x.shape
    h = rms_norm(x, lp['g_mix'])
    proj = h @ lp['w_in']
    p_rwkv, p_nsa, gate_r, gate_n = jnp.split(proj, IN_SPLITS, axis=-1)
    y_r, shift_new, S_new = rwkv7_time_mix(p_rwkv, shift0, S0, lp)
    q, kc, vc, ks, vs, kw, vw, g_nsa = jnp.split(p_nsa, NSA_SPLITS, axis=-1)
    q = q.reshape(B, T, NSA_HEADS, HEAD_DIM)
    kvh = lambda z: z.reshape(B, T, NSA_KV, HEAD_DIM)
    new_cmp = jnp.stack([kvh(kc), kvh(vc)], axis=2)
    new_slc = jnp.stack([partial_rope(kvh(ks), pos), kvh(vs)], axis=2)
    new_win = jnp.stack([partial_rope(kvh(kw), pos), kvh(vw)], axis=2)
    if past is None:
        o_c, o_s, o_w = nsa_prompt(q, pos, new_cmp, new_slc, new_win, lp)
        win_state = new_win[:, T - min(WINDOW, T):]
    else:
        cmp_past, slc_past, win_buf, buf_pos = past
        win_all = jnp.concatenate([win_buf.astype(new_win.dtype), new_win], axis=1)
        win_pos = jnp.concatenate([buf_pos, pos])
        o_c, o_s, o_w = nsa_sample(q, pos, jnp.concatenate([cmp_past.astype(new_cmp.dtype), new_cmp], axis=1),
                                   jnp.concatenate([slc_past.astype(new_slc.dtype), new_slc], axis=1),
                                   win_all, win_pos, lp)
        n_all = win_all.shape[1]
        win_state = win_all[:, n_all - min(WINDOW, n_all):]
    gts = jax.nn.sigmoid(g_nsa.astype(jnp.float32)).reshape(B, T, NSA_HEADS, 3)
    o = gts[..., 0:1] * o_c + gts[..., 1:2] * o_s + gts[..., 2:3] * o_w
    y_n = o.reshape(B, T, NSA_W).astype(x.dtype)
    merged = jax.nn.sigmoid(gate_r) * (y_r @ lp['w_br_rwkv']) + jax.nn.sigmoid(gate_n) * (y_n @ lp['w_br_nsa'])
    x = x + merged @ lp['w_out']
    h2 = rms_norm(x, lp['g_ffn'])
    x = x + (jax.nn.silu(h2 @ lp['w_ffn_gate']) * (h2 @ lp['w_ffn_up'])) @ lp['w_ffn_down']
    return x, (new_cmp, new_slc, win_state, S_new, shift_new)


def setup_inputs(seed: int = 0) -> dict:
    key = jax.random.key(seed)
    keys = jax.random.split(key, 40)
    ks = iter([keys[i] for i in range(40)])
    nrm = lambda shape, scale: scale * jax.random.normal(next(ks), shape, jnp.float32)
    uni = lambda shape, lo, hi: jax.random.uniform(next(ks), shape, jnp.float32, lo, hi)
    n_pages = PAST_LEN // PAGE_SIZE
    n_used = DEC_BATCH * n_pages
    n_pool = n_used + (n_used + 3) // 4
    w_buf = min(WINDOW, PAST_LEN)
    out = {}
    out['x_prompt'] = nrm((BATCH, SEQ, D_MODEL), 1.0)
    out['x_sample'] = nrm((DEC_BATCH, DEC_SEQ, D_MODEL), 1.0)
    out['cache_cmp_kv'] = nrm((DEPTH, n_pool, PAGE_SIZE, 2, NSA_KV, HEAD_DIM), 1.0)
    out['cache_slc_kv'] = nrm((DEPTH, n_pool, PAGE_SIZE, 2, NSA_KV, HEAD_DIM), 1.0)
    out['cache_win_kv'] = nrm((DEPTH, DEC_BATCH, w_buf, 2, NSA_KV, HEAD_DIM), 1.0)
    out['state_rwkv'] = nrm((DEPTH, DEC_BATCH, RWKV_HEADS, RWKV_HEAD, RWKV_HEAD), 0.5)
    out['state_rwkv_shift'] = nrm((DEPTH, DEC_BATCH, RWKV_COLS), 1.0)
    perm = jax.random.permutation(next(ks), n_pool)
    out['page_table'] = perm[:n_used].reshape(DEC_BATCH, n_pages).astype(jnp.int32)
    out['g_mix'] = 1.0 + nrm((DEPTH, D_MODEL), 0.02)
    out['w_in'] = nrm((DEPTH, D_MODEL, IN_COLS), D_MODEL ** -0.5)
    out['rwkv_mu'] = uni((DEPTH, RWKV_COLS), 0.0, 1.0)
    out['rwkv_w0'] = uni((DEPTH, RWKV_W), -5.0, -0.5)
    out['rwkv_w_decay'] = nrm((DEPTH, DECAY_RANK, RWKV_W), 0.5 * DECAY_RANK ** -0.5)
    out['rwkv_a0'] = nrm((DEPTH, RWKV_W), 0.5)
    out['rwkv_w_aaa'] = nrm((DEPTH, AAA_RANK, RWKV_W), AAA_RANK ** -0.5)
    out['rwkv_w_gate'] = nrm((DEPTH, GATE_RANK, RWKV_W), GATE_RANK ** -0.5)
    out['rwkv_k_k'] = 0.85 + nrm((DEPTH, RWKV_W), 0.05)
    out['rwkv_k_a'] = 1.0 + nrm((DEPTH, RWKV_W), 0.05)
    out['rwkv_r_k'] = nrm((DEPTH, RWKV_HEADS, RWKV_HEAD), 0.1)
    out['rwkv_ln_w'] = 1.0 + nrm((DEPTH, RWKV_W), 0.02)
    out['rwkv_ln_b'] = nrm((DEPTH, RWKV_W), 0.01)
    out['nsa_pe_cmp'] = nrm((DEPTH, 2, L_CMP, HEAD_DIM), 0.1)
    out['nsa_w_cmp1'] = nrm((DEPTH, 2, L_CMP, HEAD_DIM, CMP_HID), (L_CMP * HEAD_DIM) ** -0.5)
    out['nsa_w_cmp2'] = nrm((DEPTH, 2, CMP_HID, HEAD_DIM), CMP_HID ** -0.5)
    out['w_br_rwkv'] = nrm((DEPTH, RWKV_W, D_MODEL), RWKV_W ** -0.5)
    out['w_br_nsa'] = nrm((DEPTH, NSA_W, D_MODEL), NSA_W ** -0.5)
    out['w_out'] = nrm((DEPTH, D_MODEL, D_MODEL), D_MODEL ** -0.5)
    out['g_ffn'] = 1.0 + nrm((DEPTH, D_MODEL), 0.02)
    out['w_ffn_gate'] = nrm((DEPTH, D_MODEL, D_FF), D_MODEL ** -0.5)
    out['w_ffn_up'] = nrm((DEPTH, D_MODEL, D_FF), D_MODEL ** -0.5)
    out['w_ffn_down'] = nrm((DEPTH, D_FF, D_MODEL), D_FF ** -0.5)
    out['g_final'] = 1.0 + nrm((D_MODEL,), 0.02)
    return out


def reference(x_prompt, x_sample, cache_cmp_kv, cache_slc_kv, cache_win_kv, state_rwkv, state_rwkv_shift,
              page_table, g_mix, w_in, rwkv_mu, rwkv_w0, rwkv_w_decay, rwkv_a0, rwkv_w_aaa, rwkv_w_gate,
              rwkv_k_k, rwkv_k_a, rwkv_r_k, rwkv_ln_w, rwkv_ln_b, nsa_pe_cmp, nsa_w_cmp1, nsa_w_cmp2,
              w_br_rwkv, w_br_nsa, w_out, g_ffn, w_ffn_gate, w_ffn_up, w_ffn_down, g_final):
    B, T = x_prompt.shape[:2]
    TS = x_sample.shape[1]
    past_len = page_table.shape[1] * cache_cmp_kv.shape[2]
    pos_p = jnp.arange(T)
    pos_s = past_len + jnp.arange(TS)
    w_buf = cache_win_kv.shape[2]
    buf_pos = past_len - w_buf + jnp.arange(w_buf)
    xp, xs = x_prompt, x_sample
    outs_p, outs_s = [], []
    for l in range(DEPTH):
        lp = {'g_mix': g_mix[l], 'w_in': w_in[l], 'mu': rwkv_mu[l], 'w0': rwkv_w0[l],
              'w_decay': rwkv_w_decay[l], 'a0': rwkv_a0[l], 'w_aaa': rwkv_w_aaa[l], 'w_gate': rwkv_w_gate[l],
              'k_k': rwkv_k_k[l], 'k_a': rwkv_k_a[l], 'r_k': rwkv_r_k[l], 'ln_w': rwkv_ln_w[l], 'ln_b': rwkv_ln_b[l],
              'nsa_pe_cmp': nsa_pe_cmp[l], 'nsa_w_cmp1': nsa_w_cmp1[l], 'nsa_w_cmp2': nsa_w_cmp2[l],
              'w_br_rwkv': w_br_rwkv[l], 'w_br_nsa': w_br_nsa[l], 'w_out': w_out[l], 'g_ffn': g_ffn[l],
              'w_ffn_gate': w_ffn_gate[l], 'w_ffn_up': w_ffn_up[l], 'w_ffn_down': w_ffn_down[l]}
        xp, st_p = trunk_layer(xp, pos_p, lp, jnp.zeros((B, RWKV_HEADS, RWKV_HEAD, RWKV_HEAD), jnp.float32),
                               jnp.zeros((B, RWKV_COLS), xp.dtype), None)
        past = (gather_pages(cache_cmp_kv[l], page_table), gather_pages(cache_slc_kv[l], page_table),
                cache_win_kv[l], buf_pos)
        xs, st_s = trunk_layer(xs, pos_s, lp, state_rwkv[l], state_rwkv_shift[l], past)
        outs_p.append(st_p)
        outs_s.append(st_s)
    stack = lambda outs, i: jnp.stack([o[i] for o in outs])
    y_prompt = rms_norm(xp, g_final)
    y_sample = rms_norm(xs, g_final)
    return (y_prompt, y_sample,
            stack(outs_p, 0), stack(outs_p, 1), stack(outs_p, 2), stack(outs_p, 3), stack(outs_p, 4),
            stack(outs_s, 0), stack(outs_s, 1), stack(outs_s, 2), stack(outs_s, 3), stack(outs_s, 4))
```

```python
import functools

import jax
import jax.numpy as jnp
import numpy as np
from jax import lax
from jax.experimental import pallas as pl
from jax.experimental.pallas import tpu as pltpu

F32 = jnp.float32
BF16 = jnp.bfloat16

D_MODEL = 1024
RWKV_HEADS = 8
RWKV_HEAD = 64
RWKV_W = RWKV_HEADS * RWKV_HEAD
DECAY_RANK = 64
AAA_RANK = 64
GATE_RANK = 160
RWKV_GN_EPS = 64e-5
NSA_HEADS = 8
NSA_KV = 2
HEAD_DIM = 64
HPG = NSA_HEADS // NSA_KV
NSA_W = NSA_HEADS * HEAD_DIM
KV_W = NSA_KV * HEAD_DIM
L_CMP = 32
D_CMP = 16
CMP_HID = 128
L_SEL = 64
N_SEL = 16
WINDOW = 512
ROT_DIM = HEAD_DIM // 4
ROPE_THETA = 500000.0
D_FF = -(-8 * D_MODEL // (3 * 256)) * 256
NORM_EPS = 1e-6
NEG = -1e30
FORCE = 1e6
RWKV_COLS = 3 * RWKV_W + DECAY_RANK + AAA_RANK + GATE_RANK
NSA_COLS = NSA_W + 6 * KV_W + 3 * NSA_HEADS

LANES = 128
RW_PAD = 1920
GN_PAD = LANES
KV_ROW = 2 * KV_W
NBLK_PAD = 64
VMEM_LIMIT = 56 * 1024 * 1024


def _cparams(sem):
    return pltpu.CompilerParams(dimension_semantics=sem, vmem_limit_bytes=VMEM_LIMIT)


def _const_spec(shape):
    nd = len(shape)
    return pl.BlockSpec(shape, lambda *_: (0,) * nd, pipeline_mode=pl.Buffered(1))


def _dot(a, b):
    return jnp.dot(a.astype(BF16), b.astype(BF16), preferred_element_type=F32)


def _dot_nt(a, b):
    return lax.dot_general(a.astype(BF16), b.astype(BF16), (((1,), (1,)), ((), ())),
                           preferred_element_type=F32)


def _dot2(a, b):
    hi = a.astype(BF16)
    lo = (a - hi.astype(F32)).astype(BF16)
    return (jnp.dot(hi, b, preferred_element_type=F32)
            + jnp.dot(lo, b, preferred_element_type=F32))


def _sigmoid(x):
    return 1.0 / (1.0 + jnp.exp(-x))


def _silu(x):
    return x * _sigmoid(x)


def _rope128(x, cos, s1, s2):
    half = ROT_DIM // 2
    return x * cos + pltpu.roll(x, LANES - half, 1) * s1 + pltpu.roll(x, half, 1) * s2


_C_Q = RW_PAD
_C_CMP = _C_Q + NSA_W
_C_SLC = _C_CMP + KV_ROW
_C_WIN = _C_SLC + KV_ROW
_C_GN = _C_WIN + KV_ROW
_C_GATE = _C_GN + GN_PAD
_C_END = _C_GATE + 2 * D_MODEL


def _proj_body(x_ref, g_ref, w_ref, cos_ref, s1_ref, s2_ref,
               prw_ref, q_ref, qr_ref, cmp_ref, slc_ref, win_ref, gn_ref, gates_ref):
    x = x_ref[...]
    ms = jnp.mean(x * x, axis=-1, keepdims=True)
    h = (x * lax.rsqrt(ms + NORM_EPS) * g_ref[...]).astype(BF16)

    def mm(a, b):
        return jnp.dot(h, w_ref[:, a:b], preferred_element_type=F32)

    cos, s1, s2 = cos_ref[...], s1_ref[...], s2_ref[...]
    prw_ref[...] = mm(0, _C_Q)
    q = mm(_C_Q, _C_CMP)
    q_ref[...] = q
    for c in range(NSA_W // LANES):
        sl = slice(c * LANES, (c + 1) * LANES)
        qr_ref[:, sl] = _rope128(q[:, sl], cos, s1, s2)
    cmp_ref[...] = mm(_C_CMP, _C_SLC)
    slc = mm(_C_SLC, _C_WIN)
    slc_ref[:, :KV_W] = _rope128(slc[:, :KV_W], cos, s1, s2)
    slc_ref[:, KV_W:] = slc[:, KV_W:]
    win = mm(_C_WIN, _C_GN)
    win_ref[:, :KV_W] = _rope128(win[:, :KV_W], cos, s1, s2)
    win_ref[:, KV_W:] = win[:, KV_W:]
    gn_ref[...] = mm(_C_GN, _C_GATE)
    gates_ref[...] = mm(_C_GATE, _C_END)


def _in_proj(x2d, g_mix, w_r, tabs, tm):
    m = x2d.shape[0]
    tm = min(tm, m)
    widths = (RW_PAD, NSA_W, NSA_W, KV_ROW, KV_ROW, KV_ROW, GN_PAD, 2 * D_MODEL)
    row = lambda w: pl.BlockSpec((tm, w), lambda i: (i, 0))
    return pl.pallas_call(
        _proj_body,
        grid=(m // tm,),
        in_specs=[row(D_MODEL), _const_spec((1, D_MODEL)), _const_spec((D_MODEL, _C_END)),
                  row(LANES), row(LANES), row(LANES)],
        out_specs=[row(w) for w in widths],
        out_shape=[jax.ShapeDtypeStruct((m, w), F32) for w in widths],
        compiler_params=_cparams(("parallel",)),
        name="in_proj",
    )(x2d, g_mix, w_r, *tabs)


def _rwkv_pre_body(p_ref, sh_ref, mu_ref, wda_ref, w0_ref, a0_ref, wg_ref, kk_ref, ka_ref,
                   rk_ref, bd_ref, rT_ref, wT_ref, kT_ref, aT_ref, bT_ref, v_ref, bonus_ref,
                   g_ref, carry_ref):
    j = pl.program_id(1)
    bb, tt, c = p_ref.shape
    n = bb * tt
    x3 = p_ref[...]
    x = x3.reshape(n, c)
    prev = jnp.where(j == 0, sh_ref[...], carry_ref[...])
    carry_ref[...] = x3[:, tt - 1:tt, :]
    prev_rows = jnp.broadcast_to(prev, (bb, tt, c)).reshape(n, c)
    row = lax.broadcasted_iota(jnp.int32, (n, c), 0)
    shifted = jnp.where(row % tt == 0, prev_rows, pltpu.roll(x, 1, 0))
    xs = x + (shifted - x) * mu_ref[...]
    r = xs[:, 0:RWKV_W]
    k = xs[:, RWKV_W:2 * RWKV_W]
    v = xs[:, 2 * RWKV_W:3 * RWKV_W]
    wa = xs[:, 3 * RWKV_W:3 * RWKV_W + LANES]
    gl = xs[:, 3 * RWKV_W + LANES:RW_PAD]
    lane = lax.broadcasted_iota(jnp.int32, wa.shape, 1)
    z = _dot(jnp.where(lane < DECAY_RANK, jnp.tanh(wa), wa), wda_ref[...])
    u = -(w0_ref[...] + z[:, :RWKV_W])
    softplus = jnp.maximum(u, 0.0) + jnp.log1p(jnp.exp(-jnp.abs(u)))
    decay = jnp.exp(-jnp.exp(-softplus - 0.5))
    a = _sigmoid(a0_ref[...] + z[:, RWKV_W:])
    g_ref[...] = _dot(_sigmoid(gl), wg_ref[...])
    bd = bd_ref[...]
    kk = k * kk_ref[...]
    kk = kk * lax.rsqrt(jnp.maximum(_dot2(kk * kk, bd), 1e-24))
    k_h = k * (1.0 + (a - 1.0) * ka_ref[...])
    bonus_ref[...] = _dot2(r * k_h * rk_ref[...], bd) * v
    v_ref[...] = v
    rT_ref[...] = r.T
    wT_ref[...] = decay.T
    kT_ref[...] = k_h.T
    aT_ref[...] = (-kk).T
    bT_ref[...] = (kk * a).T


def _rwkv_pre(p3, shift0, prm, bb, tt):
    b, t, _ = p3.shape
    m = b * t
    n = bb * tt
    nt = t // tt
    colT = pl.BlockSpec((RWKV_W, n), lambda i, j: (0, i * nt + j))
    rowm = pl.BlockSpec((n, RWKV_W), lambda i, j: (i * nt + j, 0))
    outs = pl.pallas_call(
        _rwkv_pre_body,
        grid=(b // bb, nt),
        in_specs=[pl.BlockSpec((bb, tt, RW_PAD), lambda i, j: (i, j, 0)),
                  pl.BlockSpec((bb, 1, RW_PAD), lambda i, j: (i, 0, 0)),
                  _const_spec((1, RW_PAD)), _const_spec((LANES, 2 * RWKV_W)),
                  _const_spec((1, RWKV_W)), _const_spec((1, RWKV_W)),
                  _const_spec((RW_PAD - 3 * RWKV_W - LANES, RWKV_W)),
                  _const_spec((1, RWKV_W)), _const_spec((1, RWKV_W)), _const_spec((1, RWKV_W)),
                  _const_spec((RWKV_W, RWKV_W))],
        out_specs=[colT] * 5 + [rowm] * 3,
        out_shape=[jax.ShapeDtypeStruct((RWKV_W, m), F32)] * 5
                  + [jax.ShapeDtypeStruct((m, RWKV_W), F32)] * 3,
        scratch_shapes=[pltpu.VMEM((bb, 1, RW_PAD), F32)],
        compiler_params=_cparams(("parallel", "arbitrary")),
        name="rwkv_pre",
    )(p3, shift0, prm["mu"], prm["wda"], prm["w0"], prm["a0"], prm["wg"], prm["k_k"],
      prm["k_a"], prm["r_k"], prm["bd"])
    return outs


def _scan_body(rT_ref, wT_ref, kT_ref, aT_ref, bT_ref, v_ref, s0_ref, y_ref, s_ref, *, seq_tokens):
    first_tile = pl.program_id(2) == 0
    left = lax.broadcasted_iota(jnp.int32, (RWKV_HEAD, LANES), 1) < RWKV_HEAD

    def bcast(ref, t):
        c0 = jnp.broadcast_to(ref[0:RWKV_HEAD, t:t + 1], (RWKV_HEAD, LANES))
        c1 = jnp.broadcast_to(ref[RWKV_HEAD:LANES, t:t + 1], (RWKV_HEAD, LANES))
        return jnp.where(left, c0, c1)

    for s in range(LANES // seq_tokens):
        st = jnp.where(first_tile, s0_ref[s], s_ref[s])
        for tl in range(seq_tokens):
            t = s * seq_tokens + tl
            sa = jnp.sum(st * bcast(aT_ref, t), axis=0, keepdims=True)
            st = st * bcast(wT_ref, t) + bcast(bT_ref, t) * sa + bcast(kT_ref, t) * v_ref[t:t + 1, :]
            y_ref[t:t + 1, :] = jnp.sum(st * bcast(rT_ref, t), axis=0, keepdims=True)
        s_ref[s] = st


def _rwkv_scan(colsT, v, s0t, seq_tokens):
    m = v.shape[0]
    b = s0t.shape[0]
    npair = RWKV_HEADS // 2
    nseq = LANES // min(seq_tokens, LANES)
    lt = min(seq_tokens, LANES)
    ntile = seq_tokens // lt
    ngrp = b // nseq
    colT = pl.BlockSpec((LANES, LANES), lambda i, p, c: (p, i * ntile + c))
    rowm = pl.BlockSpec((LANES, LANES), lambda i, p, c: (i * ntile + c, p))
    sspec = pl.BlockSpec((nseq, None, RWKV_HEAD, LANES), lambda i, p, c: (i, p, 0, 0))
    return pl.pallas_call(
        functools.partial(_scan_body, seq_tokens=lt),
        grid=(ngrp, npair, ntile),
        in_specs=[colT] * 5 + [rowm, sspec],
        out_specs=[rowm, sspec],
        out_shape=[jax.ShapeDtypeStruct((m, RWKV_W), F32),
                   jax.ShapeDtypeStruct(s0t.shape, F32)],
        compiler_params=_cparams(("parallel", "parallel", "arbitrary")),
        name="rwkv_scan",
    )(*colsT, v, s0t)


def _compress_body(*refs, n_src):
    pe_ref, w1_ref, w2_ref, o_ref = refs[-4:]
    x_refs = refs[-4 - 2 * n_src:-4]
    per = x_refs[0].shape[0] // D_CMP
    for c in range(2):
        srcs = x_refs[c * n_src:(c + 1) * n_src]
        for l in range(D_CMP):
            xl = jnp.concatenate([xr[pl.ds(l, per, stride=D_CMP), :] for xr in srcs], axis=0)
            f = _dot(xl + pe_ref[c, l:l + 1, :], w1_ref[c, l])
            s = _dot(xl + pe_ref[c, D_CMP + l:D_CMP + l + 1, :], w1_ref[c, D_CMP + l])
            hf = f if l == 0 else hf + f
            hs = s if l == 0 else hs + s
        nc = hf.shape[0]
        h = _silu(hf + pltpu.roll(hs, nc - 1, 0))
        o_ref[:, c * KV_W:(c + 1) * KV_W] = _dot(h, w2_ref[c])


def _compress_weights(pe, w1, w2):
    eye = jnp.eye(NSA_KV, dtype=F32)
    w1b = jnp.einsum("cldh,gy->clgdyh", w1, eye).reshape(2, L_CMP, KV_W, NSA_KV * CMP_HID)
    w2b = jnp.einsum("chd,gy->cghyd", w2, eye).reshape(2, NSA_KV * CMP_HID, KV_W)
    peb = jnp.broadcast_to(pe[:, :, None, :], (2, L_CMP, NSA_KV, HEAD_DIM)).reshape(2, L_CMP, KV_W)
    return (peb, w1b.astype(BF16), w2b.astype(BF16))


def _compress_call(n_src, nb, nc, in_specs_x, cw, n_prefetch, args):
    hid = NSA_KV * CMP_HID
    wspecs = [pl.BlockSpec((2, L_CMP, KV_W), lambda *_: (0, 0, 0)),
              pl.BlockSpec((2, L_CMP, KV_W, hid), lambda *_: (0, 0, 0, 0)),
              pl.BlockSpec((2, hid, KV_W), lambda *_: (0, 0, 0))]
    return pl.pallas_call(
        functools.partial(_compress_body, n_src=n_src),
        grid_spec=pltpu.PrefetchScalarGridSpec(
            num_scalar_prefetch=n_prefetch, grid=(nb,), in_specs=in_specs_x + wspecs,
            out_specs=pl.BlockSpec((None, nc, KV_ROW), lambda b, *_: (b, 0, 0))),
        out_shape=jax.ShapeDtypeStruct((nb, nc, KV_ROW), F32),
        compiler_params=_cparams(("parallel",)),
        name="nsa_compress",
    )(*args, *cw)


def _compress_rows(rows3, cw):
    b, t, _ = rows3.shape
    xs = [pl.BlockSpec((None, t, KV_W), functools.partial(lambda c, i: (i, 0, c), c)) for c in range(2)]
    return _compress_call(1, b, t // D_CMP, xs, cw, 0, (rows3, rows3))


def _compress_paged(cache, page_table, cw):
    b, n_pages = page_table.shape
    page = cache.shape[1]
    xs = [pl.BlockSpec((None, page, KV_W), functools.partial(lambda c, k, i, pt: (pt[i, k], 0, c), c, k))
          for c in range(2) for k in range(n_pages)]
    return _compress_call(n_pages, b, n_pages * page // D_CMP, xs, cw, 1,
                          (page_table,) + (cache,) * (2 * n_pages))


def _head_to_half(q_ref, h, g):
    c = q_ref[:, (h // 2) * LANES:(h // 2 + 1) * LANES]
    if h % 2 != g:
        c = pltpu.roll(c, HEAD_DIM, 1)
    lane = lax.broadcasted_iota(jnp.int32, c.shape, 1)
    return jnp.where((lane >= g * HEAD_DIM) & (lane < (g + 1) * HEAD_DIM), c, 0.0)


def _halves_to_heads(o_ref, outs, g):
    for pr in range(HPG // 2):
        tiles = []
        for hl in (2 * pr, 2 * pr + 1):
            h = g * HPG + hl
            o = outs[hl]
            if h % 2 != g:
                o = pltpu.roll(o, HEAD_DIM, 1)
            tiles.append(o)
        lane = lax.broadcasted_iota(jnp.int32, tiles[0].shape, 1)
        c = (g * HPG) // 2 + pr
        o_ref[:, c * LANES:(c + 1) * LANES] = jnp.where(lane < HEAD_DIM, tiles[0], tiles[1])


def _cmp_attn_body(q_ref, kcv_ref, ov_ref, oc_ref, sel_ref, *, tq, pos0, n_c):
    qi = pl.program_id(1)
    ncp = kcv_ref.shape[0]
    kc = kcv_ref[:, :KV_W]
    vc = kcv_ref[:, KV_W:]
    row = lax.broadcasted_iota(jnp.int32, (tq, ncp), 0)
    col = lax.broadcasted_iota(jnp.int32, (tq, ncp), 1)
    pos = pos0 + qi * tq + row
    mask = (col * D_CMP + (L_CMP - 1) <= pos) & (col < n_c)
    imp = None
    for g in range(NSA_KV):
        outs = []
        psum = None
        for hl in range(HPG):
            qz = _head_to_half(q_ref, g * HPG + hl, g)
            s = jnp.where(mask, _dot_nt(qz, kc) * HEAD_DIM ** -0.5, NEG)
            e = jnp.exp(s - jnp.max(s, axis=-1, keepdims=True))
            p = jnp.where(mask, e / jnp.sum(e, axis=-1, keepdims=True), 0.0)
            outs.append(_dot(p, vc))
            psum = p if psum is None else psum + p
        _halves_to_heads(oc_ref, outs, g)
        ig = _dot2(psum, ov_ref[g])
        imp = ig if imp is None else imp + ig
    lane = lax.broadcasted_iota(jnp.int32, (tq, LANES), 1)
    jb = lane % NBLK_PAD
    cur = (pos0 + qi * tq + lax.broadcasted_iota(jnp.int32, (tq, LANES), 0)) // L_SEL
    forced = (jb == 0) | (jb == cur) | (jb == cur - 1)
    score = jnp.where(forced, FORCE, jnp.where(jb <= cur, imp, -1.0))
    left = lane < NBLK_PAD
    rank = jnp.zeros((tq, LANES), F32)
    for jp in range(NBLK_PAD):
        c0 = jnp.broadcast_to(score[:, jp:jp + 1], (tq, LANES))
        c1 = jnp.broadcast_to(score[:, NBLK_PAD + jp:NBLK_PAD + jp + 1], (tq, LANES))
        other = jnp.where(left, c0, c1)
        ahead = (other > score) | ((other == score) & (jb > jp))
        rank = rank + ahead.astype(F32)
    sel_ref[...] = ((rank < N_SEL) & (jb <= cur)).astype(F32)


def _overlap(ncp, n_c):
    ci = np.arange(ncp)[:, None]
    sj = np.arange(NBLK_PAD)[None, :]
    ov = ((ci * D_CMP < (sj + 1) * L_SEL) & (ci * D_CMP + L_CMP > sj * L_SEL) & (ci < n_c)).astype(np.float32)
    z = np.zeros_like(ov)
    return jnp.asarray(np.stack([np.concatenate([ov, z], 1), np.concatenate([z, ov], 1)]), BF16)


def _cmp_attn(q3, kcv, tq, pos0, n_c):
    b, t, _ = q3.shape
    ncp = kcv.shape[1]
    return pl.pallas_call(
        functools.partial(_cmp_attn_body, tq=tq, pos0=pos0, n_c=n_c),
        grid=(b, t // tq),
        in_specs=[pl.BlockSpec((None, tq, NSA_W), lambda i, j: (i, j, 0)),
                  pl.BlockSpec((None, ncp, KV_ROW), lambda i, j: (i, 0, 0)),
                  pl.BlockSpec((NSA_KV, ncp, LANES), lambda i, j: (0, 0, 0))],
        out_specs=[pl.BlockSpec((None, tq, NSA_W), lambda i, j: (i, j, 0)),
                   pl.BlockSpec((None, tq, LANES), lambda i, j: (i, j, 0))],
        out_shape=[jax.ShapeDtypeStruct((b, t, NSA_W), F32), jax.ShapeDtypeStruct((b, t, LANES), F32)],
        compiler_params=_cparams(("parallel", "parallel")),
        name="nsa_cmp_attn",
    )(q3, kcv, _overlap(ncp, n_c))


def _gate_expand():
    e = np.zeros((GN_PAD, 3 * NSA_W), np.float32)
    for h in range(NSA_HEADS):
        for k in range(3):
            e[h * 3 + k, k * NSA_W + h * HEAD_DIM:k * NSA_W + (h + 1) * HEAD_DIM] = 1.0
    return jnp.asarray(e, BF16)


def _combine(y_ref, gn_ref, ge_ref, oc, osel, owin):
    gates = _dot2(_sigmoid(gn_ref[...]), ge_ref[...])
    y_ref[...] = (gates[:, :NSA_W] * oc + gates[:, NSA_W:2 * NSA_W] * osel + gates[:, 2 * NSA_W:] * owin)


def _per_head(allow):
    f = jnp.where(allow, 1.0, 0.0)
    return jnp.concatenate([f] * HPG, axis=0) > 0.5


def _sel_mask_tile(sel, g, kv0, tk):
    r = lax.broadcasted_iota(jnp.int32, (LANES, tk), 0)
    blk = (kv0 + lax.broadcasted_iota(jnp.int32, (LANES, tk), 1)) // L_SEL
    expand = (r == g * NBLK_PAD + blk).astype(BF16)
    return jnp.dot(sel.astype(BF16), expand, preferred_element_type=F32)


def _nsa_prompt_body(qr_ref, slc_ref, win_ref, sel_ref, oc_ref, gn_ref, ge_ref, y_ref,
                     os_ref, ow_ref, *, tq, tk, lw):
    qi = pl.program_id(1)
    q0 = qi * tq
    t_all = slc_ref.shape[0]
    nrow = HPG * tq
    sel = sel_ref[...]
    qpos = q0 + lax.broadcasted_iota(jnp.int32, (tq, 1), 0)
    w0 = jnp.clip(q0 + tq - lw, 0, t_all - lw)
    w0 = pl.multiple_of(w0, 8)
    for g in range(NSA_KV):
        qst = jnp.concatenate([_head_to_half(qr_ref, g * HPG + hl, g) for hl in range(HPG)], axis=0)
        qst = (qst * HEAD_DIM ** -0.5).astype(BF16)

        def kv_step(it, carry):
            m, l, acc = carry
            kv0 = pl.multiple_of(it * tk, tk)
            kt = slc_ref[pl.ds(kv0, tk), :KV_W]
            vt = slc_ref[pl.ds(kv0, tk), KV_W:]
            kpos = kv0 + lax.broadcasted_iota(jnp.int32, (tq, tk), 1)
            allow = _per_head((_sel_mask_tile(sel, g, kv0, tk) > 0.5) & (kpos <= qpos))
            s = jnp.where(allow, _dot_nt(qst, kt), NEG)
            m_new = jnp.maximum(m, jnp.max(s, axis=-1, keepdims=True))
            p = jnp.where(allow, jnp.exp(s - m_new), 0.0)
            alpha = jnp.exp(m - m_new)
            return (m_new, alpha * l + jnp.sum(p, axis=-1, keepdims=True), alpha * acc + _dot(p, vt))

        n_kv = (q0 + tq + tk - 1) // tk
        init = (jnp.full((nrow, 1), NEG, F32), jnp.zeros((nrow, 1), F32), jnp.zeros((nrow, LANES), F32))
        m, l, acc = lax.fori_loop(0, n_kv, kv_step, init)
        o = acc / l
        _halves_to_heads(os_ref, [o[hl * tq:(hl + 1) * tq] for hl in range(HPG)], g)

        kt = win_ref[pl.ds(w0, lw), :KV_W]
        vt = win_ref[pl.ds(w0, lw), KV_W:]
        kpos = w0 + lax.broadcasted_iota(jnp.int32, (tq, lw), 1)
        allow = _per_head((kpos <= qpos) & (kpos > qpos - WINDOW))
        s = jnp.where(allow, _dot_nt(qst, kt), NEG)
        e = jnp.where(allow, jnp.exp(s - jnp.max(s, axis=-1, keepdims=True)), 0.0)
        o = _dot(e, vt) / jnp.sum(e, axis=-1, keepdims=True)
        _halves_to_heads(ow_ref, [o[hl * tq:(hl + 1) * tq] for hl in range(HPG)], g)
    _combine(y_ref, gn_ref, ge_ref, oc_ref[...], os_ref[...], ow_ref[...])


def _nsa_prompt(qr3, slc3, win3, sel3, oc3, gn3, tq, tk):
    b, t, _ = qr3.shape
    lw = min(WINDOW + tq, t)
    qblk = lambda w: pl.BlockSpec((None, tq, w), lambda i, j: (i, j, 0))
    full = pl.BlockSpec((None, t, KV_ROW), lambda i, j: (i, 0, 0))
    return pl.pallas_call(
        functools.partial(_nsa_prompt_body, tq=tq, tk=tk, lw=lw),
        grid=(b, t // tq),
        in_specs=[qblk(NSA_W), full, full, qblk(LANES), qblk(NSA_W), qblk(GN_PAD),
                  pl.BlockSpec((GN_PAD, 3 * NSA_W), lambda i, j: (0, 0))],
        out_specs=qblk(NSA_W),
        out_shape=jax.ShapeDtypeStruct((b, t, NSA_W), F32),
        scratch_shapes=[pltpu.VMEM((tq, NSA_W), F32), pltpu.VMEM((tq, NSA_W), F32)],
        compiler_params=_cparams(("parallel", "arbitrary")),
        name="nsa_attn_prompt",
    )(qr3, slc3, win3, sel3, oc3, gn3, _gate_expand())


def _nsa_sample_body(*refs, n_pages, past_len):
    page_refs = refs[1:1 + n_pages]
    (qr_ref, snew_ref, wold_ref, wnew_ref, sel_ref, oc_ref, gn_ref, ge_ref, y_ref, os_ref, ow_ref) = refs[1 + n_pages:]
    tq = qr_ref.shape[0]
    page = page_refs[0].shape[0]
    wb = wold_ref.shape[0]
    sel = sel_ref[...]
    qpos = past_len + lax.broadcasted_iota(jnp.int32, (tq, 1), 0)
    heads = _per_head

    def attend(parts):
        scores = [jnp.where(allow, _dot_nt(qst, ref[:, :KV_W]), NEG) for ref, allow in parts]
        m = functools.reduce(jnp.maximum, [jnp.max(s, axis=-1, keepdims=True) for s in scores])
        l, acc = 0.0, 0.0
        for s, (ref, allow) in zip(scores, parts):
            e = jnp.where(allow, jnp.exp(s - m), 0.0)
            l = l + jnp.sum(e, axis=-1, keepdims=True)
            acc = acc + _dot(e, ref[:, KV_W:])
        o = acc / l
        return [o[hl * tq:(hl + 1) * tq] for hl in range(HPG)]

    kpos_new = past_len + lax.broadcasted_iota(jnp.int32, (tq, page), 1)
    causal_new = kpos_new <= qpos
    for g in range(NSA_KV):
        qst = jnp.concatenate([_head_to_half(qr_ref, g * HPG + hl, g) for hl in range(HPG)], axis=0)
        qst = (qst * HEAD_DIM ** -0.5).astype(BF16)
        parts = [(page_refs[k], heads(_sel_mask_tile(sel, g, k * page, page) > 0.5)) for k in range(n_pages)]
        parts.append((snew_ref, heads((_sel_mask_tile(sel, g, past_len, page) > 0.5) & causal_new)))
        _halves_to_heads(os_ref, attend(parts), g)
        kpos_old = past_len - wb + lax.broadcasted_iota(jnp.int32, (tq, wb), 1)
        parts = [(wold_ref, heads((kpos_old > qpos - WINDOW) & (kpos_old >= 0))),
                 (wnew_ref, heads(causal_new & (kpos_new > qpos - WINDOW)))]
        _halves_to_heads(ow_ref, attend(parts), g)
    _combine(y_ref, gn_ref, ge_ref, oc_ref[...], os_ref[...], ow_ref[...])


def _nsa_sample(qr3, slc_cache, page_table, snew3, wold3, wnew3, sel3, oc3, gn3):
    b, tq, _ = qr3.shape
    n_pages = page_table.shape[1]
    page = slc_cache.shape[1]
    wb = wold3.shape[1]
    per = lambda r, w: pl.BlockSpec((None, r, w), lambda i, pt: (i, 0, 0))
    pad = lambda a: jnp.pad(a, ((0, 0), (0, page - tq), (0, 0)))
    snew3, wnew3 = pad(snew3), pad(wnew3)
    pages = [pl.BlockSpec((None, page, KV_ROW), functools.partial(lambda k, i, pt: (pt[i, k], 0, 0), k))
             for k in range(n_pages)]
    return pl.pallas_call(
        functools.partial(_nsa_sample_body, n_pages=n_pages, past_len=n_pages * page),
        grid_spec=pltpu.PrefetchScalarGridSpec(
            num_scalar_prefetch=1, grid=(b,),
            in_specs=pages + [per(tq, NSA_W), per(page, KV_ROW), per(wb, KV_ROW), per(page, KV_ROW),
                              per(tq, LANES), per(tq, NSA_W), per(tq, GN_PAD),
                              pl.BlockSpec((GN_PAD, 3 * NSA_W), lambda i, pt: (0, 0))],
            out_specs=per(tq, NSA_W),
            scratch_shapes=[pltpu.VMEM((tq, NSA_W), F32), pltpu.VMEM((tq, NSA_W), F32)]),
        out_shape=jax.ShapeDtypeStruct((b, tq, NSA_W), F32),
        compiler_params=_cparams(("arbitrary",)),
        name="nsa_attn_sample",
    )(page_table, *([slc_cache] * n_pages), qr3, snew3, wold3, wnew3, sel3, oc3, gn3, _gate_expand())


def _final_body(x_ref, yraw_ref, bonus_ref, g_ref, yn_ref, gates_ref, lnw_ref, lnb_ref, bd_ref,
                wbr_ref, wbn_ref, wo_ref, gffn_ref, wg_ref, wu_ref, wd_ref, gfin_ref, y_ref):
    bd = bd_ref[...]
    y = yraw_ref[...]
    d = y - _dot2(y, bd) * (1.0 / RWKV_HEAD)
    var = _dot2(d * d, bd) * (1.0 / RWKV_HEAD)
    y_r = (d * lax.rsqrt(var + RWKV_GN_EPS) * lnw_ref[...] + lnb_ref[...] + bonus_ref[...]) * g_ref[...]
    merged = (_sigmoid(gates_ref[:, :D_MODEL]) * _dot(y_r, wbr_ref[...])
              + _sigmoid(gates_ref[:, D_MODEL:]) * _dot(yn_ref[...], wbn_ref[...]))
    x1 = x_ref[...] + _dot(merged, wo_ref[...])
    h2 = x1 * lax.rsqrt(jnp.mean(x1 * x1, axis=-1, keepdims=True) + NORM_EPS) * gffn_ref[...]
    hb = h2.astype(BF16)
    up = _silu(jnp.dot(hb, wg_ref[...], preferred_element_type=F32)) * jnp.dot(hb, wu_ref[...], preferred_element_type=F32)
    x2 = x1 + _dot(up, wd_ref[...])
    y_ref[...] = x2 * lax.rsqrt(jnp.mean(x2 * x2, axis=-1, keepdims=True) + NORM_EPS) * gfin_ref[...]


def _final(x2d, yraw, bonus, g, yn, gates, prm, tm):
    m = x2d.shape[0]
    tm = min(tm, m)
    row = lambda w: pl.BlockSpec((tm, w), lambda i: (i, 0))
    return pl.pallas_call(
        _final_body,
        grid=(m // tm,),
        in_specs=[row(D_MODEL), row(RWKV_W), row(RWKV_W), row(RWKV_W), row(NSA_W), row(2 * D_MODEL),
                  _const_spec((1, RWKV_W)), _const_spec((1, RWKV_W)), _const_spec((RWKV_W, RWKV_W)),
                  _const_spec((RWKV_W, D_MODEL)), _const_spec((NSA_W, D_MODEL)),
                  _const_spec((D_MODEL, D_MODEL)), _const_spec((1, D_MODEL)),
                  _const_spec((D_MODEL, D_FF)), _const_spec((D_MODEL, D_FF)), _const_spec((D_FF, D_MODEL)),
                  _const_spec((1, D_MODEL))],
        out_specs=row(D_MODEL),
        out_shape=jax.ShapeDtypeStruct((m, D_MODEL), F32),
        compiler_params=_cparams(("parallel",)),
        name="merge_ffn",
    )(x2d, yraw, bonus, g, yn, gates, prm["ln_w"], prm["ln_b"], prm["bd"], prm["w_br_rwkv"],
      prm["w_br_nsa"], prm["w_out"], prm["g_ffn"], prm["w_ffn_gate"], prm["w_ffn_up"],
      prm["w_ffn_down"], prm["g_final"])


_S = (RWKV_W, RWKV_W + DECAY_RANK, 2 * RWKV_W + DECAY_RANK, 3 * RWKV_W + DECAY_RANK,
      3 * RWKV_W + DECAY_RANK + AAA_RANK)
_RW_PERM = np.concatenate([np.arange(0, _S[0]), np.arange(_S[1], _S[2]), np.arange(_S[2], _S[3]),
                           np.arange(_S[0], _S[1]), np.arange(_S[3], _S[4]), np.arange(_S[4], RWKV_COLS)])
_RW_INV = np.argsort(_RW_PERM)


def _prep_params(l, g_mix, w_in, rwkv_mu, rwkv_w0, rwkv_w_decay, rwkv_a0, rwkv_w_aaa, rwkv_w_gate,
                 rwkv_k_k, rwkv_k_a, rwkv_r_k, rwkv_ln_w, rwkv_ln_b, w_br_rwkv, w_br_nsa, w_out, g_ffn,
                 w_ffn_gate, w_ffn_up, w_ffn_down, g_final):
    w = w_in[l]
    nsa0 = RWKV_COLS
    kv_end = nsa0 + NSA_W + 6 * KV_W
    zc = lambda n: jnp.zeros((D_MODEL, n), F32)
    w_r = jnp.concatenate([w[:, _RW_PERM], zc(RW_PAD - RWKV_COLS), w[:, nsa0:kv_end],
                           w[:, kv_end:nsa0 + NSA_COLS], zc(GN_PAD - 3 * NSA_HEADS),
                           w[:, nsa0 + NSA_COLS:]], axis=1).astype(BF16)
    mu = jnp.concatenate([rwkv_mu[l][_RW_PERM], jnp.zeros((RW_PAD - RWKV_COLS,), F32)])[None]
    wda = jnp.zeros((LANES, 2 * RWKV_W), F32)
    wda = wda.at[:DECAY_RANK, :RWKV_W].set(rwkv_w_decay[l]).at[DECAY_RANK:, RWKV_W:].set(rwkv_w_aaa[l])
    wg = jnp.zeros((RW_PAD - 3 * RWKV_W - LANES, RWKV_W), F32).at[:GATE_RANK].set(rwkv_w_gate[l])
    hid = np.arange(RWKV_W) // RWKV_HEAD
    bd = jnp.asarray(hid[:, None] == hid[None, :], BF16)
    r1 = lambda a: a.reshape(1, -1)
    return dict(
        g_mix=r1(g_mix[l]), w_r=w_r, mu=mu, wda=wda.astype(BF16), w0=r1(rwkv_w0[l]), a0=r1(rwkv_a0[l]),
        wg=wg.astype(BF16), k_k=r1(rwkv_k_k[l]), k_a=r1(rwkv_k_a[l]), r_k=r1(rwkv_r_k[l]), bd=bd,
        ln_w=r1(rwkv_ln_w[l]), ln_b=r1(rwkv_ln_b[l]), w_br_rwkv=w_br_rwkv[l].astype(BF16),
        w_br_nsa=w_br_nsa[l].astype(BF16), w_out=w_out[l].astype(BF16), g_ffn=r1(g_ffn[l]),
        w_ffn_gate=w_ffn_gate[l].astype(BF16), w_ffn_up=w_ffn_up[l].astype(BF16),
        w_ffn_down=w_ffn_down[l].astype(BF16), g_final=r1(g_final))


def _rope_tables(pos):
    half = ROT_DIM // 2
    inv = ROPE_THETA ** (-jnp.arange(half, dtype=F32) / half)
    ang = pos.astype(F32)[:, None] * inv[None, :]
    cos, sin = jnp.cos(ang), jnp.sin(ang)
    n = pos.shape[0]
    one, zero = jnp.ones((n, HEAD_DIM - ROT_DIM), F32), jnp.zeros((n, HEAD_DIM - ROT_DIM), F32)
    z8 = jnp.zeros((n, half), F32)
    c = jnp.concatenate([cos, cos, one], 1)
    s1 = jnp.concatenate([-sin, z8, zero], 1)
    s2 = jnp.concatenate([z8, sin, zero], 1)
    return tuple(jnp.concatenate([a, a], 1) for a in (c, s1, s2))


def _state_to_pairs(s):
    b = s.shape[0]
    return s.reshape(b, RWKV_HEADS // 2, 2, RWKV_HEAD, RWKV_HEAD).transpose(0, 1, 4, 2, 3).reshape(
        b, RWKV_HEADS // 2, RWKV_HEAD, LANES)


def _pairs_to_state(st):
    b = st.shape[0]
    return st.reshape(b, RWKV_HEADS // 2, RWKV_HEAD, 2, RWKV_HEAD).transpose(0, 1, 3, 4, 2).reshape(
        b, RWKV_HEADS, RWKV_HEAD, RWKV_HEAD)


def _group(x, pos, shift0, s0, prm, cw, past, tm, pre_blk, tq):
    b, t, _ = x.shape
    m = b * t
    x2d = x.reshape(m, D_MODEL)
    tabs = _rope_tables(jnp.tile(pos, b))
    p_rw, q, q_r, cmp_rows, slc_rows, win_rows, gn, gates = _in_proj(x2d, prm["g_mix"], prm["w_r"], tabs, tm)

    shift_r = jnp.concatenate([shift0[:, _RW_PERM], jnp.zeros((b, RW_PAD - RWKV_COLS), F32)], 1)[:, None, :]
    rT, wT, kT, aT, bT, v, bonus, g = _rwkv_pre(p_rw.reshape(b, t, RW_PAD), shift_r, prm, *pre_blk)
    y_raw, st = _rwkv_scan((rT, wT, kT, aT, bT), v, _state_to_pairs(s0), t)
    shift_new = p_rw.reshape(b, t, RW_PAD)[:, -1, :RWKV_COLS][:, _RW_INV]

    r3 = lambda a: a.reshape(b, t, a.shape[-1])
    if past is None:
        kcv = _compress_rows(r3(cmp_rows), cw)
        n_c = t // D_CMP - 1
        oc, sel = _cmp_attn(r3(q), kcv, tq, 0, n_c)
        y_n = _nsa_prompt(r3(q_r), r3(slc_rows), r3(win_rows), sel, oc, r3(gn), tq, min(512, t))
        win_state = r3(win_rows)[:, t - min(WINDOW, t):]
    else:
        cmp_cache, slc_cache, win_cache, page_table = past
        past_len = page_table.shape[1] * cmp_cache.shape[1]
        kcv = _compress_paged(cmp_cache, page_table, cw)
        n_c = (past_len + t) // D_CMP - 1
        oc, sel = _cmp_attn(r3(q), kcv, tq, past_len, n_c)
        y_n = _nsa_sample(r3(q_r), slc_cache, page_table, r3(slc_rows), win_cache, r3(win_rows), sel, oc, r3(gn))
        win_all = jnp.concatenate([win_cache, r3(win_rows)], axis=1)
        win_state = win_all[:, win_all.shape[1] - min(WINDOW, win_all.shape[1]):]

    y = _final(x2d, y_raw, bonus, g, y_n.reshape(m, NSA_W), gates, prm, tm)
    kv6 = lambda a: a.reshape(b, -1, 2, NSA_KV, HEAD_DIM)
    return (y.reshape(b, t, D_MODEL), kv6(r3(cmp_rows)), kv6(r3(slc_rows)), kv6(win_state),
            _pairs_to_state(st), shift_new)


def kernel(x_prompt, x_sample, cache_cmp_kv, cache_slc_kv, cache_win_kv, state_rwkv, state_rwkv_shift, page_table, g_mix, w_in, rwkv_mu, rwkv_w0, rwkv_w_decay, rwkv_a0, rwkv_w_aaa, rwkv_w_gate, rwkv_k_k, rwkv_k_a, rwkv_r_k, rwkv_ln_w, rwkv_ln_b, nsa_pe_cmp, nsa_w_cmp1, nsa_w_cmp2, w_br_rwkv, w_br_nsa, w_out, g_ffn, w_ffn_gate, w_ffn_up, w_ffn_down, g_final):
    depth = w_in.shape[0]
    assert depth == 1, "single-layer trunk"
    b, t, _ = x_prompt.shape
    bs, ts, _ = x_sample.shape
    n_pool, page = cache_cmp_kv.shape[1:3]
    past_len = page_table.shape[1] * page
    assert (past_len + ts) // D_CMP == past_len // D_CMP and past_len % D_CMP == 0
    prm = _prep_params(0, g_mix, w_in, rwkv_mu, rwkv_w0, rwkv_w_decay, rwkv_a0, rwkv_w_aaa, rwkv_w_gate,
                       rwkv_k_k, rwkv_k_a, rwkv_r_k, rwkv_ln_w, rwkv_ln_b, w_br_rwkv, w_br_nsa, w_out,
                       g_ffn, w_ffn_gate, w_ffn_up, w_ffn_down, g_final)
    cw = _compress_weights(nsa_pe_cmp[0], nsa_w_cmp1[0], nsa_w_cmp2[0])
    flat = lambda c: c.reshape(n_pool, page, KV_ROW)

    outs_p = _group(x_prompt, jnp.arange(t), jnp.zeros((b, RWKV_COLS), F32),
                    jnp.zeros((b, RWKV_HEADS, RWKV_HEAD, RWKV_HEAD), F32), prm, cw, None,
                    tm=256, pre_blk=(1, min(512, t)), tq=min(128, t))
    wcache = cache_win_kv[0].reshape(bs, -1, KV_ROW)
    outs_s = _group(x_sample, past_len + jnp.arange(ts), state_rwkv_shift[0], state_rwkv[0], prm, cw,
                    (flat(cache_cmp_kv[0]), flat(cache_slc_kv[0]), wcache, page_table),
                    tm=256, pre_blk=(min(64, bs), ts), tq=ts)
    return (outs_p[0], outs_s[0]) + tuple(o[None] for o in outs_p[1:]) + tuple(o[None] for o in outs_s[1:])
```

```python
import functools

import jax
import jax.numpy as jnp
import numpy as np
from jax import lax
from jax.experimental import pallas as pl
from jax.experimental.pallas import tpu as pltpu

F32 = jnp.float32
BF16 = jnp.bfloat16

D_MODEL = 1024
RWKV_HEADS = 8
RWKV_HEAD = 64
RWKV_W = RWKV_HEADS * RWKV_HEAD
DECAY_RANK = 64
AAA_RANK = 64
GATE_RANK = 160
RWKV_GN_EPS = 64e-5
NSA_HEADS = 8
NSA_KV = 2
HEAD_DIM = 64
HPG = NSA_HEADS // NSA_KV
NSA_W = NSA_HEADS * HEAD_DIM
KV_W = NSA_KV * HEAD_DIM
L_CMP = 32
D_CMP = 16
CMP_HID = 128
L_SEL = 64
N_SEL = 16
WINDOW = 512
ROT_DIM = HEAD_DIM // 4
ROPE_THETA = 500000.0
D_FF = -(-8 * D_MODEL // (3 * 256)) * 256
NORM_EPS = 1e-6
NEG = -1e30
FORCE = 1e6
RWKV_COLS = 3 * RWKV_W + DECAY_RANK + AAA_RANK + GATE_RANK
NSA_COLS = NSA_W + 6 * KV_W + 3 * NSA_HEADS

LANES = 128
RW_PAD = 1920
GN_PAD = LANES
KV_ROW = 2 * KV_W
NBLK_PAD = 64
VMEM_LIMIT = 56 * 1024 * 1024


def _cparams(sem):
    return pltpu.CompilerParams(dimension_semantics=sem, vmem_limit_bytes=VMEM_LIMIT)


def _const_spec(shape):
    nd = len(shape)
    return pl.BlockSpec(shape, lambda *_: (0,) * nd, pipeline_mode=pl.Buffered(1))


def _dot(a, b):
    return jnp.dot(a.astype(BF16), b.astype(BF16), preferred_element_type=F32)


def _dot_nt(a, b):
    return lax.dot_general(a.astype(BF16), b.astype(BF16), (((1,), (1,)), ((), ())),
                           preferred_element_type=F32)


def _dot2(a, b):
    hi = a.astype(BF16)
    lo = (a - hi.astype(F32)).astype(BF16)
    return (jnp.dot(hi, b, preferred_element_type=F32)
            + jnp.dot(lo, b, preferred_element_type=F32))


def _sigmoid(x):
    return 1.0 / (1.0 + jnp.exp(-x))


def _silu(x):
    return x * _sigmoid(x)


def _rope128(x, cos, s1, s2):
    half = ROT_DIM // 2
    return x * cos + pltpu.roll(x, LANES - half, 1) * s1 + pltpu.roll(x, half, 1) * s2


_C_Q = RW_PAD
_C_CMP = _C_Q + NSA_W
_C_SLC = _C_CMP + KV_ROW
_C_WIN = _C_SLC + KV_ROW
_C_GN = _C_WIN + KV_ROW
_C_GATE = _C_GN + GN_PAD
_C_END = _C_GATE + 2 * D_MODEL


def _proj_body(x_ref, g_ref, w_ref, cos_ref, s1_ref, s2_ref,
               prw_ref, q_ref, qr_ref, cmp_ref, slc_ref, win_ref, gn_ref, gates_ref):
    x = x_ref[...]
    ms = jnp.mean(x * x, axis=-1, keepdims=True)
    h = (x * lax.rsqrt(ms + NORM_EPS) * g_ref[...]).astype(BF16)

    def mm(a, b):
        return jnp.dot(h, w_ref[:, a:b], preferred_element_type=F32)

    cos, s1, s2 = cos_ref[...], s1_ref[...], s2_ref[...]
    prw_ref[...] = mm(0, _C_Q)
    q = mm(_C_Q, _C_CMP)
    q_ref[...] = q
    for c in range(NSA_W // LANES):
        sl = slice(c * LANES, (c + 1) * LANES)
        qr_ref[:, sl] = _rope128(q[:, sl], cos, s1, s2)
    cmp_ref[...] = mm(_C_CMP, _C_SLC)
    slc = mm(_C_SLC, _C_WIN)
    slc_ref[:, :KV_W] = _rope128(slc[:, :KV_W], cos, s1, s2)
    slc_ref[:, KV_W:] = slc[:, KV_W:]
    win = mm(_C_WIN, _C_GN)
    win_ref[:, :KV_W] = _rope128(win[:, :KV_W], cos, s1, s2)
    win_ref[:, KV_W:] = win[:, KV_W:]
    gn_ref[...] = mm(_C_GN, _C_GATE)
    gates_ref[...] = mm(_C_GATE, _C_END)


def _in_proj(x2d, g_mix, w_r, tabs, tm):
    m = x2d.shape[0]
    tm = min(tm, m)
    widths = (RW_PAD, NSA_W, NSA_W, KV_ROW, KV_ROW, KV_ROW, GN_PAD, 2 * D_MODEL)
    row = lambda w: pl.BlockSpec((tm, w), lambda i: (i, 0))
    return pl.pallas_call(
        _proj_body,
        grid=(m // tm,),
        in_specs=[row(D_MODEL), _const_spec((1, D_MODEL)), _const_spec((D_MODEL, _C_END)),
                  row(LANES), row(LANES), row(LANES)],
        out_specs=[row(w) for w in widths],
        out_shape=[jax.ShapeDtypeStruct((m, w), F32) for w in widths],
        compiler_params=_cparams(("parallel",)),
        name="in_proj",
    )(x2d, g_mix, w_r, *tabs)


def _rwkv_pre_body(p_ref, sh_ref, mu_ref, wda_ref, w0_ref, a0_ref, wg_ref, kk_ref, ka_ref,
                   rk_ref, bd_ref, rT_ref, wT_ref, kT_ref, aT_ref, bT_ref, v_ref, bonus_ref,
                   g_ref, carry_ref, *, transposed):
    j = pl.program_id(1)
    bb, tt, c = p_ref.shape
    n = bb * tt
    x3 = p_ref[...]
    x = x3.reshape(n, c)
    prev = jnp.where(j == 0, sh_ref[...], carry_ref[...])
    carry_ref[...] = x3[:, tt - 1:tt, :]
    prev_rows = jnp.broadcast_to(prev, (bb, tt, c)).reshape(n, c)
    row = lax.broadcasted_iota(jnp.int32, (n, c), 0)
    shifted = jnp.where(row % tt == 0, prev_rows, pltpu.roll(x, 1, 0))
    xs = x + (shifted - x) * mu_ref[...]
    r = xs[:, 0:RWKV_W]
    k = xs[:, RWKV_W:2 * RWKV_W]
    v = xs[:, 2 * RWKV_W:3 * RWKV_W]
    wa = xs[:, 3 * RWKV_W:3 * RWKV_W + LANES]
    gl = xs[:, 3 * RWKV_W + LANES:RW_PAD]
    lane = lax.broadcasted_iota(jnp.int32, wa.shape, 1)
    z = _dot(jnp.where(lane < DECAY_RANK, jnp.tanh(wa), wa), wda_ref[...])
    u = -(w0_ref[...] + z[:, :RWKV_W])
    softplus = jnp.maximum(u, 0.0) + jnp.log1p(jnp.exp(-jnp.abs(u)))
    log_decay = -jnp.exp(-softplus - 0.5)
    a = _sigmoid(a0_ref[...] + z[:, RWKV_W:])
    g_ref[...] = _dot(_sigmoid(gl), wg_ref[...])
    bd = bd_ref[...]
    kk = k * kk_ref[...]
    kk = kk * lax.rsqrt(jnp.maximum(_dot2(kk * kk, bd), 1e-24))
    k_h = k * (1.0 + (a - 1.0) * ka_ref[...])
    bonus_ref[...] = _dot2(r * k_h * rk_ref[...], bd) * v
    v_ref[...] = v
    if transposed:
        rT_ref[...] = r.T
        wT_ref[...] = jnp.exp(log_decay).T
        kT_ref[...] = k_h.T
        aT_ref[...] = (-kk).T
        bT_ref[...] = (kk * a).T
    else:
        rT_ref[...] = r
        wT_ref[...] = log_decay
        kT_ref[...] = k_h
        aT_ref[...] = -kk
        bT_ref[...] = kk * a


def _rwkv_pre(p3, shift0, prm, bb, tt, transposed):
    b, t, _ = p3.shape
    m = b * t
    n = bb * tt
    nt = t // tt
    rowm = pl.BlockSpec((n, RWKV_W), lambda i, j: (i * nt + j, 0))
    colT = pl.BlockSpec((RWKV_W, n), lambda i, j: (0, i * nt + j)) if transposed else rowm
    scan_shape = (RWKV_W, m) if transposed else (m, RWKV_W)
    outs = pl.pallas_call(
        functools.partial(_rwkv_pre_body, transposed=transposed),
        grid=(b // bb, nt),
        in_specs=[pl.BlockSpec((bb, tt, RW_PAD), lambda i, j: (i, j, 0)),
                  pl.BlockSpec((bb, 1, RW_PAD), lambda i, j: (i, 0, 0)),
                  _const_spec((1, RW_PAD)), _const_spec((LANES, 2 * RWKV_W)),
                  _const_spec((1, RWKV_W)), _const_spec((1, RWKV_W)),
                  _const_spec((RW_PAD - 3 * RWKV_W - LANES, RWKV_W)),
                  _const_spec((1, RWKV_W)), _const_spec((1, RWKV_W)), _const_spec((1, RWKV_W)),
                  _const_spec((RWKV_W, RWKV_W))],
        out_specs=[colT] * 5 + [rowm] * 3,
        out_shape=[jax.ShapeDtypeStruct(scan_shape, F32)] * 5
                  + [jax.ShapeDtypeStruct((m, RWKV_W), F32)] * 3,
        scratch_shapes=[pltpu.VMEM((bb, 1, RW_PAD), F32)],
        compiler_params=_cparams(("parallel", "arbitrary")),
        name="rwkv_pre",
    )(p3, shift0, prm["mu"], prm["wda"], prm["w0"], prm["a0"], prm["wg"], prm["k_k"],
      prm["k_a"], prm["r_k"], prm["bd"])
    return outs


def _scan_body(rT_ref, wT_ref, kT_ref, aT_ref, bT_ref, v_ref, s0_ref, y_ref, s_ref, *, seq_tokens):
    first_tile = pl.program_id(2) == 0
    left = lax.broadcasted_iota(jnp.int32, (RWKV_HEAD, LANES), 1) < RWKV_HEAD

    def bcast(ref, t):
        c0 = jnp.broadcast_to(ref[0:RWKV_HEAD, t:t + 1], (RWKV_HEAD, LANES))
        c1 = jnp.broadcast_to(ref[RWKV_HEAD:LANES, t:t + 1], (RWKV_HEAD, LANES))
        return jnp.where(left, c0, c1)

    for s in range(LANES // seq_tokens):
        st = jnp.where(first_tile, s0_ref[s], s_ref[s])
        for tl in range(seq_tokens):
            t = s * seq_tokens + tl
            sa = jnp.sum(st * bcast(aT_ref, t), axis=0, keepdims=True)
            st = st * bcast(wT_ref, t) + bcast(bT_ref, t) * sa + bcast(kT_ref, t) * v_ref[t:t + 1, :]
            y_ref[t:t + 1, :] = jnp.sum(st * bcast(rT_ref, t), axis=0, keepdims=True)
        s_ref[s] = st


def _rwkv_scan(colsT, v, s0t, seq_tokens):
    m = v.shape[0]
    b = s0t.shape[0]
    npair = RWKV_HEADS // 2
    nseq = LANES // min(seq_tokens, LANES)
    lt = min(seq_tokens, LANES)
    ntile = seq_tokens // lt
    ngrp = b // nseq
    colT = pl.BlockSpec((LANES, LANES), lambda i, p, c: (p, i * ntile + c))
    rowm = pl.BlockSpec((LANES, LANES), lambda i, p, c: (i * ntile + c, p))
    sspec = pl.BlockSpec((nseq, None, RWKV_HEAD, LANES), lambda i, p, c: (i, p, 0, 0))
    return pl.pallas_call(
        functools.partial(_scan_body, seq_tokens=lt),
        grid=(ngrp, npair, ntile),
        in_specs=[colT] * 5 + [rowm, sspec],
        out_specs=[rowm, sspec],
        out_shape=[jax.ShapeDtypeStruct((m, RWKV_W), F32),
                   jax.ShapeDtypeStruct(s0t.shape, F32)],
        compiler_params=_cparams(("parallel", "parallel", "arbitrary")),
        name="rwkv_scan",
    )(*colsT, v, s0t)


RWKV_CHUNK = 64


def _split3(x):
    x1 = x.astype(BF16)
    r1 = x - x1.astype(F32)
    x2 = r1.astype(BF16)
    return x1, x2, (r1 - x2.astype(F32)).astype(BF16)


def _chunk_scan_body(r_ref, lw_ref, k_ref, a_ref, b_ref, v_ref, s0_ref, tri_ref, y_ref, s_ref):
    c = r_ref.shape[0]
    n = 2 * c
    first = pl.program_id(1) == 0
    left = lax.broadcasted_iota(jnp.int32, (c, LANES), 1) < RWKV_HEAD
    stack = lambda x: jnp.concatenate([jnp.where(left, x, 0.0), jnp.where(left, 0.0, x)], axis=0)
    ri = lax.broadcasted_iota(jnp.int32, (n, n), 0)
    ci = lax.broadcasted_iota(jnp.int32, (n, n), 1)
    same = (ri < c) == (ci < c)
    strict = same & (ci < ri)
    incl = same & (ci <= ri)
    eye = jnp.where(ri == ci, 1.0, 0.0)
    tri = tri_ref[...]
    pairs = range(RWKV_HEADS // 2)
    pre = []
    for p in pairs:
        sl = slice(p * LANES, (p + 1) * LANES)
        lw = lw_ref[:, sl]
        cs = sum(jnp.dot(tri, part, preferred_element_type=F32) for part in _split3(lw))
        c_end = cs[c - 1:c, :]
        e_neg = jnp.exp(-cs)
        e_hat = jnp.exp(c_end - cs)
        at = stack(a_ref[:, sl] * jnp.exp(cs - lw))
        rt = stack(r_ref[:, sl] * jnp.exp(cs))
        bt = stack(b_ref[:, sl] * e_neg)
        kt = stack(k_ref[:, sl] * e_neg)
        bk_hat = jnp.concatenate([stack(b_ref[:, sl] * e_hat), stack(k_ref[:, sl] * e_hat)], axis=0)
        g = _dot_nt(jnp.concatenate([at, rt], axis=0), jnp.concatenate([bt, kt], axis=0))
        pre.append(dict(sl=sl, e_end=jnp.exp(c_end), at=at, rt=rt, bk_hat=bk_hat.astype(BF16),
                        vs=stack(v_ref[:, sl]), low=jnp.where(strict, g[:n, :n], 0.0),
                        g_ak=jnp.where(strict, g[:n, n:], 0.0),
                        g_r=jnp.where(jnp.concatenate([incl, incl], axis=1), g[n:, :], 0.0)))
    tinv = [eye + q["low"] for q in pre]
    lp = [q["low"] for q in pre]
    for _ in range(c.bit_length() - 2):
        lp = [_dot(x, x) for x in lp]
        tinv = [t + _dot(t, x) for t, x in zip(tinv, lp)]
    state = [jnp.where(first, s0_ref[p], s_ref[p]) for p in pairs]
    rhs = [_dot_nt(q["at"], s) + _dot(q["g_ak"], q["vs"]) for q, s in zip(pre, state)]
    uv = [jnp.concatenate([_dot(t, x), q["vs"]], axis=0) for t, x, q in zip(tinv, rhs, pre)]
    for p, q, s, w in zip(pairs, pre, state, uv):
        ys = _dot_nt(q["rt"], s) + _dot(q["g_r"], w)
        y_ref[:, q["sl"]] = ys[:c] + ys[c:]
        s_ref[p] = s * q["e_end"] + lax.dot_general(
            w.astype(BF16), q["bk_hat"], (((0,), (0,)), ((), ())), preferred_element_type=F32)


def _rwkv_chunk_scan(rows, v, s0pair, t):
    m = v.shape[0]
    b = s0pair.shape[0]
    c = min(RWKV_CHUNK, t)
    nc = t // c
    rowm = pl.BlockSpec((c, RWKV_W), lambda i, j: (i * nc + j, 0))
    sspec = pl.BlockSpec((None, RWKV_HEADS // 2, LANES, LANES), lambda i, j: (i, 0, 0, 0))
    tri = jnp.asarray(np.tril(np.ones((c, c), np.float32)), BF16)
    return pl.pallas_call(
        _chunk_scan_body,
        grid=(b, nc),
        in_specs=[rowm] * 6 + [sspec, pl.BlockSpec((c, c), lambda i, j: (0, 0))],
        out_specs=[rowm, sspec],
        out_shape=[jax.ShapeDtypeStruct((m, RWKV_W), F32), jax.ShapeDtypeStruct(s0pair.shape, F32)],
        compiler_params=_cparams(("parallel", "arbitrary")),
        name="rwkv_chunk_scan",
    )(*rows, v, s0pair, tri)


def _state_to_blockdiag(s):
    b = s.shape[0]
    sp = s.reshape(b, RWKV_HEADS // 2, 2, RWKV_HEAD, RWKV_HEAD)
    z = jnp.zeros_like(sp[:, :, 0])
    top = jnp.concatenate([sp[:, :, 0], z], axis=-1)
    bot = jnp.concatenate([z, sp[:, :, 1]], axis=-1)
    return jnp.concatenate([top, bot], axis=-2)


def _blockdiag_to_state(sp):
    b = sp.shape[0]
    h0 = sp[:, :, :RWKV_HEAD, :RWKV_HEAD]
    h1 = sp[:, :, RWKV_HEAD:, RWKV_HEAD:]
    return jnp.stack([h0, h1], axis=2).reshape(b, RWKV_HEADS, RWKV_HEAD, RWKV_HEAD)


def _compress_body(*refs, n_src):
    pe_ref, w1_ref, w2_ref, o_ref = refs[-4:]
    x_refs = refs[-4 - 2 * n_src:-4]
    per = x_refs[0].shape[0] // D_CMP
    for c in range(2):
        srcs = x_refs[c * n_src:(c + 1) * n_src]
        for l in range(D_CMP):
            xl = jnp.concatenate([xr[pl.ds(l, per, stride=D_CMP), :] for xr in srcs], axis=0)
            f = _dot(xl + pe_ref[c, l:l + 1, :], w1_ref[c, l])
            s = _dot(xl + pe_ref[c, D_CMP + l:D_CMP + l + 1, :], w1_ref[c, D_CMP + l])
            hf = f if l == 0 else hf + f
            hs = s if l == 0 else hs + s
        nc = hf.shape[0]
        h = _silu(hf + pltpu.roll(hs, nc - 1, 0))
        o_ref[:, c * KV_W:(c + 1) * KV_W] = _dot(h, w2_ref[c])


def _compress_weights(pe, w1, w2):
    eye = jnp.eye(NSA_KV, dtype=F32)
    w1b = jnp.einsum("cldh,gy->clgdyh", w1, eye).reshape(2, L_CMP, KV_W, NSA_KV * CMP_HID)
    w2b = jnp.einsum("chd,gy->cghyd", w2, eye).reshape(2, NSA_KV * CMP_HID, KV_W)
    peb = jnp.broadcast_to(pe[:, :, None, :], (2, L_CMP, NSA_KV, HEAD_DIM)).reshape(2, L_CMP, KV_W)
    return (peb, w1b.astype(BF16), w2b.astype(BF16))


def _compress_call(n_src, nb, nc, in_specs_x, cw, n_prefetch, args):
    hid = NSA_KV * CMP_HID
    wspecs = [pl.BlockSpec((2, L_CMP, KV_W), lambda *_: (0, 0, 0)),
              pl.BlockSpec((2, L_CMP, KV_W, hid), lambda *_: (0, 0, 0, 0)),
              pl.BlockSpec((2, hid, KV_W), lambda *_: (0, 0, 0))]
    return pl.pallas_call(
        functools.partial(_compress_body, n_src=n_src),
        grid_spec=pltpu.PrefetchScalarGridSpec(
            num_scalar_prefetch=n_prefetch, grid=(nb,), in_specs=in_specs_x + wspecs,
            out_specs=pl.BlockSpec((None, nc, KV_ROW), lambda b, *_: (b, 0, 0))),
        out_shape=jax.ShapeDtypeStruct((nb, nc, KV_ROW), F32),
        compiler_params=_cparams(("parallel",)),
        name="nsa_compress",
    )(*args, *cw)


def _compress_rows(rows3, cw):
    b, t, _ = rows3.shape
    xs = [pl.BlockSpec((None, t, KV_W), functools.partial(lambda c, i: (i, 0, c), c)) for c in range(2)]
    return _compress_call(1, b, t // D_CMP, xs, cw, 0, (rows3, rows3))


def _compress_paged(cache, page_table, cw):
    b, n_pages = page_table.shape
    page = cache.shape[1]
    xs = [pl.BlockSpec((None, page, KV_W), functools.partial(lambda c, k, i, pt: (pt[i, k], 0, c), c, k))
          for c in range(2) for k in range(n_pages)]
    return _compress_call(n_pages, b, n_pages * page // D_CMP, xs, cw, 1,
                          (page_table,) + (cache,) * (2 * n_pages))


def _head_to_half(q_ref, h, g):
    c = q_ref[:, (h // 2) * LANES:(h // 2 + 1) * LANES]
    if h % 2 != g:
        c = pltpu.roll(c, HEAD_DIM, 1)
    lane = lax.broadcasted_iota(jnp.int32, c.shape, 1)
    return jnp.where((lane >= g * HEAD_DIM) & (lane < (g + 1) * HEAD_DIM), c, 0.0)


def _halves_to_heads(o_ref, outs, g):
    for pr in range(HPG // 2):
        tiles = []
        for hl in (2 * pr, 2 * pr + 1):
            h = g * HPG + hl
            o = outs[hl]
            if h % 2 != g:
                o = pltpu.roll(o, HEAD_DIM, 1)
            tiles.append(o)
        lane = lax.broadcasted_iota(jnp.int32, tiles[0].shape, 1)
        c = (g * HPG) // 2 + pr
        o_ref[:, c * LANES:(c + 1) * LANES] = jnp.where(lane < HEAD_DIM, tiles[0], tiles[1])


def _cmp_attn_body(q_ref, kcv_ref, ov_ref, oc_ref, sel_ref, *, tq, pos0, n_c):
    qi = pl.program_id(1)
    ncp = kcv_ref.shape[0]
    kc = kcv_ref[:, :KV_W]
    vc = kcv_ref[:, KV_W:]
    row = lax.broadcasted_iota(jnp.int32, (tq, ncp), 0)
    col = lax.broadcasted_iota(jnp.int32, (tq, ncp), 1)
    pos = pos0 + qi * tq + row
    mask = (col * D_CMP + (L_CMP - 1) <= pos) & (col < n_c)
    imp = None
    for g in range(NSA_KV):
        outs = []
        psum = None
        for hl in range(HPG):
            qz = _head_to_half(q_ref, g * HPG + hl, g)
            s = jnp.where(mask, _dot_nt(qz, kc) * HEAD_DIM ** -0.5, NEG)
            e = jnp.exp(s - jnp.max(s, axis=-1, keepdims=True))
            p = jnp.where(mask, e / jnp.sum(e, axis=-1, keepdims=True), 0.0)
            outs.append(_dot(p, vc))
            psum = p if psum is None else psum + p
        _halves_to_heads(oc_ref, outs, g)
        ig = _dot2(psum, ov_ref[g])
        imp = ig if imp is None else imp + ig
    lane = lax.broadcasted_iota(jnp.int32, (tq, LANES), 1)
    jb = lane % NBLK_PAD
    cur = (pos0 + qi * tq + lax.broadcasted_iota(jnp.int32, (tq, LANES), 0)) // L_SEL
    forced = (jb == 0) | (jb == cur) | (jb == cur - 1)
    score = jnp.where(forced, FORCE, jnp.where(jb <= cur, imp, -1.0))
    left = lane < NBLK_PAD
    rank = jnp.zeros((tq, LANES), F32)
    for jp in range(NBLK_PAD):
        c0 = jnp.broadcast_to(score[:, jp:jp + 1], (tq, LANES))
        c1 = jnp.broadcast_to(score[:, NBLK_PAD + jp:NBLK_PAD + jp + 1], (tq, LANES))
        other = jnp.where(left, c0, c1)
        ahead = (other > score) | ((other == score) & (jb > jp))
        rank = rank + ahead.astype(F32)
    sel_ref[...] = ((rank < N_SEL) & (jb <= cur)).astype(F32)


def _overlap(ncp, n_c):
    ci = np.arange(ncp)[:, None]
    sj = np.arange(NBLK_PAD)[None, :]
    ov = ((ci * D_CMP < (sj + 1) * L_SEL) & (ci * D_CMP + L_CMP > sj * L_SEL) & (ci < n_c)).astype(np.float32)
    z = np.zeros_like(ov)
    return jnp.asarray(np.stack([np.concatenate([ov, z], 1), np.concatenate([z, ov], 1)]), BF16)


def _cmp_attn(q3, kcv, tq, pos0, n_c):
    b, t, _ = q3.shape
    ncp = kcv.shape[1]
    return pl.pallas_call(
        functools.partial(_cmp_attn_body, tq=tq, pos0=pos0, n_c=n_c),
        grid=(b, t // tq),
        in_specs=[pl.BlockSpec((None, tq, NSA_W), lambda i, j: (i, j, 0)),
                  pl.BlockSpec((None, ncp, KV_ROW), lambda i, j: (i, 0, 0)),
                  pl.BlockSpec((NSA_KV, ncp, LANES), lambda i, j: (0, 0, 0))],
        out_specs=[pl.BlockSpec((None, tq, NSA_W), lambda i, j: (i, j, 0)),
                   pl.BlockSpec((None, tq, LANES), lambda i, j: (i, j, 0))],
        out_shape=[jax.ShapeDtypeStruct((b, t, NSA_W), F32), jax.ShapeDtypeStruct((b, t, LANES), F32)],
        compiler_params=_cparams(("parallel", "parallel")),
        name="nsa_cmp_attn",
    )(q3, kcv, _overlap(ncp, n_c))


def _gate_expand():
    e = np.zeros((GN_PAD, 3 * NSA_W), np.float32)
    for h in range(NSA_HEADS):
        for k in range(3):
            e[h * 3 + k, k * NSA_W + h * HEAD_DIM:k * NSA_W + (h + 1) * HEAD_DIM] = 1.0
    return jnp.asarray(e, BF16)


def _combine(y_ref, gn_ref, ge_ref, oc, osel, owin):
    gates = _dot2(_sigmoid(gn_ref[...]), ge_ref[...])
    y_ref[...] = (gates[:, :NSA_W] * oc + gates[:, NSA_W:2 * NSA_W] * osel + gates[:, 2 * NSA_W:] * owin)


def _per_head(allow):
    f = jnp.where(allow, 1.0, 0.0)
    return jnp.concatenate([f] * HPG, axis=0) > 0.5


def _sel_mask_tile(sel, g, kv0, tk):
    r = lax.broadcasted_iota(jnp.int32, (LANES, tk), 0)
    blk = (kv0 + lax.broadcasted_iota(jnp.int32, (LANES, tk), 1)) // L_SEL
    expand = (r == g * NBLK_PAD + blk).astype(BF16)
    return jnp.dot(sel.astype(BF16), expand, preferred_element_type=F32)


def _nsa_prompt_body(qr_ref, slc_ref, win_ref, sel_ref, oc_ref, gn_ref, ge_ref, y_ref,
                     os_ref, ow_ref, *, tq, tk, lw):
    qi = pl.program_id(1)
    q0 = qi * tq
    t_all = slc_ref.shape[0]
    nrow = HPG * tq
    sel = sel_ref[...]
    qpos = q0 + lax.broadcasted_iota(jnp.int32, (tq, 1), 0)
    w0 = jnp.clip(q0 + tq - lw, 0, t_all - lw)
    w0 = pl.multiple_of(w0, 8)
    for g in range(NSA_KV):
        qst = jnp.concatenate([_head_to_half(qr_ref, g * HPG + hl, g) for hl in range(HPG)], axis=0)
        qst = (qst * HEAD_DIM ** -0.5).astype(BF16)

        def kv_step(it, carry):
            m, l, acc = carry
            kv0 = pl.multiple_of(it * tk, tk)
            kt = slc_ref[pl.ds(kv0, tk), :KV_W]
            vt = slc_ref[pl.ds(kv0, tk), KV_W:]
            kpos = kv0 + lax.broadcasted_iota(jnp.int32, (tq, tk), 1)
            allow = _per_head((_sel_mask_tile(sel, g, kv0, tk) > 0.5) & (kpos <= qpos))
            s = jnp.where(allow, _dot_nt(qst, kt), NEG)
            m_new = jnp.maximum(m, jnp.max(s, axis=-1, keepdims=True))
            p = jnp.where(allow, jnp.exp(s - m_new), 0.0)
            alpha = jnp.exp(m - m_new)
            return (m_new, alpha * l + jnp.sum(p, axis=-1, keepdims=True), alpha * acc + _dot(p, vt))

        n_kv = (q0 + tq + tk - 1) // tk
        init = (jnp.full((nrow, 1), NEG, F32), jnp.zeros((nrow, 1), F32), jnp.zeros((nrow, LANES), F32))
        m, l, acc = lax.fori_loop(0, n_kv, kv_step, init)
        o = acc / l
        _halves_to_heads(os_ref, [o[hl * tq:(hl + 1) * tq] for hl in range(HPG)], g)

        kt = win_ref[pl.ds(w0, lw), :KV_W]
        vt = win_ref[pl.ds(w0, lw), KV_W:]
        kpos = w0 + lax.broadcasted_iota(jnp.int32, (tq, lw), 1)
        allow = _per_head((kpos <= qpos) & (kpos > qpos - WINDOW))
        s = jnp.where(allow, _dot_nt(qst, kt), NEG)
        e = jnp.where(allow, jnp.exp(s - jnp.max(s, axis=-1, keepdims=True)), 0.0)
        o = _dot(e, vt) / jnp.sum(e, axis=-1, keepdims=True)
        _halves_to_heads(ow_ref, [o[hl * tq:(hl + 1) * tq] for hl in range(HPG)], g)
    _combine(y_ref, gn_ref, ge_ref, oc_ref[...], os_ref[...], ow_ref[...])


def _nsa_prompt(qr3, slc3, win3, sel3, oc3, gn3, tq, tk):
    b, t, _ = qr3.shape
    lw = min(WINDOW + tq, t)
    qblk = lambda w: pl.BlockSpec((None, tq, w), lambda i, j: (i, j, 0))
    full = pl.BlockSpec((None, t, KV_ROW), lambda i, j: (i, 0, 0))
    return pl.pallas_call(
        functools.partial(_nsa_prompt_body, tq=tq, tk=tk, lw=lw),
        grid=(b, t // tq),
        in_specs=[qblk(NSA_W), full, full, qblk(LANES), qblk(NSA_W), qblk(GN_PAD),
                  pl.BlockSpec((GN_PAD, 3 * NSA_W), lambda i, j: (0, 0))],
        out_specs=qblk(NSA_W),
        out_shape=jax.ShapeDtypeStruct((b, t, NSA_W), F32),
        scratch_shapes=[pltpu.VMEM((tq, NSA_W), F32), pltpu.VMEM((tq, NSA_W), F32)],
        compiler_params=_cparams(("parallel", "arbitrary")),
        name="nsa_attn_prompt",
    )(qr3, slc3, win3, sel3, oc3, gn3, _gate_expand())


def _nsa_sample_body(*refs, n_pages, past_len):
    page_refs = refs[1:1 + n_pages]
    (qr_ref, snew_ref, wold_ref, wnew_ref, sel_ref, oc_ref, gn_ref, ge_ref, y_ref, os_ref, ow_ref) = refs[1 + n_pages:]
    tq = qr_ref.shape[0]
    page = page_refs[0].shape[0]
    wb = wold_ref.shape[0]
    sel = sel_ref[...]
    qpos = past_len + lax.broadcasted_iota(jnp.int32, (tq, 1), 0)
    heads = _per_head

    def attend(parts):
        scores = [jnp.where(allow, _dot_nt(qst, ref[:, :KV_W]), NEG) for ref, allow in parts]
        m = functools.reduce(jnp.maximum, [jnp.max(s, axis=-1, keepdims=True) for s in scores])
        l, acc = 0.0, 0.0
        for s, (ref, allow) in zip(scores, parts):
            e = jnp.where(allow, jnp.exp(s - m), 0.0)
            l = l + jnp.sum(e, axis=-1, keepdims=True)
            acc = acc + _dot(e, ref[:, KV_W:])
        o = acc / l
        return [o[hl * tq:(hl + 1) * tq] for hl in range(HPG)]

    kpos_new = past_len + lax.broadcasted_iota(jnp.int32, (tq, page), 1)
    causal_new = kpos_new <= qpos
    for g in range(NSA_KV):
        qst = jnp.concatenate([_head_to_half(qr_ref, g * HPG + hl, g) for hl in range(HPG)], axis=0)
        qst = (qst * HEAD_DIM ** -0.5).astype(BF16)
        parts = [(page_refs[k], heads(_sel_mask_tile(sel, g, k * page, page) > 0.5)) for k in range(n_pages)]
        parts.append((snew_ref, heads((_sel_mask_tile(sel, g, past_len, page) > 0.5) & causal_new)))
        _halves_to_heads(os_ref, attend(parts), g)
        kpos_old = past_len - wb + lax.broadcasted_iota(jnp.int32, (tq, wb), 1)
        parts = [(wold_ref, heads((kpos_old > qpos - WINDOW) & (kpos_old >= 0))),
                 (wnew_ref, heads(causal_new & (kpos_new > qpos - WINDOW)))]
        _halves_to_heads(ow_ref, attend(parts), g)
    _combine(y_ref, gn_ref, ge_ref, oc_ref[...], os_ref[...], ow_ref[...])


def _nsa_sample(qr3, slc_cache, page_table, snew3, wold3, wnew3, sel3, oc3, gn3):
    b, tq, _ = qr3.shape
    n_pages = page_table.shape[1]
    page = slc_cache.shape[1]
    wb = wold3.shape[1]
    per = lambda r, w: pl.BlockSpec((None, r, w), lambda i, pt: (i, 0, 0))
    pad = lambda a: jnp.pad(a, ((0, 0), (0, page - tq), (0, 0)))
    snew3, wnew3 = pad(snew3), pad(wnew3)
    pages = [pl.BlockSpec((None, page, KV_ROW), functools.partial(lambda k, i, pt: (pt[i, k], 0, 0), k))
             for k in range(n_pages)]
    return pl.pallas_call(
        functools.partial(_nsa_sample_body, n_pages=n_pages, past_len=n_pages * page),
        grid_spec=pltpu.PrefetchScalarGridSpec(
            num_scalar_prefetch=1, grid=(b,),
            in_specs=pages + [per(tq, NSA_W), per(page, KV_ROW), per(wb, KV_ROW), per(page, KV_ROW),
                              per(tq, LANES), per(tq, NSA_W), per(tq, GN_PAD),
                              pl.BlockSpec((GN_PAD, 3 * NSA_W), lambda i, pt: (0, 0))],
            out_specs=per(tq, NSA_W),
            scratch_shapes=[pltpu.VMEM((tq, NSA_W), F32), pltpu.VMEM((tq, NSA_W), F32)]),
        out_shape=jax.ShapeDtypeStruct((b, tq, NSA_W), F32),
        compiler_params=_cparams(("arbitrary",)),
        name="nsa_attn_sample",
    )(page_table, *([slc_cache] * n_pages), qr3, snew3, wold3, wnew3, sel3, oc3, gn3, _gate_expand())


def _final_body(x_ref, yraw_ref, bonus_ref, g_ref, yn_ref, gates_ref, lnw_ref, lnb_ref, bd_ref,
                wbr_ref, wbn_ref, wo_ref, gffn_ref, wg_ref, wu_ref, wd_ref, gfin_ref, y_ref):
    bd = bd_ref[...]
    y = yraw_ref[...]
    d = y - _dot2(y, bd) * (1.0 / RWKV_HEAD)
    var = _dot2(d * d, bd) * (1.0 / RWKV_HEAD)
    y_r = (d * lax.rsqrt(var + RWKV_GN_EPS) * lnw_ref[...] + lnb_ref[...] + bonus_ref[...]) * g_ref[...]
    merged = (_sigmoid(gates_ref[:, :D_MODEL]) * _dot(y_r, wbr_ref[...])
              + _sigmoid(gates_ref[:, D_MODEL:]) * _dot(yn_ref[...], wbn_ref[...]))
    x1 = x_ref[...] + _dot(merged, wo_ref[...])
    h2 = x1 * lax.rsqrt(jnp.mean(x1 * x1, axis=-1, keepdims=True) + NORM_EPS) * gffn_ref[...]
    hb = h2.astype(BF16)
    up = _silu(jnp.dot(hb, wg_ref[...], preferred_element_type=F32)) * jnp.dot(hb, wu_ref[...], preferred_element_type=F32)
    x2 = x1 + _dot(up, wd_ref[...])
    y_ref[...] = x2 * lax.rsqrt(jnp.mean(x2 * x2, axis=-1, keepdims=True) + NORM_EPS) * gfin_ref[...]


def _final(x2d, yraw, bonus, g, yn, gates, prm, tm):
    m = x2d.shape[0]
    tm = min(tm, m)
    row = lambda w: pl.BlockSpec((tm, w), lambda i: (i, 0))
    return pl.pallas_call(
        _final_body,
        grid=(m // tm,),
        in_specs=[row(D_MODEL), row(RWKV_W), row(RWKV_W), row(RWKV_W), row(NSA_W), row(2 * D_MODEL),
                  _const_spec((1, RWKV_W)), _const_spec((1, RWKV_W)), _const_spec((RWKV_W, RWKV_W)),
                  _const_spec((RWKV_W, D_MODEL)), _const_spec((NSA_W, D_MODEL)),
                  _const_spec((D_MODEL, D_MODEL)), _const_spec((1, D_MODEL)),
                  _const_spec((D_MODEL, D_FF)), _const_spec((D_MODEL, D_FF)), _const_spec((D_FF, D_MODEL)),
                  _const_spec((1, D_MODEL))],
        out_specs=row(D_MODEL),
        out_shape=jax.ShapeDtypeStruct((m, D_MODEL), F32),
        compiler_params=_cparams(("parallel",)),
        name="merge_ffn",
    )(x2d, yraw, bonus, g, yn, gates, prm["ln_w"], prm["ln_b"], prm["bd"], prm["w_br_rwkv"],
      prm["w_br_nsa"], prm["w_out"], prm["g_ffn"], prm["w_ffn_gate"], prm["w_ffn_up"],
      prm["w_ffn_down"], prm["g_final"])


_S = (RWKV_W, RWKV_W + DECAY_RANK, 2 * RWKV_W + DECAY_RANK, 3 * RWKV_W + DECAY_RANK,
      3 * RWKV_W + DECAY_RANK + AAA_RANK)
_RW_PERM = np.concatenate([np.arange(0, _S[0]), np.arange(_S[1], _S[2]), np.arange(_S[2], _S[3]),
                           np.arange(_S[0], _S[1]), np.arange(_S[3], _S[4]), np.arange(_S[4], RWKV_COLS)])
_RW_INV = np.argsort(_RW_PERM)


def _prep_params(l, g_mix, w_in, rwkv_mu, rwkv_w0, rwkv_w_decay, rwkv_a0, rwkv_w_aaa, rwkv_w_gate,
                 rwkv_k_k, rwkv_k_a, rwkv_r_k, rwkv_ln_w, rwkv_ln_b, w_br_rwkv, w_br_nsa, w_out, g_ffn,
                 w_ffn_gate, w_ffn_up, w_ffn_down, g_final):
    w = w_in[l]
    nsa0 = RWKV_COLS
    kv_end = nsa0 + NSA_W + 6 * KV_W
    zc = lambda n: jnp.zeros((D_MODEL, n), F32)
    w_r = jnp.concatenate([w[:, _RW_PERM], zc(RW_PAD - RWKV_COLS), w[:, nsa0:kv_end],
                           w[:, kv_end:nsa0 + NSA_COLS], zc(GN_PAD - 3 * NSA_HEADS),
                           w[:, nsa0 + NSA_COLS:]], axis=1).astype(BF16)
    mu = jnp.concatenate([rwkv_mu[l][_RW_PERM], jnp.zeros((RW_PAD - RWKV_COLS,), F32)])[None]
    wda = jnp.zeros((LANES, 2 * RWKV_W), F32)
    wda = wda.at[:DECAY_RANK, :RWKV_W].set(rwkv_w_decay[l]).at[DECAY_RANK:, RWKV_W:].set(rwkv_w_aaa[l])
    wg = jnp.zeros((RW_PAD - 3 * RWKV_W - LANES, RWKV_W), F32).at[:GATE_RANK].set(rwkv_w_gate[l])
    hid = np.arange(RWKV_W) // RWKV_HEAD
    bd = jnp.asarray(hid[:, None] == hid[None, :], BF16)
    r1 = lambda a: a.reshape(1, -1)
    return dict(
        g_mix=r1(g_mix[l]), w_r=w_r, mu=mu, wda=wda.astype(BF16), w0=r1(rwkv_w0[l]), a0=r1(rwkv_a0[l]),
        wg=wg.astype(BF16), k_k=r1(rwkv_k_k[l]), k_a=r1(rwkv_k_a[l]), r_k=r1(rwkv_r_k[l]), bd=bd,
        ln_w=r1(rwkv_ln_w[l]), ln_b=r1(rwkv_ln_b[l]), w_br_rwkv=w_br_rwkv[l].astype(BF16),
        w_br_nsa=w_br_nsa[l].astype(BF16), w_out=w_out[l].astype(BF16), g_ffn=r1(g_ffn[l]),
        w_ffn_gate=w_ffn_gate[l].astype(BF16), w_ffn_up=w_ffn_up[l].astype(BF16),
        w_ffn_down=w_ffn_down[l].astype(BF16), g_final=r1(g_final))


def _rope_tables(pos):
    half = ROT_DIM // 2
    inv = ROPE_THETA ** (-jnp.arange(half, dtype=F32) / half)
    ang = pos.astype(F32)[:, None] * inv[None, :]
    cos, sin = jnp.cos(ang), jnp.sin(ang)
    n = pos.shape[0]
    one, zero = jnp.ones((n, HEAD_DIM - ROT_DIM), F32), jnp.zeros((n, HEAD_DIM - ROT_DIM), F32)
    z8 = jnp.zeros((n, half), F32)
    c = jnp.concatenate([cos, cos, one], 1)
    s1 = jnp.concatenate([-sin, z8, zero], 1)
    s2 = jnp.concatenate([z8, sin, zero], 1)
    return tuple(jnp.concatenate([a, a], 1) for a in (c, s1, s2))


def _state_to_pairs(s):
    b = s.shape[0]
    return s.reshape(b, RWKV_HEADS // 2, 2, RWKV_HEAD, RWKV_HEAD).transpose(0, 1, 4, 2, 3).reshape(
        b, RWKV_HEADS // 2, RWKV_HEAD, LANES)


def _pairs_to_state(st):
    b = st.shape[0]
    return st.reshape(b, RWKV_HEADS // 2, RWKV_HEAD, 2, RWKV_HEAD).transpose(0, 1, 3, 4, 2).reshape(
        b, RWKV_HEADS, RWKV_HEAD, RWKV_HEAD)


def _group(x, pos, shift0, s0, prm, cw, past, tm, pre_blk, tq):
    b, t, _ = x.shape
    m = b * t
    x2d = x.reshape(m, D_MODEL)
    tabs = _rope_tables(jnp.tile(pos, b))
    p_rw, q, q_r, cmp_rows, slc_rows, win_rows, gn, gates = _in_proj(x2d, prm["g_mix"], prm["w_r"], tabs, tm)

    shift_r = jnp.concatenate([shift0[:, _RW_PERM], jnp.zeros((b, RW_PAD - RWKV_COLS), F32)], 1)[:, None, :]
    chunked = t % RWKV_CHUNK == 0
    *scan_in, v, bonus, g = _rwkv_pre(p_rw.reshape(b, t, RW_PAD), shift_r, prm, *pre_blk, not chunked)
    if chunked:
        y_raw, st = _rwkv_chunk_scan(scan_in, v, _state_to_blockdiag(s0), t)
        s_new = _blockdiag_to_state(st)
    else:
        y_raw, st = _rwkv_scan(scan_in, v, _state_to_pairs(s0), t)
        s_new = _pairs_to_state(st)
    shift_new = p_rw.reshape(b, t, RW_PAD)[:, -1, :RWKV_COLS][:, _RW_INV]

    r3 = lambda a: a.reshape(b, t, a.shape[-1])
    if past is None:
        kcv = _compress_rows(r3(cmp_rows), cw)
        n_c = t // D_CMP - 1
        oc, sel = _cmp_attn(r3(q), kcv, tq, 0, n_c)
        y_n = _nsa_prompt(r3(q_r), r3(slc_rows), r3(win_rows), sel, oc, r3(gn), tq, min(512, t))
        win_state = r3(win_rows)[:, t - min(WINDOW, t):]
    else:
        cmp_cache, slc_cache, win_cache, page_table = past
        past_len = page_table.shape[1] * cmp_cache.shape[1]
        kcv = _compress_paged(cmp_cache, page_table, cw)
        n_c = (past_len + t) // D_CMP - 1
        oc, sel = _cmp_attn(r3(q), kcv, tq, past_len, n_c)
        y_n = _nsa_sample(r3(q_r), slc_cache, page_table, r3(slc_rows), win_cache, r3(win_rows), sel, oc, r3(gn))
        win_all = jnp.concatenate([win_cache, r3(win_rows)], axis=1)
        win_state = win_all[:, win_all.shape[1] - min(WINDOW, win_all.shape[1]):]

    y = _final(x2d, y_raw, bonus, g, y_n.reshape(m, NSA_W), gates, prm, tm)
    kv6 = lambda a: a.reshape(b, -1, 2, NSA_KV, HEAD_DIM)
    return (y.reshape(b, t, D_MODEL), kv6(r3(cmp_rows)), kv6(r3(slc_rows)), kv6(win_state),
            s_new, shift_new)


def kernel(x_prompt, x_sample, cache_cmp_kv, cache_slc_kv, cache_win_kv, state_rwkv, state_rwkv_shift, page_table, g_mix, w_in, rwkv_mu, rwkv_w0, rwkv_w_decay, rwkv_a0, rwkv_w_aaa, rwkv_w_gate, rwkv_k_k, rwkv_k_a, rwkv_r_k, rwkv_ln_w, rwkv_ln_b, nsa_pe_cmp, nsa_w_cmp1, nsa_w_cmp2, w_br_rwkv, w_br_nsa, w_out, g_ffn, w_ffn_gate, w_ffn_up, w_ffn_down, g_final):
    depth = w_in.shape[0]
    assert depth == 1, "single-layer trunk"
    b, t, _ = x_prompt.shape
    bs, ts, _ = x_sample.shape
    n_pool, page = cache_cmp_kv.shape[1:3]
    past_len = page_table.shape[1] * page
    assert (past_len + ts) // D_CMP == past_len // D_CMP and past_len % D_CMP == 0
    prm = _prep_params(0, g_mix, w_in, rwkv_mu, rwkv_w0, rwkv_w_decay, rwkv_a0, rwkv_w_aaa, rwkv_w_gate,
                       rwkv_k_k, rwkv_k_a, rwkv_r_k, rwkv_ln_w, rwkv_ln_b, w_br_rwkv, w_br_nsa, w_out,
                       g_ffn, w_ffn_gate, w_ffn_up, w_ffn_down, g_final)
    cw = _compress_weights(nsa_pe_cmp[0], nsa_w_cmp1[0], nsa_w_cmp2[0])
    flat = lambda c: c.reshape(n_pool, page, KV_ROW)

    outs_p = _group(x_prompt, jnp.arange(t), jnp.zeros((b, RWKV_COLS), F32),
                    jnp.zeros((b, RWKV_HEADS, RWKV_HEAD, RWKV_HEAD), F32), prm, cw, None,
                    tm=256, pre_blk=(1, min(512, t)), tq=min(128, t))
    wcache = cache_win_kv[0].reshape(bs, -1, KV_ROW)
    outs_s = _group(x_sample, past_len + jnp.arange(ts), state_rwkv_shift[0], state_rwkv[0], prm, cw,
                    (flat(cache_cmp_kv[0]), flat(cache_slc_kv[0]), wcache, page_table),
                    tm=256, pre_blk=(min(64, bs), ts), tq=ts)
    return (outs_p[0], outs_s[0]) + tuple(o[None] for o in outs_p[1:]) + tuple(o[None] for o in outs_s[1:])
```

```python
import functools

import jax
import jax.numpy as jnp
import numpy as np
from jax import lax
from jax.experimental import pallas as pl
from jax.experimental.pallas import tpu as pltpu

F32 = jnp.float32
BF16 = jnp.bfloat16

D_MODEL = 1024
RWKV_HEADS = 8
RWKV_HEAD = 64
RWKV_W = RWKV_HEADS * RWKV_HEAD
DECAY_RANK = 64
AAA_RANK = 64
GATE_RANK = 160
RWKV_GN_EPS = 64e-5
NSA_HEADS = 8
NSA_KV = 2
HEAD_DIM = 64
HPG = NSA_HEADS // NSA_KV
NSA_W = NSA_HEADS * HEAD_DIM
KV_W = NSA_KV * HEAD_DIM
L_CMP = 32
D_CMP = 16
CMP_HID = 128
L_SEL = 64
N_SEL = 16
WINDOW = 512
ROT_DIM = HEAD_DIM // 4
ROPE_THETA = 500000.0
D_FF = -(-8 * D_MODEL // (3 * 256)) * 256
NORM_EPS = 1e-6
NEG = -1e30
FORCE = 1e6
RWKV_COLS = 3 * RWKV_W + DECAY_RANK + AAA_RANK + GATE_RANK
NSA_COLS = NSA_W + 6 * KV_W + 3 * NSA_HEADS

LANES = 128
RW_PAD = 1920
GN_PAD = LANES
KV_ROW = 2 * KV_W
NBLK_PAD = 64
VMEM_LIMIT = 56 * 1024 * 1024


def _cparams(sem):
    return pltpu.CompilerParams(dimension_semantics=sem, vmem_limit_bytes=VMEM_LIMIT)


def _const_spec(shape):
    nd = len(shape)
    return pl.BlockSpec(shape, lambda *_: (0,) * nd, pipeline_mode=pl.Buffered(1))


def _dot(a, b):
    return jnp.dot(a.astype(BF16), b.astype(BF16), preferred_element_type=F32)


def _dot_nt(a, b):
    return lax.dot_general(a.astype(BF16), b.astype(BF16), (((1,), (1,)), ((), ())),
                           preferred_element_type=F32)


def _dot2(a, b):
    hi = a.astype(BF16)
    lo = (a - hi.astype(F32)).astype(BF16)
    return (jnp.dot(hi, b, preferred_element_type=F32)
            + jnp.dot(lo, b, preferred_element_type=F32))


def _sigmoid(x):
    return 1.0 / (1.0 + jnp.exp(-x))


def _silu(x):
    return x * _sigmoid(x)


def _rope128(x, cos, s1, s2):
    half = ROT_DIM // 2
    return x * cos + pltpu.roll(x, LANES - half, 1) * s1 + pltpu.roll(x, half, 1) * s2


_C_Q = RW_PAD
_C_CMP = _C_Q + NSA_W
_C_SLC = _C_CMP + KV_ROW
_C_WIN = _C_SLC + KV_ROW
_C_GN = _C_WIN + KV_ROW
_C_GATE = _C_GN + GN_PAD
_C_END = _C_GATE + 2 * D_MODEL


def _proj_body(x_ref, g_ref, w_ref, cos_ref, s1_ref, s2_ref,
               prw_ref, q_ref, qr_ref, cmp_ref, slc_ref, win_ref, gn_ref, gates_ref, slcb_ref, winb_ref):
    x = x_ref[...]
    ms = jnp.mean(x * x, axis=-1, keepdims=True)
    h = (x * lax.rsqrt(ms + NORM_EPS) * g_ref[...]).astype(BF16)

    def mm(a, b):
        return jnp.dot(h, w_ref[:, a:b], preferred_element_type=F32)

    cos, s1, s2 = cos_ref[...], s1_ref[...], s2_ref[...]
    prw_ref[...] = mm(0, _C_Q)
    q = mm(_C_Q, _C_CMP)
    q_ref[...] = q
    for c in range(NSA_W // LANES):
        sl = slice(c * LANES, (c + 1) * LANES)
        qr_ref[:, sl] = _rope128(q[:, sl], cos, s1, s2)
    cmp_ref[...] = mm(_C_CMP, _C_SLC)
    for lo, hi, f_ref, h_ref in ((_C_SLC, _C_WIN, slc_ref, slcb_ref), (_C_WIN, _C_GN, win_ref, winb_ref)):
        kv = mm(lo, hi)
        f_ref[:, :KV_W] = _rope128(kv[:, :KV_W], cos, s1, s2)
        f_ref[:, KV_W:] = kv[:, KV_W:]
        h_ref[...] = f_ref[...].astype(BF16)
    gn_ref[...] = mm(_C_GN, _C_GATE)
    gates_ref[...] = mm(_C_GATE, _C_END)


def _in_proj(x2d, g_mix, w_r, pos, tm):
    m = x2d.shape[0]
    tm = min(tm, m)
    period = max(pos.shape[0], tm) // tm
    tabs = _rope_tables(jnp.tile(pos, max(1, tm // pos.shape[0])))
    widths = (RW_PAD, NSA_W, NSA_W, KV_ROW, KV_ROW, KV_ROW, GN_PAD, 2 * D_MODEL)
    row = lambda w: pl.BlockSpec((tm, w), lambda i: (i, 0))
    tab = pl.BlockSpec((tm, LANES), lambda i: (i % period, 0))
    return pl.pallas_call(
        _proj_body,
        grid=(m // tm,),
        in_specs=[row(D_MODEL), _const_spec((1, D_MODEL)), _const_spec((D_MODEL, _C_END)),
                  tab, tab, tab],
        out_specs=[row(w) for w in widths] + [row(KV_ROW)] * 2,
        out_shape=[jax.ShapeDtypeStruct((m, w), F32) for w in widths]
                  + [jax.ShapeDtypeStruct((m, KV_ROW), BF16)] * 2,
        compiler_params=_cparams(("parallel",)),
        name="in_proj",
    )(x2d, g_mix, w_r, *tabs)


def _rwkv_pre_body(p_ref, sh_ref, mu_ref, wda_ref, w0_ref, a0_ref, wg_ref, kk_ref, ka_ref,
                   rk_ref, bd_ref, r_ref, lw_ref, k_ref, a_ref, b_ref, v_ref, bonus_ref,
                   g_ref, carry_ref):
    j = pl.program_id(1)
    bb, tt, c = p_ref.shape
    n = bb * tt
    x3 = p_ref[...]
    x = x3.reshape(n, c)
    prev = jnp.where(j == 0, sh_ref[...], carry_ref[...])
    carry_ref[...] = x3[:, tt - 1:tt, :]
    prev_rows = jnp.broadcast_to(prev, (bb, tt, c)).reshape(n, c)
    row = lax.broadcasted_iota(jnp.int32, (n, c), 0)
    shifted = jnp.where(row % tt == 0, prev_rows, pltpu.roll(x, 1, 0))
    xs = x + (shifted - x) * mu_ref[...]
    r = xs[:, 0:RWKV_W]
    k = xs[:, RWKV_W:2 * RWKV_W]
    v = xs[:, 2 * RWKV_W:3 * RWKV_W]
    wa = xs[:, 3 * RWKV_W:3 * RWKV_W + LANES]
    gl = xs[:, 3 * RWKV_W + LANES:RW_PAD]
    lane = lax.broadcasted_iota(jnp.int32, wa.shape, 1)
    z = _dot(jnp.where(lane < DECAY_RANK, jnp.tanh(wa), wa), wda_ref[...])
    u = -(w0_ref[...] + z[:, :RWKV_W])
    softplus = jnp.maximum(u, 0.0) + jnp.log1p(jnp.exp(-jnp.abs(u)))
    log_decay = -jnp.exp(-softplus - 0.5)
    a = _sigmoid(a0_ref[...] + z[:, RWKV_W:])
    g_ref[...] = _dot(_sigmoid(gl), wg_ref[...])
    bd = bd_ref[...]
    kk = k * kk_ref[...]
    kk = kk * lax.rsqrt(jnp.maximum(_dot2(kk * kk, bd), 1e-24))
    k_h = k * (1.0 + (a - 1.0) * ka_ref[...])
    bonus_ref[...] = _dot2(r * k_h * rk_ref[...], bd) * v
    v_ref[...] = v
    r_ref[...] = r
    lw_ref[...] = log_decay
    k_ref[...] = k_h
    a_ref[...] = -kk
    b_ref[...] = kk * a


def _rwkv_pre(p3, shift0, prm, bb, tt):
    b, t, _ = p3.shape
    m = b * t
    n = bb * tt
    nt = t // tt
    rowm = pl.BlockSpec((n, RWKV_W), lambda i, j: (i * nt + j, 0))
    outs = pl.pallas_call(
        _rwkv_pre_body,
        grid=(b // bb, nt),
        in_specs=[pl.BlockSpec((bb, tt, RW_PAD), lambda i, j: (i, j, 0)),
                  pl.BlockSpec((bb, 1, RW_PAD), lambda i, j: (i, 0, 0)),
                  _const_spec((1, RW_PAD)), _const_spec((LANES, 2 * RWKV_W)),
                  _const_spec((1, RWKV_W)), _const_spec((1, RWKV_W)),
                  _const_spec((RW_PAD - 3 * RWKV_W - LANES, RWKV_W)),
                  _const_spec((1, RWKV_W)), _const_spec((1, RWKV_W)), _const_spec((1, RWKV_W)),
                  _const_spec((RWKV_W, RWKV_W))],
        out_specs=[rowm] * 8,
        out_shape=[jax.ShapeDtypeStruct((m, RWKV_W), F32)] * 8,
        scratch_shapes=[pltpu.VMEM((bb, 1, RW_PAD), F32)],
        compiler_params=_cparams(("parallel", "arbitrary")),
        name="rwkv_pre",
    )(p3, shift0, prm["mu"], prm["wda"], prm["w0"], prm["a0"], prm["wg"], prm["k_k"],
      prm["k_a"], prm["r_k"], prm["bd"])
    return outs


RWKV_CHUNK = 64


def _split3(x):
    x1 = x.astype(BF16)
    r1 = x - x1.astype(F32)
    x2 = r1.astype(BF16)
    return x1, x2, (r1 - x2.astype(F32)).astype(BF16)


def _chunk_scan_body(r_ref, lw_ref, k_ref, a_ref, b_ref, v_ref, s0_ref, tri_ref, y_ref, s_ref):
    c = r_ref.shape[0]
    n = 2 * c
    first = pl.program_id(1) == 0
    left = lax.broadcasted_iota(jnp.int32, (c, LANES), 1) < RWKV_HEAD
    stack = lambda x: jnp.concatenate([jnp.where(left, x, 0.0), jnp.where(left, 0.0, x)], axis=0)
    ri = lax.broadcasted_iota(jnp.int32, (n, n), 0)
    ci = lax.broadcasted_iota(jnp.int32, (n, n), 1)
    strict = ((ri < c) == (ci < c)) & (ci < ri)
    eye = jnp.where(ri == ci, 1.0, 0.0)
    ri2 = lax.broadcasted_iota(jnp.int32, (n, 2 * n), 0)
    ci2 = lax.broadcasted_iota(jnp.int32, (n, 2 * n), 1) % n
    incl2 = ((ri2 < c) == (ci2 < c)) & (ci2 <= ri2)
    tri = tri_ref[...]
    pairs = range(RWKV_HEADS // 2)
    pre = []
    for p in pairs:
        sl = slice(p * LANES, (p + 1) * LANES)
        lw = lw_ref[:, sl]
        cs = sum(jnp.dot(tri, part, preferred_element_type=F32) for part in _split3(lw))
        c_end = cs[c - 1:c, :]
        e_neg = jnp.exp(-cs)
        e_hat = jnp.exp(c_end - cs)
        at = stack(a_ref[:, sl] * jnp.exp(cs - lw))
        rt = stack(r_ref[:, sl] * jnp.exp(cs))
        bt = stack(b_ref[:, sl] * e_neg)
        kt = stack(k_ref[:, sl] * e_neg)
        bk_hat = jnp.concatenate([stack(b_ref[:, sl] * e_hat), stack(k_ref[:, sl] * e_hat)], axis=0)
        g = _dot_nt(jnp.concatenate([at, rt], axis=0), jnp.concatenate([bt, kt], axis=0))
        pre.append(dict(sl=sl, e_end=jnp.exp(c_end), at=at, rt=rt, bk_hat=bk_hat.astype(BF16),
                        vs=stack(v_ref[:, sl]), low=jnp.where(strict, g[:n, :n], 0.0),
                        g_ak=jnp.where(strict, g[:n, n:], 0.0),
                        g_r=jnp.where(incl2, g[n:, :], 0.0)))
    tinv = [eye + q["low"] for q in pre]
    lp = [q["low"] for q in pre]
    for _ in range(c.bit_length() - 2):
        lp = [_dot(x, x) for x in lp]
        tinv = [t + _dot(t, x) for t, x in zip(tinv, lp)]
    state = [jnp.where(first, s0_ref[p], s_ref[p]) for p in pairs]
    rhs = [_dot_nt(q["at"], s) + _dot(q["g_ak"], q["vs"]) for q, s in zip(pre, state)]
    uv = [jnp.concatenate([_dot(t, x), q["vs"]], axis=0) for t, x, q in zip(tinv, rhs, pre)]
    for p, q, s, w in zip(pairs, pre, state, uv):
        ys = _dot_nt(q["rt"], s) + _dot(q["g_r"], w)
        y_ref[:, q["sl"]] = ys[:c] + ys[c:]
        s_ref[p] = s * q["e_end"] + lax.dot_general(
            w.astype(BF16), q["bk_hat"], (((0,), (0,)), ((), ())), preferred_element_type=F32)


def _rwkv_chunk_scan(rows, v, s0pair, t):
    m = v.shape[0]
    b = s0pair.shape[0]
    c = min(RWKV_CHUNK, t)
    nc = t // c
    rowm = pl.BlockSpec((c, RWKV_W), lambda i, j: (i * nc + j, 0))
    sspec = pl.BlockSpec((None, RWKV_HEADS // 2, LANES, LANES), lambda i, j: (i, 0, 0, 0))
    tri = jnp.asarray(np.tril(np.ones((c, c), np.float32)), BF16)
    return pl.pallas_call(
        _chunk_scan_body,
        grid=(b, nc),
        in_specs=[rowm] * 6 + [sspec, pl.BlockSpec((c, c), lambda i, j: (0, 0))],
        out_specs=[rowm, sspec],
        out_shape=[jax.ShapeDtypeStruct((m, RWKV_W), F32), jax.ShapeDtypeStruct(s0pair.shape, F32)],
        compiler_params=_cparams(("parallel", "arbitrary")),
        name="rwkv_chunk_scan",
    )(*rows, v, s0pair, tri)


def _state_to_blockdiag(s):
    b = s.shape[0]
    sp = s.reshape(b, RWKV_HEADS // 2, 2, RWKV_HEAD, RWKV_HEAD)
    z = jnp.zeros_like(sp[:, :, 0])
    top = jnp.concatenate([sp[:, :, 0], z], axis=-1)
    bot = jnp.concatenate([z, sp[:, :, 1]], axis=-1)
    return jnp.concatenate([top, bot], axis=-2)


def _blockdiag_to_state(sp):
    b = sp.shape[0]
    h0 = sp[:, :, :RWKV_HEAD, :RWKV_HEAD]
    h1 = sp[:, :, RWKV_HEAD:, RWKV_HEAD:]
    return jnp.stack([h0, h1], axis=2).reshape(b, RWKV_HEADS, RWKV_HEAD, RWKV_HEAD)


def _compress_body(*refs, n_src):
    pe_ref, w1_ref, w2_ref, o_ref = refs[-4:]
    x_refs = refs[-4 - 2 * n_src:-4]
    per = x_refs[0].shape[0] // D_CMP
    for c in range(2):
        srcs = x_refs[c * n_src:(c + 1) * n_src]
        for l in range(D_CMP):
            xl = jnp.concatenate([xr[pl.ds(l, per, stride=D_CMP), :] for xr in srcs], axis=0)
            f = _dot(xl + pe_ref[c, l:l + 1, :], w1_ref[c, l])
            s = _dot(xl + pe_ref[c, D_CMP + l:D_CMP + l + 1, :], w1_ref[c, D_CMP + l])
            hf = f if l == 0 else hf + f
            hs = s if l == 0 else hs + s
        nc = hf.shape[0]
        h = _silu(hf + pltpu.roll(hs, nc - 1, 0))
        o_ref[:, c * KV_W:(c + 1) * KV_W] = _dot(h, w2_ref[c])


def _compress_weights(pe, w1, w2):
    eye = jnp.eye(NSA_KV, dtype=F32)
    w1b = jnp.einsum("cldh,gy->clgdyh", w1, eye).reshape(2, L_CMP, KV_W, NSA_KV * CMP_HID)
    w2b = jnp.einsum("chd,gy->cghyd", w2, eye).reshape(2, NSA_KV * CMP_HID, KV_W)
    peb = jnp.broadcast_to(pe[:, :, None, :], (2, L_CMP, NSA_KV, HEAD_DIM)).reshape(2, L_CMP, KV_W)
    return (peb, w1b.astype(BF16), w2b.astype(BF16))


def _compress_call(n_src, nb, nc, in_specs_x, cw, n_prefetch, args):
    hid = NSA_KV * CMP_HID
    wspecs = [pl.BlockSpec((2, L_CMP, KV_W), lambda *_: (0, 0, 0)),
              pl.BlockSpec((2, L_CMP, KV_W, hid), lambda *_: (0, 0, 0, 0)),
              pl.BlockSpec((2, hid, KV_W), lambda *_: (0, 0, 0))]
    return pl.pallas_call(
        functools.partial(_compress_body, n_src=n_src),
        grid_spec=pltpu.PrefetchScalarGridSpec(
            num_scalar_prefetch=n_prefetch, grid=(nb,), in_specs=in_specs_x + wspecs,
            out_specs=pl.BlockSpec((None, nc, KV_ROW), lambda b, *_: (b, 0, 0))),
        out_shape=jax.ShapeDtypeStruct((nb, nc, KV_ROW), F32),
        compiler_params=_cparams(("parallel",)),
        name="nsa_compress",
    )(*args, *cw)


def _compress_rows(rows3, cw):
    b, t, _ = rows3.shape
    xs = [pl.BlockSpec((None, t, KV_W), functools.partial(lambda c, i: (i, 0, c), c)) for c in range(2)]
    return _compress_call(1, b, t // D_CMP, xs, cw, 0, (rows3, rows3))


def _compress_paged(cache, page_table, cw):
    b, n_pages = page_table.shape
    page = cache.shape[1]
    xs = [pl.BlockSpec((None, page, KV_W), functools.partial(lambda c, k, i, pt: (pt[i, k], 0, c), c, k))
          for c in range(2) for k in range(n_pages)]
    return _compress_call(n_pages, b, n_pages * page // D_CMP, xs, cw, 1,
                          (page_table,) + (cache,) * (2 * n_pages))


def _head_to_half(q_ref, h, g):
    c = q_ref[:, (h // 2) * LANES:(h // 2 + 1) * LANES]
    if h % 2 != g:
        c = pltpu.roll(c, HEAD_DIM, 1)
    lane = lax.broadcasted_iota(jnp.int32, c.shape, 1)
    return jnp.where((lane >= g * HEAD_DIM) & (lane < (g + 1) * HEAD_DIM), c, 0.0)


def _halves_to_heads(o_ref, outs, g):
    for pr in range(HPG // 2):
        tiles = []
        for hl in (2 * pr, 2 * pr + 1):
            h = g * HPG + hl
            o = outs[hl]
            if h % 2 != g:
                o = pltpu.roll(o, HEAD_DIM, 1)
            tiles.append(o)
        lane = lax.broadcasted_iota(jnp.int32, tiles[0].shape, 1)
        c = (g * HPG) // 2 + pr
        o_ref[:, c * LANES:(c + 1) * LANES] = jnp.where(lane < HEAD_DIM, tiles[0], tiles[1])


def _cmp_attn_body(q_ref, kcv_ref, ov_ref, oc_ref, sel_ref, *, tq, pos0, n_c):
    qi = pl.program_id(1)
    nb, ncp = kcv_ref.shape[0], kcv_ref.shape[1]
    n = nb * tq
    row = lax.broadcasted_iota(jnp.int32, (tq, ncp), 0)
    col = lax.broadcasted_iota(jnp.int32, (tq, ncp), 1)
    mask = (col * D_CMP + (L_CMP - 1) <= pos0 + qi * tq + row) & (col < n_c)
    imps = []
    for sb in range(nb):
        kc = kcv_ref[sb, :, :KV_W]
        vc = kcv_ref[sb, :, KV_W:]
        imp = None
        for g in range(NSA_KV):
            outs = []
            psum = None
            for hl in range(HPG):
                qz = _head_to_half(q_ref.at[sb], g * HPG + hl, g)
                s = jnp.where(mask, _dot_nt(qz, kc) * HEAD_DIM ** -0.5, NEG)
                e = jnp.exp(s - jnp.max(s, axis=-1, keepdims=True))
                p = jnp.where(mask, e / jnp.sum(e, axis=-1, keepdims=True), 0.0)
                outs.append(_dot(p, vc))
                psum = p if psum is None else psum + p
            _halves_to_heads(oc_ref.at[sb], outs, g)
            ig = _dot2(psum, ov_ref[g])
            imp = ig if imp is None else imp + ig
        imps.append(imp)
    imp_t = jnp.concatenate(imps, axis=0).T
    jb = lax.broadcasted_iota(jnp.int32, (LANES, n), 0) % NBLK_PAD
    cur = (pos0 + qi * tq + lax.broadcasted_iota(jnp.int32, (LANES, n), 1) % tq) // L_SEL
    forced = (jb == 0) | (jb == cur) | (jb == cur - 1)
    score = jnp.where(forced, FORCE, jnp.where(jb <= cur, imp_t, -1.0))
    sub = lax.broadcasted_iota(jnp.int32, (8, n), 0)
    ranks = []
    for g in range(NSA_KV):
        blocks = [score[g * NBLK_PAD + 8 * k:g * NBLK_PAD + 8 * k + 8, :] for k in range(NBLK_PAD // 8)]
        cnt = [jnp.zeros((8, n), F32) for _ in blocks]
        for jp in range(NBLK_PAD):
            other = jnp.broadcast_to(score[g * NBLK_PAD + jp:g * NBLK_PAD + jp + 1, :], (8, n))
            for k, blk in enumerate(blocks):
                if 8 * k > jp:
                    ahead = other >= blk
                elif 8 * k + 7 <= jp:
                    ahead = other > blk
                else:
                    ahead = (other > blk) | ((other == blk) & (sub > jp - 8 * k))
                cnt[k] = cnt[k] + jnp.where(ahead, 1.0, 0.0)
        ranks += cnt
    rank = jnp.concatenate(ranks, axis=0)
    sel = jnp.where((rank < N_SEL) & (jb <= cur), 1.0, 0.0).T
    for sb in range(nb):
        sel_ref[sb] = sel[sb * tq:(sb + 1) * tq]


def _overlap(ncp, n_c):
    ci = np.arange(ncp)[:, None]
    sj = np.arange(NBLK_PAD)[None, :]
    ov = ((ci * D_CMP < (sj + 1) * L_SEL) & (ci * D_CMP + L_CMP > sj * L_SEL) & (ci < n_c)).astype(np.float32)
    z = np.zeros_like(ov)
    return jnp.asarray(np.stack([np.concatenate([ov, z], 1), np.concatenate([z, ov], 1)]), BF16)


def _cmp_attn(q3, kcv, tq, pos0, n_c):
    b, t, _ = q3.shape
    ncp = kcv.shape[1]
    nb = LANES // tq
    blk = lambda r, w: pl.BlockSpec((nb, r, w), lambda i, j: (i, j, 0))
    return pl.pallas_call(
        functools.partial(_cmp_attn_body, tq=tq, pos0=pos0, n_c=n_c),
        grid=(b // nb, t // tq),
        in_specs=[blk(tq, NSA_W), pl.BlockSpec((nb, ncp, KV_ROW), lambda i, j: (i, 0, 0)),
                  pl.BlockSpec((NSA_KV, ncp, LANES), lambda i, j: (0, 0, 0))],
        out_specs=[blk(tq, NSA_W), blk(tq, LANES)],
        out_shape=[jax.ShapeDtypeStruct((b, t, NSA_W), F32), jax.ShapeDtypeStruct((b, t, LANES), F32)],
        compiler_params=_cparams(("parallel", "parallel")),
        name="nsa_cmp_attn",
    )(q3, kcv, _overlap(ncp, n_c))


def _gate_expand():
    e = np.zeros((GN_PAD, 3 * NSA_W), np.float32)
    for h in range(NSA_HEADS):
        for k in range(3):
            e[h * 3 + k, k * NSA_W + h * HEAD_DIM:k * NSA_W + (h + 1) * HEAD_DIM] = 1.0
    return jnp.asarray(e, BF16)


def _combine(y_ref, gn_ref, ge_ref, oc, osel, owin):
    gates = _dot2(_sigmoid(gn_ref[...]), ge_ref[...])
    y_ref[...] = (gates[:, :NSA_W] * oc + gates[:, NSA_W:2 * NSA_W] * osel + gates[:, 2 * NSA_W:] * owin)


def _per_head(allow):
    f = jnp.where(allow, 1.0, 0.0)
    return jnp.concatenate([f] * HPG, axis=0) > 0.5


def _sel_mask_tile(sel, g, kv0, tk):
    r = lax.broadcasted_iota(jnp.int32, (LANES, tk), 0)
    blk = (kv0 + lax.broadcasted_iota(jnp.int32, (LANES, tk), 1)) // L_SEL
    expand = (r == g * NBLK_PAD + blk).astype(BF16)
    return jnp.dot(sel.astype(BF16), expand, preferred_element_type=F32)


LOG2E = 1.4426950408889634


def _nsa_prompt_body(qr_ref, slc_ref, win_ref, oh_ref, sel_ref, oc_ref, gn_ref, ge_ref, y_ref,
                     os_ref, ow_ref, *, tq, tk, lw):
    qi = pl.program_id(1)
    q0 = qi * tq
    t_all = slc_ref.shape[0]
    nrow = HPG * tq
    heads = lambda a: jnp.concatenate([a] * HPG, axis=0)
    sel = sel_ref[...]
    lane = lax.broadcasted_iota(jnp.int32, (tq, LANES), 1)
    qpos = q0 + lax.broadcasted_iota(jnp.int32, (tq, 1), 0)
    w0 = pl.multiple_of(jnp.clip(q0 + tq - lw, 0, t_all - lw), 16)
    kpos_w = w0 + lax.broadcasted_iota(jnp.int32, (tq, lw), 1)
    wbias = heads(jnp.where((kpos_w <= qpos) & (kpos_w > qpos - WINDOW), 0.0, NEG))
    n_full = q0 // tk
    kv_diag = pl.multiple_of(n_full * tk, tk)
    kpos_d = kv_diag + lax.broadcasted_iota(jnp.int32, (tq, tk), 1)
    cbias = heads(jnp.where(kpos_d <= qpos, 0.0, NEG))
    qst, qaug = [], []
    for g in range(NSA_KV):
        q4 = jnp.concatenate([_head_to_half(qr_ref, g * HPG + hl, g) for hl in range(HPG)], axis=0)
        q4 = q4 * (HEAD_DIM ** -0.5 * LOG2E)
        selg = sel if g == 0 else pltpu.roll(sel, NBLK_PAD, 1)
        sbias = jnp.where(lane < NBLK_PAD, jnp.where(selg > 0.5, 0.0, NEG), 0.0)
        qst.append(q4.astype(BF16))
        qaug.append(jnp.concatenate([q4, heads(sbias)], axis=1).astype(BF16))

    def fold(carry, s, v):
        m, l, acc = carry
        m_new = jnp.maximum(m, jnp.max(s, axis=-1, keepdims=True))
        p = jnp.exp2(s - m_new)
        alpha = jnp.exp2(m - m_new)
        return (m_new, alpha * l + jnp.sum(p, axis=-1, keepdims=True),
                alpha * acc + jnp.dot(p.astype(BF16), v, preferred_element_type=F32))

    def sel_step(kv0, carries, bias):
        kaug = jnp.concatenate([slc_ref[pl.ds(kv0, tk), :KV_W], oh_ref[pl.ds(kv0, tk), :]], axis=1)
        v = slc_ref[pl.ds(kv0, tk), KV_W:]
        out = []
        for g in range(NSA_KV):
            s = lax.dot_general(qaug[g], kaug, (((1,), (1,)), ((), ())), preferred_element_type=F32)
            out.append(fold(carries[g], s if bias is None else s + bias, v))
        return tuple(out)

    init = (jnp.full((nrow, 1), NEG, F32), jnp.zeros((nrow, 1), F32), jnp.zeros((nrow, LANES), F32))
    carries = lax.fori_loop(0, n_full, lambda it, c: sel_step(pl.multiple_of(it * tk, tk), c, None),
                            (init,) * NSA_KV)
    carries = sel_step(kv_diag, carries, cbias)
    kw = win_ref[pl.ds(w0, lw), :KV_W]
    vw = win_ref[pl.ds(w0, lw), KV_W:]
    for g in range(NSA_KV):
        m, l, acc = carries[g]
        o = acc / l
        _halves_to_heads(os_ref, [o[hl * tq:(hl + 1) * tq] for hl in range(HPG)], g)
        s = lax.dot_general(qst[g], kw, (((1,), (1,)), ((), ())), preferred_element_type=F32) + wbias
        e = jnp.exp2(s - jnp.max(s, axis=-1, keepdims=True))
        o = jnp.dot(e.astype(BF16), vw, preferred_element_type=F32) / jnp.sum(e, axis=-1, keepdims=True)
        _halves_to_heads(ow_ref, [o[hl * tq:(hl + 1) * tq] for hl in range(HPG)], g)
    _combine(y_ref, gn_ref, ge_ref, oc_ref[...], os_ref[...], ow_ref[...])


def _nsa_prompt(qr3, slc3, win3, sel3, oc3, gn3, tq, tk):
    b, t, _ = qr3.shape
    lw = min(WINDOW + tq, t)
    qblk = lambda w: pl.BlockSpec((None, tq, w), lambda i, j: (i, j, 0))
    full = pl.BlockSpec((None, t, KV_ROW), lambda i, j: (i, 0, 0))
    onehot = jnp.asarray(np.arange(t)[:, None] // L_SEL == np.arange(LANES)[None, :], BF16)
    return pl.pallas_call(
        functools.partial(_nsa_prompt_body, tq=tq, tk=tk, lw=lw),
        grid=(b, t // tq),
        in_specs=[qblk(NSA_W), full, full, pl.BlockSpec((t, LANES), lambda i, j: (0, 0)),
                  qblk(LANES), qblk(NSA_W), qblk(GN_PAD),
                  pl.BlockSpec((GN_PAD, 3 * NSA_W), lambda i, j: (0, 0))],
        out_specs=qblk(NSA_W),
        out_shape=jax.ShapeDtypeStruct((b, t, NSA_W), F32),
        scratch_shapes=[pltpu.VMEM((tq, NSA_W), F32), pltpu.VMEM((tq, NSA_W), F32)],
        compiler_params=_cparams(("parallel", "arbitrary")),
        name="nsa_attn_prompt",
    )(qr3, slc3, win3, onehot, sel3, oc3, gn3, _gate_expand())


def _nsa_sample_body(*refs, n_pages, past_len):
    page_refs = refs[1:1 + n_pages]
    (qr_ref, snew_ref, wold_ref, wnew_ref, sel_ref, oc_ref, gn_ref, ge_ref, y_ref, os_ref, ow_ref) = refs[1 + n_pages:]
    tq = qr_ref.shape[0]
    page = page_refs[0].shape[0]
    wb = wold_ref.shape[0]
    sel = sel_ref[...]
    qpos = past_len + lax.broadcasted_iota(jnp.int32, (tq, 1), 0)
    heads = _per_head

    def attend(parts):
        scores = [jnp.where(allow, _dot_nt(qst, ref[:, :KV_W]), NEG) for ref, allow in parts]
        m = functools.reduce(jnp.maximum, [jnp.max(s, axis=-1, keepdims=True) for s in scores])
        l, acc = 0.0, 0.0
        for s, (ref, allow) in zip(scores, parts):
            e = jnp.where(allow, jnp.exp(s - m), 0.0)
            l = l + jnp.sum(e, axis=-1, keepdims=True)
            acc = acc + _dot(e, ref[:, KV_W:])
        o = acc / l
        return [o[hl * tq:(hl + 1) * tq] for hl in range(HPG)]

    kpos_new = past_len + lax.broadcasted_iota(jnp.int32, (tq, page), 1)
    causal_new = kpos_new <= qpos
    for g in range(NSA_KV):
        qst = jnp.concatenate([_head_to_half(qr_ref, g * HPG + hl, g) for hl in range(HPG)], axis=0)
        qst = (qst * HEAD_DIM ** -0.5).astype(BF16)
        parts = [(page_refs[k], heads(_sel_mask_tile(sel, g, k * page, page) > 0.5)) for k in range(n_pages)]
        parts.append((snew_ref, heads((_sel_mask_tile(sel, g, past_len, page) > 0.5) & causal_new)))
        _halves_to_heads(os_ref, attend(parts), g)
        kpos_old = past_len - wb + lax.broadcasted_iota(jnp.int32, (tq, wb), 1)
        parts = [(wold_ref, heads((kpos_old > qpos - WINDOW) & (kpos_old >= 0))),
                 (wnew_ref, heads(causal_new & (kpos_new > qpos - WINDOW)))]
        _halves_to_heads(ow_ref, attend(parts), g)
    _combine(y_ref, gn_ref, ge_ref, oc_ref[...], os_ref[...], ow_ref[...])


def _nsa_sample(qr3, slc_cache, page_table, snew3, wold3, wnew3, sel3, oc3, gn3):
    b, tq, _ = qr3.shape
    n_pages = page_table.shape[1]
    page = slc_cache.shape[1]
    wb = wold3.shape[1]
    per = lambda r, w: pl.BlockSpec((None, r, w), lambda i, pt: (i, 0, 0))
    pad = lambda a: jnp.pad(a, ((0, 0), (0, page - tq), (0, 0)))
    snew3, wnew3 = pad(snew3), pad(wnew3)
    pages = [pl.BlockSpec((None, page, KV_ROW), functools.partial(lambda k, i, pt: (pt[i, k], 0, 0), k))
             for k in range(n_pages)]
    return pl.pallas_call(
        functools.partial(_nsa_sample_body, n_pages=n_pages, past_len=n_pages * page),
        grid_spec=pltpu.PrefetchScalarGridSpec(
            num_scalar_prefetch=1, grid=(b,),
            in_specs=pages + [per(tq, NSA_W), per(page, KV_ROW), per(wb, KV_ROW), per(page, KV_ROW),
                              per(tq, LANES), per(tq, NSA_W), per(tq, GN_PAD),
                              pl.BlockSpec((GN_PAD, 3 * NSA_W), lambda i, pt: (0, 0))],
            out_specs=per(tq, NSA_W),
            scratch_shapes=[pltpu.VMEM((tq, NSA_W), F32), pltpu.VMEM((tq, NSA_W), F32)]),
        out_shape=jax.ShapeDtypeStruct((b, tq, NSA_W), F32),
        compiler_params=_cparams(("arbitrary",)),
        name="nsa_attn_sample",
    )(page_table, *([slc_cache] * n_pages), qr3, snew3, wold3, wnew3, sel3, oc3, gn3, _gate_expand())


def _final_body(x_ref, yraw_ref, bonus_ref, g_ref, yn_ref, gates_ref, lnw_ref, lnb_ref, bd_ref,
                wbr_ref, wbn_ref, wo_ref, gffn_ref, wg_ref, wu_ref, wd_ref, gfin_ref, y_ref):
    bd = bd_ref[...]
    y = yraw_ref[...]
    d = y - _dot2(y, bd) * (1.0 / RWKV_HEAD)
    var = _dot2(d * d, bd) * (1.0 / RWKV_HEAD)
    y_r = (d * lax.rsqrt(var + RWKV_GN_EPS) * lnw_ref[...] + lnb_ref[...] + bonus_ref[...]) * g_ref[...]
    merged = (_sigmoid(gates_ref[:, :D_MODEL]) * _dot(y_r, wbr_ref[...])
              + _sigmoid(gates_ref[:, D_MODEL:]) * _dot(yn_ref[...], wbn_ref[...]))
    x1 = x_ref[...] + _dot(merged, wo_ref[...])
    h2 = x1 * lax.rsqrt(jnp.mean(x1 * x1, axis=-1, keepdims=True) + NORM_EPS) * gffn_ref[...]
    hb = h2.astype(BF16)
    up = _silu(jnp.dot(hb, wg_ref[...], preferred_element_type=F32)) * jnp.dot(hb, wu_ref[...], preferred_element_type=F32)
    x2 = x1 + _dot(up, wd_ref[...])
    y_ref[...] = x2 * lax.rsqrt(jnp.mean(x2 * x2, axis=-1, keepdims=True) + NORM_EPS) * gfin_ref[...]


def _final(x2d, yraw, bonus, g, yn, gates, prm, tm):
    m = x2d.shape[0]
    tm = min(tm, m)
    row = lambda w: pl.BlockSpec((tm, w), lambda i: (i, 0))
    return pl.pallas_call(
        _final_body,
        grid=(m // tm,),
        in_specs=[row(D_MODEL), row(RWKV_W), row(RWKV_W), row(RWKV_W), row(NSA_W), row(2 * D_MODEL),
                  _const_spec((1, RWKV_W)), _const_spec((1, RWKV_W)), _const_spec((RWKV_W, RWKV_W)),
                  _const_spec((RWKV_W, D_MODEL)), _const_spec((NSA_W, D_MODEL)),
                  _const_spec((D_MODEL, D_MODEL)), _const_spec((1, D_MODEL)),
                  _const_spec((D_MODEL, D_FF)), _const_spec((D_MODEL, D_FF)), _const_spec((D_FF, D_MODEL)),
                  _const_spec((1, D_MODEL))],
        out_specs=row(D_MODEL),
        out_shape=jax.ShapeDtypeStruct((m, D_MODEL), F32),
        compiler_params=_cparams(("parallel",)),
        name="merge_ffn",
    )(x2d, yraw, bonus, g, yn, gates, prm["ln_w"], prm["ln_b"], prm["bd"], prm["w_br_rwkv"],
      prm["w_br_nsa"], prm["w_out"], prm["g_ffn"], prm["w_ffn_gate"], prm["w_ffn_up"],
      prm["w_ffn_down"], prm["g_final"])


_S = (RWKV_W, RWKV_W + DECAY_RANK, 2 * RWKV_W + DECAY_RANK, 3 * RWKV_W + DECAY_RANK,
      3 * RWKV_W + DECAY_RANK + AAA_RANK)
_RW_PERM = np.concatenate([np.arange(0, _S[0]), np.arange(_S[1], _S[2]), np.arange(_S[2], _S[3]),
                           np.arange(_S[0], _S[1]), np.arange(_S[3], _S[4]), np.arange(_S[4], RWKV_COLS)])
_RW_INV = np.argsort(_RW_PERM)


def _prep_params(l, g_mix, w_in, rwkv_mu, rwkv_w0, rwkv_w_decay, rwkv_a0, rwkv_w_aaa, rwkv_w_gate,
                 rwkv_k_k, rwkv_k_a, rwkv_r_k, rwkv_ln_w, rwkv_ln_b, w_br_rwkv, w_br_nsa, w_out, g_ffn,
                 w_ffn_gate, w_ffn_up, w_ffn_down, g_final):
    w = w_in[l]
    nsa0 = RWKV_COLS
    kv_end = nsa0 + NSA_W + 6 * KV_W
    zc = lambda n: jnp.zeros((D_MODEL, n), F32)
    w_r = jnp.concatenate([w[:, _RW_PERM], zc(RW_PAD - RWKV_COLS), w[:, nsa0:kv_end],
                           w[:, kv_end:nsa0 + NSA_COLS], zc(GN_PAD - 3 * NSA_HEADS),
                           w[:, nsa0 + NSA_COLS:]], axis=1).astype(BF16)
    mu = jnp.concatenate([rwkv_mu[l][_RW_PERM], jnp.zeros((RW_PAD - RWKV_COLS,), F32)])[None]
    wda = jnp.zeros((LANES, 2 * RWKV_W), F32)
    wda = wda.at[:DECAY_RANK, :RWKV_W].set(rwkv_w_decay[l]).at[DECAY_RANK:, RWKV_W:].set(rwkv_w_aaa[l])
    wg = jnp.zeros((RW_PAD - 3 * RWKV_W - LANES, RWKV_W), F32).at[:GATE_RANK].set(rwkv_w_gate[l])
    hid = np.arange(RWKV_W) // RWKV_HEAD
    bd = jnp.asarray(hid[:, None] == hid[None, :], BF16)
    r1 = lambda a: a.reshape(1, -1)
    return dict(
        g_mix=r1(g_mix[l]), w_r=w_r, mu=mu, wda=wda.astype(BF16), w0=r1(rwkv_w0[l]), a0=r1(rwkv_a0[l]),
        wg=wg.astype(BF16), k_k=r1(rwkv_k_k[l]), k_a=r1(rwkv_k_a[l]), r_k=r1(rwkv_r_k[l]), bd=bd,
        ln_w=r1(rwkv_ln_w[l]), ln_b=r1(rwkv_ln_b[l]), w_br_rwkv=w_br_rwkv[l].astype(BF16),
        w_br_nsa=w_br_nsa[l].astype(BF16), w_out=w_out[l].astype(BF16), g_ffn=r1(g_ffn[l]),
        w_ffn_gate=w_ffn_gate[l].astype(BF16), w_ffn_up=w_ffn_up[l].astype(BF16),
        w_ffn_down=w_ffn_down[l].astype(BF16), g_final=r1(g_final))


def _rope_tables(pos):
    half = ROT_DIM // 2
    inv = ROPE_THETA ** (-jnp.arange(half, dtype=F32) / half)
    ang = pos.astype(F32)[:, None] * inv[None, :]
    cos, sin = jnp.cos(ang), jnp.sin(ang)
    n = pos.shape[0]
    one, zero = jnp.ones((n, HEAD_DIM - ROT_DIM), F32), jnp.zeros((n, HEAD_DIM - ROT_DIM), F32)
    z8 = jnp.zeros((n, half), F32)
    c = jnp.concatenate([cos, cos, one], 1)
    s1 = jnp.concatenate([-sin, z8, zero], 1)
    s2 = jnp.concatenate([z8, sin, zero], 1)
    return tuple(jnp.concatenate([a, a], 1) for a in (c, s1, s2))


def _group(x, pos, shift0, s0, prm, cw, past, tm, pre_blk, tq):
    b, t, _ = x.shape
    m = b * t
    x2d = x.reshape(m, D_MODEL)
    p_rw, q, q_r, cmp_rows, slc_rows, win_rows, gn, gates, slc_bf, win_bf = _in_proj(
        x2d, prm["g_mix"], prm["w_r"], pos, tm)

    shift_r = jnp.concatenate([shift0[:, _RW_PERM], jnp.zeros((b, RW_PAD - RWKV_COLS), F32)], 1)[:, None, :]
    *scan_in, v, bonus, g = _rwkv_pre(p_rw.reshape(b, t, RW_PAD), shift_r, prm, *pre_blk)
    y_raw, st = _rwkv_chunk_scan(scan_in, v, _state_to_blockdiag(s0), t)
    s_new = _blockdiag_to_state(st)
    shift_new = p_rw.reshape(b, t, RW_PAD)[:, -1, :RWKV_COLS][:, _RW_INV]

    r3 = lambda a: a.reshape(b, t, a.shape[-1])
    if past is None:
        kcv = _compress_rows(r3(cmp_rows), cw)
        n_c = t // D_CMP - 1
        oc, sel = _cmp_attn(r3(q), kcv, tq, 0, n_c)
        y_n = _nsa_prompt(r3(q_r), r3(slc_bf), r3(win_bf), sel, oc, r3(gn), tq, min(512, t))
        win_state = r3(win_rows)[:, t - min(WINDOW, t):]
    else:
        cmp_cache, slc_cache, win_cache, page_table = past
        past_len = page_table.shape[1] * cmp_cache.shape[1]
        kcv = _compress_paged(cmp_cache, page_table, cw)
        n_c = (past_len + t) // D_CMP - 1
        oc, sel = _cmp_attn(r3(q), kcv, tq, past_len, n_c)
        y_n = _nsa_sample(r3(q_r), slc_cache, page_table, r3(slc_rows), win_cache, r3(win_rows), sel, oc, r3(gn))
        win_all = jnp.concatenate([win_cache, r3(win_rows)], axis=1)
        win_state = win_all[:, win_all.shape[1] - min(WINDOW, win_all.shape[1]):]

    y = _final(x2d, y_raw, bonus, g, y_n.reshape(m, NSA_W), gates, prm, tm)
    kv6 = lambda a: a.reshape(b, -1, 2, NSA_KV, HEAD_DIM)
    return (y.reshape(b, t, D_MODEL), kv6(r3(cmp_rows)), kv6(r3(slc_rows)), kv6(win_state),
            s_new, shift_new)


def kernel(x_prompt, x_sample, cache_cmp_kv, cache_slc_kv, cache_win_kv, state_rwkv, state_rwkv_shift, page_table, g_mix, w_in, rwkv_mu, rwkv_w0, rwkv_w_decay, rwkv_a0, rwkv_w_aaa, rwkv_w_gate, rwkv_k_k, rwkv_k_a, rwkv_r_k, rwkv_ln_w, rwkv_ln_b, nsa_pe_cmp, nsa_w_cmp1, nsa_w_cmp2, w_br_rwkv, w_br_nsa, w_out, g_ffn, w_ffn_gate, w_ffn_up, w_ffn_down, g_final):
    depth = w_in.shape[0]
    assert depth == 1, "single-layer trunk"
    b, t, _ = x_prompt.shape
    bs, ts, _ = x_sample.shape
    n_pool, page = cache_cmp_kv.shape[1:3]
    past_len = page_table.shape[1] * page
    assert (past_len + ts) // D_CMP == past_len // D_CMP and past_len % D_CMP == 0
    prm = _prep_params(0, g_mix, w_in, rwkv_mu, rwkv_w0, rwkv_w_decay, rwkv_a0, rwkv_w_aaa, rwkv_w_gate,
                       rwkv_k_k, rwkv_k_a, rwkv_r_k, rwkv_ln_w, rwkv_ln_b, w_br_rwkv, w_br_nsa, w_out,
                       g_ffn, w_ffn_gate, w_ffn_up, w_ffn_down, g_final)
    cw = _compress_weights(nsa_pe_cmp[0], nsa_w_cmp1[0], nsa_w_cmp2[0])
    flat = lambda c: c.reshape(n_pool, page, KV_ROW)

    outs_p = _group(x_prompt, jnp.arange(t), jnp.zeros((b, RWKV_COLS), F32),
                    jnp.zeros((b, RWKV_HEADS, RWKV_HEAD, RWKV_HEAD), F32), prm, cw, None,
                    tm=256, pre_blk=(1, min(512, t)), tq=min(128, t))
    wcache = cache_win_kv[0].reshape(bs, -1, KV_ROW)
    outs_s = _group(x_sample, past_len + jnp.arange(ts), state_rwkv_shift[0], state_rwkv[0], prm, cw,
                    (flat(cache_cmp_kv[0]), flat(cache_slc_kv[0]), wcache, page_table),
                    tm=256, pre_blk=(min(64, bs), ts), tq=ts)
    return (outs_p[0], outs_s[0]) + tuple(o[None] for o in outs_p[1:]) + tuple(o[None] for o in outs_s[1:])
```

```python
import functools

import jax
import jax.numpy as jnp
import numpy as np
from jax import lax
from jax.experimental import pallas as pl
from jax.experimental.pallas import tpu as pltpu

F32 = jnp.float32
BF16 = jnp.bfloat16

D_MODEL = 1024
RWKV_HEADS = 8
RWKV_HEAD = 64
RWKV_W = RWKV_HEADS * RWKV_HEAD
DECAY_RANK = 64
AAA_RANK = 64
GATE_RANK = 160
RWKV_GN_EPS = 64e-5
NSA_HEADS = 8
NSA_KV = 2
HEAD_DIM = 64
HPG = NSA_HEADS // NSA_KV
NSA_W = NSA_HEADS * HEAD_DIM
KV_W = NSA_KV * HEAD_DIM
L_CMP = 32
D_CMP = 16
CMP_HID = 128
L_SEL = 64
N_SEL = 16
WINDOW = 512
ROT_DIM = HEAD_DIM // 4
ROPE_THETA = 500000.0
D_FF = -(-8 * D_MODEL // (3 * 256)) * 256
NORM_EPS = 1e-6
NEG = -1e30
FORCE = 1e6
RWKV_COLS = 3 * RWKV_W + DECAY_RANK + AAA_RANK + GATE_RANK
NSA_COLS = NSA_W + 6 * KV_W + 3 * NSA_HEADS

LANES = 128
RW_PAD = 1920
GN_PAD = LANES
KV_ROW = 2 * KV_W
NBLK_PAD = 64
VMEM_LIMIT = 56 * 1024 * 1024


def _cparams(sem):
    return pltpu.CompilerParams(dimension_semantics=sem, vmem_limit_bytes=VMEM_LIMIT)


def _const_spec(shape):
    nd = len(shape)
    return pl.BlockSpec(shape, lambda *_: (0,) * nd, pipeline_mode=pl.Buffered(1))


def _dot(a, b):
    return jnp.dot(a.astype(BF16), b.astype(BF16), preferred_element_type=F32)


def _dot_nt(a, b):
    return lax.dot_general(a.astype(BF16), b.astype(BF16), (((1,), (1,)), ((), ())),
                           preferred_element_type=F32)


def _dot2(a, b):
    hi = a.astype(BF16)
    lo = (a - hi.astype(F32)).astype(BF16)
    return (jnp.dot(hi, b, preferred_element_type=F32)
            + jnp.dot(lo, b, preferred_element_type=F32))


def _sigmoid(x):
    return 1.0 / (1.0 + jnp.exp(-x))


def _silu(x):
    return x * _sigmoid(x)


def _rope128(x, cos, s1, s2):
    half = ROT_DIM // 2
    return x * cos + pltpu.roll(x, LANES - half, 1) * s1 + pltpu.roll(x, half, 1) * s2


_C_Q = RW_PAD
_C_CMP = _C_Q + NSA_W
_C_SLC = _C_CMP + KV_ROW
_C_WIN = _C_SLC + KV_ROW
_C_GN = _C_WIN + KV_ROW
_C_GATE = _C_GN + GN_PAD
_C_END = _C_GATE + 2 * D_MODEL


def _proj_body(x_ref, g_ref, w_ref, cos_ref, s1_ref, s2_ref,
               prw_ref, q_ref, qr_ref, cmp_ref, slc_ref, win_ref, gn_ref, gates_ref, slcb_ref, winb_ref):
    x = x_ref[...]
    ms = jnp.mean(x * x, axis=-1, keepdims=True)
    h = (x * lax.rsqrt(ms + NORM_EPS) * g_ref[...]).astype(BF16)

    def mm(a, b):
        return jnp.dot(h, w_ref[:, a:b], preferred_element_type=F32)

    cos, s1, s2 = cos_ref[...], s1_ref[...], s2_ref[...]
    prw_ref[...] = mm(0, _C_Q)
    q = mm(_C_Q, _C_CMP)
    q_ref[...] = q
    for c in range(NSA_W // LANES):
        sl = slice(c * LANES, (c + 1) * LANES)
        qr_ref[:, sl] = _rope128(q[:, sl], cos, s1, s2)
    cmp_ref[...] = mm(_C_CMP, _C_SLC)
    for lo, hi, f_ref, h_ref in ((_C_SLC, _C_WIN, slc_ref, slcb_ref), (_C_WIN, _C_GN, win_ref, winb_ref)):
        kv = mm(lo, hi)
        f_ref[:, :KV_W] = _rope128(kv[:, :KV_W], cos, s1, s2)
        f_ref[:, KV_W:] = kv[:, KV_W:]
        h_ref[...] = f_ref[...].astype(BF16)
    gn_ref[...] = mm(_C_GN, _C_GATE)
    gates_ref[...] = mm(_C_GATE, _C_END)


def _in_proj(x2d, g_mix, w_r, pos, tm):
    m = x2d.shape[0]
    tm = min(tm, m)
    period = max(pos.shape[0], tm) // tm
    tabs = _rope_tables(jnp.tile(pos, max(1, tm // pos.shape[0])))
    widths = (RW_PAD, NSA_W, NSA_W, KV_ROW, KV_ROW, KV_ROW, GN_PAD, 2 * D_MODEL)
    row = lambda w: pl.BlockSpec((tm, w), lambda i: (i, 0))
    tab = pl.BlockSpec((tm, LANES), lambda i: (i % period, 0))
    return pl.pallas_call(
        _proj_body,
        grid=(m // tm,),
        in_specs=[row(D_MODEL), _const_spec((1, D_MODEL)), _const_spec((D_MODEL, _C_END)),
                  tab, tab, tab],
        out_specs=[row(w) for w in widths] + [row(KV_ROW)] * 2,
        out_shape=[jax.ShapeDtypeStruct((m, w), F32) for w in widths]
                  + [jax.ShapeDtypeStruct((m, KV_ROW), BF16)] * 2,
        compiler_params=_cparams(("parallel",)),
        name="in_proj",
    )(x2d, g_mix, w_r, *tabs)


def _rwkv_pre_body(p_ref, sh_ref, mu_ref, wda_ref, w0_ref, a0_ref, wg_ref, kk_ref, ka_ref,
                   rk_ref, bd_ref, r_ref, lw_ref, k_ref, a_ref, b_ref, v_ref, bonus_ref,
                   g_ref, carry_ref):
    j = pl.program_id(1)
    bb, tt, c = p_ref.shape
    n = bb * tt
    x3 = p_ref[...]
    x = x3.reshape(n, c)
    prev = jnp.where(j == 0, sh_ref[...], carry_ref[...])
    carry_ref[...] = x3[:, tt - 1:tt, :]
    prev_rows = jnp.broadcast_to(prev, (bb, tt, c)).reshape(n, c)
    row = lax.broadcasted_iota(jnp.int32, (n, c), 0)
    shifted = jnp.where(row % tt == 0, prev_rows, pltpu.roll(x, 1, 0))
    xs = x + (shifted - x) * mu_ref[...]
    r = xs[:, 0:RWKV_W]
    k = xs[:, RWKV_W:2 * RWKV_W]
    v = xs[:, 2 * RWKV_W:3 * RWKV_W]
    wa = xs[:, 3 * RWKV_W:3 * RWKV_W + LANES]
    gl = xs[:, 3 * RWKV_W + LANES:RW_PAD]
    lane = lax.broadcasted_iota(jnp.int32, wa.shape, 1)
    z = _dot(jnp.where(lane < DECAY_RANK, jnp.tanh(wa), wa), wda_ref[...])
    u = -(w0_ref[...] + z[:, :RWKV_W])
    softplus = jnp.maximum(u, 0.0) + jnp.log1p(jnp.exp(-jnp.abs(u)))
    log_decay = -jnp.exp(-softplus - 0.5)
    a = _sigmoid(a0_ref[...] + z[:, RWKV_W:])
    g_ref[...] = _dot(_sigmoid(gl), wg_ref[...])
    bd = bd_ref[...]
    kk = k * kk_ref[...]
    kk = kk * lax.rsqrt(jnp.maximum(_dot2(kk * kk, bd), 1e-24))
    k_h = k * (1.0 + (a - 1.0) * ka_ref[...])
    bonus_ref[...] = _dot2(r * k_h * rk_ref[...], bd) * v
    v_ref[...] = v
    r_ref[...] = r
    lw_ref[...] = log_decay
    k_ref[...] = k_h
    a_ref[...] = -kk
    b_ref[...] = kk * a


def _rwkv_pre(p3, shift0, prm, bb, tt):
    b, t, _ = p3.shape
    m = b * t
    n = bb * tt
    nt = t // tt
    rowm = pl.BlockSpec((n, RWKV_W), lambda i, j: (i * nt + j, 0))
    outs = pl.pallas_call(
        _rwkv_pre_body,
        grid=(b // bb, nt),
        in_specs=[pl.BlockSpec((bb, tt, RW_PAD), lambda i, j: (i, j, 0)),
                  pl.BlockSpec((bb, 1, RW_PAD), lambda i, j: (i, 0, 0)),
                  _const_spec((1, RW_PAD)), _const_spec((LANES, 2 * RWKV_W)),
                  _const_spec((1, RWKV_W)), _const_spec((1, RWKV_W)),
                  _const_spec((RW_PAD - 3 * RWKV_W - LANES, RWKV_W)),
                  _const_spec((1, RWKV_W)), _const_spec((1, RWKV_W)), _const_spec((1, RWKV_W)),
                  _const_spec((RWKV_W, RWKV_W))],
        out_specs=[rowm] * 8,
        out_shape=[jax.ShapeDtypeStruct((m, RWKV_W), F32)] * 8,
        scratch_shapes=[pltpu.VMEM((bb, 1, RW_PAD), F32)],
        compiler_params=_cparams(("parallel", "arbitrary")),
        name="rwkv_pre",
    )(p3, shift0, prm["mu"], prm["wda"], prm["w0"], prm["a0"], prm["wg"], prm["k_k"],
      prm["k_a"], prm["r_k"], prm["bd"])
    return outs


RWKV_CHUNK = 64


def _split3(x):
    x1 = x.astype(BF16)
    r1 = x - x1.astype(F32)
    x2 = r1.astype(BF16)
    return x1, x2, (r1 - x2.astype(F32)).astype(BF16)


def _chunk_scan_body(r_ref, lw_ref, k_ref, a_ref, b_ref, v_ref, s0_ref, tri_ref, y_ref, s_ref):
    c = r_ref.shape[0]
    n = 2 * c
    first = pl.program_id(1) == 0
    left = lax.broadcasted_iota(jnp.int32, (c, LANES), 1) < RWKV_HEAD
    stack = lambda x: jnp.concatenate([jnp.where(left, x, 0.0), jnp.where(left, 0.0, x)], axis=0)
    ri = lax.broadcasted_iota(jnp.int32, (n, n), 0)
    ci = lax.broadcasted_iota(jnp.int32, (n, n), 1)
    strict = ((ri < c) == (ci < c)) & (ci < ri)
    eye = jnp.where(ri == ci, 1.0, 0.0)
    ri2 = lax.broadcasted_iota(jnp.int32, (n, 2 * n), 0)
    ci2 = lax.broadcasted_iota(jnp.int32, (n, 2 * n), 1) % n
    incl2 = ((ri2 < c) == (ci2 < c)) & (ci2 <= ri2)
    tri = tri_ref[...]
    pairs = range(RWKV_HEADS // 2)
    pre = []
    for p in pairs:
        sl = slice(p * LANES, (p + 1) * LANES)
        lw = lw_ref[:, sl]
        cs = sum(jnp.dot(tri, part, preferred_element_type=F32) for part in _split3(lw))
        c_end = cs[c - 1:c, :]
        e_neg = jnp.exp(-cs)
        e_hat = jnp.exp(c_end - cs)
        at = stack(a_ref[:, sl] * jnp.exp(cs - lw))
        rt = stack(r_ref[:, sl] * jnp.exp(cs))
        bt = stack(b_ref[:, sl] * e_neg)
        kt = stack(k_ref[:, sl] * e_neg)
        bk_hat = jnp.concatenate([stack(b_ref[:, sl] * e_hat), stack(k_ref[:, sl] * e_hat)], axis=0)
        g = _dot_nt(jnp.concatenate([at, rt], axis=0), jnp.concatenate([bt, kt], axis=0))
        pre.append(dict(sl=sl, e_end=jnp.exp(c_end), at=at, rt=rt, bk_hat=bk_hat.astype(BF16),
                        vs=stack(v_ref[:, sl]), low=jnp.where(strict, g[:n, :n], 0.0),
                        g_ak=jnp.where(strict, g[:n, n:], 0.0),
                        g_r=jnp.where(incl2, g[n:, :], 0.0)))
    tinv = [eye + q["low"] for q in pre]
    lp = [q["low"] for q in pre]
    for _ in range(c.bit_length() - 2):
        lp = [_dot(x, x) for x in lp]
        tinv = [t + _dot(t, x) for t, x in zip(tinv, lp)]
    state = [jnp.where(first, s0_ref[p], s_ref[p]) for p in pairs]
    rhs = [_dot_nt(q["at"], s) + _dot(q["g_ak"], q["vs"]) for q, s in zip(pre, state)]
    uv = [jnp.concatenate([_dot(t, x), q["vs"]], axis=0) for t, x, q in zip(tinv, rhs, pre)]
    for p, q, s, w in zip(pairs, pre, state, uv):
        ys = _dot_nt(q["rt"], s) + _dot(q["g_r"], w)
        y_ref[:, q["sl"]] = ys[:c] + ys[c:]
        s_ref[p] = s * q["e_end"] + lax.dot_general(
            w.astype(BF16), q["bk_hat"], (((0,), (0,)), ((), ())), preferred_element_type=F32)


def _rwkv_chunk_scan(rows, v, s0pair, t):
    m = v.shape[0]
    b = s0pair.shape[0]
    c = min(RWKV_CHUNK, t)
    nc = t // c
    rowm = pl.BlockSpec((c, RWKV_W), lambda i, j: (i * nc + j, 0))
    sspec = pl.BlockSpec((None, RWKV_HEADS // 2, LANES, LANES), lambda i, j: (i, 0, 0, 0))
    tri = jnp.asarray(np.tril(np.ones((c, c), np.float32)), BF16)
    return pl.pallas_call(
        _chunk_scan_body,
        grid=(b, nc),
        in_specs=[rowm] * 6 + [sspec, pl.BlockSpec((c, c), lambda i, j: (0, 0))],
        out_specs=[rowm, sspec],
        out_shape=[jax.ShapeDtypeStruct((m, RWKV_W), F32), jax.ShapeDtypeStruct(s0pair.shape, F32)],
        compiler_params=_cparams(("parallel", "arbitrary")),
        name="rwkv_chunk_scan",
    )(*rows, v, s0pair, tri)


def _state_to_blockdiag(s):
    b = s.shape[0]
    sp = s.reshape(b, RWKV_HEADS // 2, 2, RWKV_HEAD, RWKV_HEAD)
    z = jnp.zeros_like(sp[:, :, 0])
    top = jnp.concatenate([sp[:, :, 0], z], axis=-1)
    bot = jnp.concatenate([z, sp[:, :, 1]], axis=-1)
    return jnp.concatenate([top, bot], axis=-2)


def _blockdiag_to_state(sp):
    b = sp.shape[0]
    h0 = sp[:, :, :RWKV_HEAD, :RWKV_HEAD]
    h1 = sp[:, :, RWKV_HEAD:, RWKV_HEAD:]
    return jnp.stack([h0, h1], axis=2).reshape(b, RWKV_HEADS, RWKV_HEAD, RWKV_HEAD)


def _compress_body(*refs, n_src, native):
    if native:
        pe_ref, w1_ref, w2_ref, o_ref, xs_ref = refs[-5:]
        pages = refs[-5 - n_src:-5]
        page = pages[0].shape[-1]
        for k, pg in enumerate(pages):
            for c in range(2):
                xs_ref[c, k * page:(k + 1) * page, :] = pg[c].reshape(KV_W, page).T
        srcs_of = lambda c: [xs_ref.at[c]]
        per = n_src * page // D_CMP
    else:
        pe_ref, w1_ref, w2_ref, o_ref = refs[-4:]
        x_refs = refs[-4 - 2 * n_src:-4]
        srcs_of = lambda c: x_refs[c * n_src:(c + 1) * n_src]
        per = x_refs[0].shape[0] // D_CMP
    nseq, ncs = o_ref.shape[0], o_ref.shape[1]
    for c in range(2):
        srcs = srcs_of(c)
        rows = lambda l: jnp.concatenate([xr[pl.ds(l, per, stride=D_CMP), :] for xr in srcs], axis=0)
        for j in range(D_CMP // 2):
            xl = jnp.concatenate([rows(2 * j), rows(2 * j + 1)], axis=1)
            f = _dot(xl + pe_ref[c, j:j + 1, :], w1_ref[c, j])
            s = _dot(xl + pe_ref[c, D_CMP // 2 + j:D_CMP // 2 + j + 1, :], w1_ref[c, D_CMP // 2 + j])
            hf = f if j == 0 else hf + f
            hs = s if j == 0 else hs + s
        nc = hf.shape[0]
        out = _dot(_silu(hf + pltpu.roll(hs, nc - 1, 0)), w2_ref[c])
        for q in range(nseq):
            o_ref[q, :, c * KV_W:(c + 1) * KV_W] = out[q * ncs:(q + 1) * ncs]


def _compress_weights(pe, w1, w2):
    eye = jnp.eye(NSA_KV, dtype=F32)
    w1b = jnp.einsum("cldh,gy->clgdyh", w1, eye).reshape(2, L_CMP, KV_W, NSA_KV * CMP_HID)
    w2b = jnp.einsum("chd,gy->cghyd", w2, eye).reshape(2, NSA_KV * CMP_HID, KV_W)
    peb = jnp.broadcast_to(pe[:, :, None, :], (2, L_CMP, NSA_KV, HEAD_DIM)).reshape(2, L_CMP, KV_W)
    return (peb.reshape(2, L_CMP // 2, 2 * KV_W),
            w1b.reshape(2, L_CMP // 2, 2 * KV_W, NSA_KV * CMP_HID).astype(BF16), w2b.astype(BF16))


CMP_SEQS = 2


def _compress_call(n_src, nb, nseq, nc, in_specs_x, cw, n_prefetch, args, scratch=()):
    hid = NSA_KV * CMP_HID
    wspecs = [pl.BlockSpec((2, L_CMP // 2, 2 * KV_W), lambda *_: (0, 0, 0)),
              pl.BlockSpec((2, L_CMP // 2, 2 * KV_W, hid), lambda *_: (0, 0, 0, 0)),
              pl.BlockSpec((2, hid, KV_W), lambda *_: (0, 0, 0))]
    return pl.pallas_call(
        functools.partial(_compress_body, n_src=n_src, native=bool(scratch)),
        grid_spec=pltpu.PrefetchScalarGridSpec(
            num_scalar_prefetch=n_prefetch, grid=(nb // nseq,), in_specs=in_specs_x + wspecs,
            out_specs=pl.BlockSpec((nseq, nc, KV_ROW), lambda b, *_: (b, 0, 0)),
            scratch_shapes=list(scratch)),
        out_shape=jax.ShapeDtypeStruct((nb, nc, KV_ROW), F32),
        compiler_params=_cparams(("parallel",)),
        name="nsa_compress",
    )(*args, *cw)


def _compress_rows(rows3, cw):
    b, t, _ = rows3.shape
    xs = [pl.BlockSpec((None, t, KV_W), functools.partial(lambda c, i: (i, 0, c), c)) for c in range(2)]
    return _compress_call(1, b, 1, t // D_CMP, xs, cw, 0, (rows3, rows3))


def _page_spec(page, index_map):
    return pl.BlockSpec((None, 2, NSA_KV, HEAD_DIM, page), index_map)


def _compress_paged(cache_t, page_table, cw):
    b, n_pages = page_table.shape
    page = cache_t.shape[-1]
    nseq = CMP_SEQS if b % CMP_SEQS == 0 else 1
    xs = [_page_spec(page, functools.partial(lambda q, k, i, pt: (pt[i * nseq + q, k], 0, 0, 0, 0), q, k))
          for q in range(nseq) for k in range(n_pages)]
    n_src = nseq * n_pages
    return _compress_call(n_src, b, nseq, n_pages * page // D_CMP, xs, cw, 1,
                          (page_table,) + (cache_t,) * n_src,
                          scratch=[pltpu.VMEM((2, n_src * page, KV_W), F32)])


def _head_to_half(q_ref, h, g):
    c = q_ref[:, (h // 2) * LANES:(h // 2 + 1) * LANES]
    if h % 2 != g:
        c = pltpu.roll(c, HEAD_DIM, 1)
    lane = lax.broadcasted_iota(jnp.int32, c.shape, 1)
    return jnp.where((lane >= g * HEAD_DIM) & (lane < (g + 1) * HEAD_DIM), c, 0.0)


def _halves_to_heads(o_ref, outs, g):
    for pr in range(HPG // 2):
        tiles = []
        for hl in (2 * pr, 2 * pr + 1):
            h = g * HPG + hl
            o = outs[hl]
            if h % 2 != g:
                o = pltpu.roll(o, HEAD_DIM, 1)
            tiles.append(o)
        lane = lax.broadcasted_iota(jnp.int32, tiles[0].shape, 1)
        c = (g * HPG) // 2 + pr
        o_ref[:, c * LANES:(c + 1) * LANES] = jnp.where(lane < HEAD_DIM, tiles[0], tiles[1])


def _cmp_attn_body(q_ref, kcv_ref, ov_ref, oc_ref, sel_ref, *, tq, pos0, n_c):
    qi = pl.program_id(1)
    nb, ncp = kcv_ref.shape[0], kcv_ref.shape[1]
    n = nb * tq
    row = lax.broadcasted_iota(jnp.int32, (tq, ncp), 0)
    col = lax.broadcasted_iota(jnp.int32, (tq, ncp), 1)
    mask = (col * D_CMP + (L_CMP - 1) <= pos0 + qi * tq + row) & (col < n_c)
    imps = []
    for sb in range(nb):
        kc = kcv_ref[sb, :, :KV_W]
        vc = kcv_ref[sb, :, KV_W:]
        imp = None
        for g in range(NSA_KV):
            outs = []
            psum = None
            for hl in range(HPG):
                qz = _head_to_half(q_ref.at[sb], g * HPG + hl, g)
                s = jnp.where(mask, _dot_nt(qz, kc) * HEAD_DIM ** -0.5, NEG)
                e = jnp.exp(s - jnp.max(s, axis=-1, keepdims=True))
                p = jnp.where(mask, e / jnp.sum(e, axis=-1, keepdims=True), 0.0)
                outs.append(_dot(p, vc))
                psum = p if psum is None else psum + p
            _halves_to_heads(oc_ref.at[sb], outs, g)
            ig = _dot2(psum, ov_ref[g])
            imp = ig if imp is None else imp + ig
        imps.append(imp)
    imp_t = jnp.concatenate(imps, axis=0).T
    jb = lax.broadcasted_iota(jnp.int32, (LANES, n), 0) % NBLK_PAD
    cur = (pos0 + qi * tq + lax.broadcasted_iota(jnp.int32, (LANES, n), 1) % tq) // L_SEL
    forced = (jb == 0) | (jb == cur) | (jb == cur - 1)
    score = jnp.where(forced, FORCE, jnp.where(jb <= cur, imp_t, -1.0))
    sub = lax.broadcasted_iota(jnp.int32, (8, n), 0)
    ranks = []
    for g in range(NSA_KV):
        blocks = [score[g * NBLK_PAD + 8 * k:g * NBLK_PAD + 8 * k + 8, :] for k in range(NBLK_PAD // 8)]
        cnt = [jnp.zeros((8, n), F32) for _ in blocks]
        for jp in range(NBLK_PAD):
            other = jnp.broadcast_to(score[g * NBLK_PAD + jp:g * NBLK_PAD + jp + 1, :], (8, n))
            for k, blk in enumerate(blocks):
                if 8 * k > jp:
                    ahead = other >= blk
                elif 8 * k + 7 <= jp:
                    ahead = other > blk
                else:
                    ahead = (other > blk) | ((other == blk) & (sub > jp - 8 * k))
                cnt[k] = cnt[k] + jnp.where(ahead, 1.0, 0.0)
        ranks += cnt
    rank = jnp.concatenate(ranks, axis=0)
    sel = jnp.where((rank < N_SEL) & (jb <= cur), 1.0, 0.0).T
    for sb in range(nb):
        sel_ref[sb] = sel[sb * tq:(sb + 1) * tq]


def _overlap(ncp, n_c):
    ci = np.arange(ncp)[:, None]
    sj = np.arange(NBLK_PAD)[None, :]
    ov = ((ci * D_CMP < (sj + 1) * L_SEL) & (ci * D_CMP + L_CMP > sj * L_SEL) & (ci < n_c)).astype(np.float32)
    z = np.zeros_like(ov)
    return jnp.asarray(np.stack([np.concatenate([ov, z], 1), np.concatenate([z, ov], 1)]), BF16)


def _cmp_attn(q3, kcv, tq, pos0, n_c):
    b, t, _ = q3.shape
    ncp = kcv.shape[1]
    nb = LANES // tq
    blk = lambda r, w: pl.BlockSpec((nb, r, w), lambda i, j: (i, j, 0))
    return pl.pallas_call(
        functools.partial(_cmp_attn_body, tq=tq, pos0=pos0, n_c=n_c),
        grid=(b // nb, t // tq),
        in_specs=[blk(tq, NSA_W), pl.BlockSpec((nb, ncp, KV_ROW), lambda i, j: (i, 0, 0)),
                  pl.BlockSpec((NSA_KV, ncp, LANES), lambda i, j: (0, 0, 0))],
        out_specs=[blk(tq, NSA_W), blk(tq, LANES)],
        out_shape=[jax.ShapeDtypeStruct((b, t, NSA_W), F32), jax.ShapeDtypeStruct((b, t, LANES), F32)],
        compiler_params=_cparams(("parallel", "parallel")),
        name="nsa_cmp_attn",
    )(q3, kcv, _overlap(ncp, n_c))


def _gate_expand():
    e = np.zeros((GN_PAD, 3 * NSA_W), np.float32)
    for h in range(NSA_HEADS):
        for k in range(3):
            e[h * 3 + k, k * NSA_W + h * HEAD_DIM:k * NSA_W + (h + 1) * HEAD_DIM] = 1.0
    return jnp.asarray(e, BF16)


def _combine(y_ref, gn_ref, ge_ref, oc, osel, owin):
    gates = _dot2(_sigmoid(gn_ref[...]), ge_ref[...])
    y_ref[...] = (gates[:, :NSA_W] * oc + gates[:, NSA_W:2 * NSA_W] * osel + gates[:, 2 * NSA_W:] * owin)


def _per_head(allow):
    f = jnp.where(allow, 1.0, 0.0)
    return jnp.concatenate([f] * HPG, axis=0) > 0.5


def _sel_mask_tile(sel, g, kv0, tk):
    r = lax.broadcasted_iota(jnp.int32, (LANES, tk), 0)
    blk = (kv0 + lax.broadcasted_iota(jnp.int32, (LANES, tk), 1)) // L_SEL
    expand = (r == g * NBLK_PAD + blk).astype(BF16)
    return jnp.dot(sel.astype(BF16), expand, preferred_element_type=F32)


LOG2E = 1.4426950408889634


def _nsa_prompt_body(qr_ref, slc_ref, win_ref, oh_ref, sel_ref, oc_ref, gn_ref, ge_ref, y_ref,
                     os_ref, ow_ref, *, tq, tk, lw):
    qi = pl.program_id(1)
    q0 = qi * tq
    t_all = slc_ref.shape[0]
    nrow = HPG * tq
    heads = lambda a: jnp.concatenate([a] * HPG, axis=0)
    sel = sel_ref[...]
    lane = lax.broadcasted_iota(jnp.int32, (tq, LANES), 1)
    qpos = q0 + lax.broadcasted_iota(jnp.int32, (tq, 1), 0)
    w0 = pl.multiple_of(jnp.clip(q0 + tq - lw, 0, t_all - lw), 16)
    kpos_w = w0 + lax.broadcasted_iota(jnp.int32, (tq, lw), 1)
    wbias = heads(jnp.where((kpos_w <= qpos) & (kpos_w > qpos - WINDOW), 0.0, NEG))
    n_full = q0 // tk
    kv_diag = pl.multiple_of(n_full * tk, tk)
    kpos_d = kv_diag + lax.broadcasted_iota(jnp.int32, (tq, tk), 1)
    cbias = heads(jnp.where(kpos_d <= qpos, 0.0, NEG))
    qst, qaug = [], []
    for g in range(NSA_KV):
        q4 = jnp.concatenate([_head_to_half(qr_ref, g * HPG + hl, g) for hl in range(HPG)], axis=0)
        q4 = q4 * (HEAD_DIM ** -0.5 * LOG2E)
        selg = sel if g == 0 else pltpu.roll(sel, NBLK_PAD, 1)
        sbias = jnp.where(lane < NBLK_PAD, jnp.where(selg > 0.5, 0.0, NEG), 0.0)
        qst.append(q4.astype(BF16))
        qaug.append(jnp.concatenate([q4, heads(sbias)], axis=1).astype(BF16))

    def fold(carry, s, v):
        m, l, acc = carry
        m_new = jnp.maximum(m, jnp.max(s, axis=-1, keepdims=True))
        p = jnp.exp2(s - m_new)
        alpha = jnp.exp2(m - m_new)
        return (m_new, alpha * l + jnp.sum(p, axis=-1, keepdims=True),
                alpha * acc + jnp.dot(p.astype(BF16), v, preferred_element_type=F32))

    def sel_step(kv0, carries, bias):
        kaug = jnp.concatenate([slc_ref[pl.ds(kv0, tk), :KV_W], oh_ref[pl.ds(kv0, tk), :]], axis=1)
        v = slc_ref[pl.ds(kv0, tk), KV_W:]
        out = []
        for g in range(NSA_KV):
            s = lax.dot_general(qaug[g], kaug, (((1,), (1,)), ((), ())), preferred_element_type=F32)
            out.append(fold(carries[g], s if bias is None else s + bias, v))
        return tuple(out)

    init = (jnp.full((nrow, 1), NEG, F32), jnp.zeros((nrow, 1), F32), jnp.zeros((nrow, LANES), F32))
    carries = lax.fori_loop(0, n_full, lambda it, c: sel_step(pl.multiple_of(it * tk, tk), c, None),
                            (init,) * NSA_KV)
    carries = sel_step(kv_diag, carries, cbias)
    kw = win_ref[pl.ds(w0, lw), :KV_W]
    vw = win_ref[pl.ds(w0, lw), KV_W:]
    for g in range(NSA_KV):
        m, l, acc = carries[g]
        o = acc / l
        _halves_to_heads(os_ref, [o[hl * tq:(hl + 1) * tq] for hl in range(HPG)], g)
        s = lax.dot_general(qst[g], kw, (((1,), (1,)), ((), ())), preferred_element_type=F32) + wbias
        e = jnp.exp2(s - jnp.max(s, axis=-1, keepdims=True))
        o = jnp.dot(e.astype(BF16), vw, preferred_element_type=F32) / jnp.sum(e, axis=-1, keepdims=True)
        _halves_to_heads(ow_ref, [o[hl * tq:(hl + 1) * tq] for hl in range(HPG)], g)
    _combine(y_ref, gn_ref, ge_ref, oc_ref[...], os_ref[...], ow_ref[...])


def _nsa_prompt(qr3, slc3, win3, sel3, oc3, gn3, tq, tk):
    b, t, _ = qr3.shape
    lw = min(WINDOW + tq, t)
    qblk = lambda w: pl.BlockSpec((None, tq, w), lambda i, j: (i, j, 0))
    full = pl.BlockSpec((None, t, KV_ROW), lambda i, j: (i, 0, 0))
    onehot = jnp.asarray(np.arange(t)[:, None] // L_SEL == np.arange(LANES)[None, :], BF16)
    return pl.pallas_call(
        functools.partial(_nsa_prompt_body, tq=tq, tk=tk, lw=lw),
        grid=(b, t // tq),
        in_specs=[qblk(NSA_W), full, full, pl.BlockSpec((t, LANES), lambda i, j: (0, 0)),
                  qblk(LANES), qblk(NSA_W), qblk(GN_PAD),
                  pl.BlockSpec((GN_PAD, 3 * NSA_W), lambda i, j: (0, 0))],
        out_specs=qblk(NSA_W),
        out_shape=jax.ShapeDtypeStruct((b, t, NSA_W), F32),
        scratch_shapes=[pltpu.VMEM((tq, NSA_W), F32), pltpu.VMEM((tq, NSA_W), F32)],
        compiler_params=_cparams(("parallel", "arbitrary")),
        name="nsa_attn_prompt",
    )(qr3, slc3, win3, onehot, sel3, oc3, gn3, _gate_expand())


def _nsa_sample_body(*refs, n_pages, past_len):
    page_refs = refs[1:1 + n_pages]
    (qr_ref, snew_ref, wold_ref, wnew_ref, sel_ref, oc_ref, gn_ref, ge_ref, y_ref, os_ref, ow_ref) = refs[1 + n_pages:]
    tq = qr_ref.shape[0]
    page = page_refs[0].shape[-1]
    wb = wold_ref.shape[-1]
    sel = sel_ref[...]
    qpos = past_len + lax.broadcasted_iota(jnp.int32, (tq, 1), 0)

    def head_q(h):
        c = qr_ref[:, (h // 2) * LANES:(h // 2 + 1) * LANES]
        return (pltpu.roll(c, HEAD_DIM, 1) if h % 2 else c)[:, :HEAD_DIM]

    def attend(q, cached, fresh):
        scores = ([jnp.where(allow, _dot(q, kt), NEG) for kt, _, allow in cached]
                  + [jnp.where(allow, _dot_nt(q, k), NEG) for k, _, allow in fresh])
        masks = [p[2] for p in cached + fresh]
        m = functools.reduce(jnp.maximum, [jnp.max(s, axis=-1, keepdims=True) for s in scores])
        es = [jnp.where(allow, jnp.exp(s - m), 0.0) for s, allow in zip(scores, masks)]
        l = sum(jnp.sum(e, axis=-1, keepdims=True) for e in es)
        acc = sum([_dot_nt(e, vt) for e, (_, vt, _) in zip(es, cached)]
                  + [_dot(e, v) for e, (_, v, _) in zip(es[len(cached):], fresh)])
        return acc / l

    def put_heads(o_ref, o, g):
        for pr in range(HPG // 2):
            pair = jnp.concatenate([o[2 * pr * tq:(2 * pr + 1) * tq], o[(2 * pr + 1) * tq:(2 * pr + 2) * tq]], axis=1)
            c = (g * HPG) // 2 + pr
            o_ref[:, c * LANES:(c + 1) * LANES] = pair

    pad = lambda ref: jnp.concatenate([ref[...], jnp.zeros((page - tq, KV_ROW), F32)], axis=0)
    snew, wnew = pad(snew_ref), pad(wnew_ref)
    kpos_new = past_len + lax.broadcasted_iota(jnp.int32, (tq, page), 1)
    causal_new = kpos_new <= qpos
    kpos_old = past_len - wb + lax.broadcasted_iota(jnp.int32, (tq, wb), 1)
    old_ok = _per_head((kpos_old > qpos - WINDOW) & (kpos_old >= 0))
    new_ok = _per_head(causal_new & (kpos_new > qpos - WINDOW))
    for g in range(NSA_KV):
        q = jnp.concatenate([head_q(g * HPG + hl) for hl in range(HPG)], axis=0) * HEAD_DIM ** -0.5
        kcol = slice(g * HEAD_DIM, (g + 1) * HEAD_DIM)
        vcol = slice(KV_W + g * HEAD_DIM, KV_W + (g + 1) * HEAD_DIM)
        cached = [(page_refs[k][0, g], page_refs[k][1, g], _per_head(_sel_mask_tile(sel, g, k * page, page) > 0.5))
                  for k in range(n_pages)]
        fresh = [(snew[:, kcol], snew[:, vcol], _per_head((_sel_mask_tile(sel, g, past_len, page) > 0.5) & causal_new))]
        put_heads(os_ref, attend(q, cached, fresh), g)
        put_heads(ow_ref, attend(q, [(wold_ref[0, g], wold_ref[1, g], old_ok)], [(wnew[:, kcol], wnew[:, vcol], new_ok)]), g)
    _combine(y_ref, gn_ref, ge_ref, oc_ref[...], os_ref[...], ow_ref[...])


def _nsa_sample(qr3, slc_t, page_table, snew3, wold_t, wnew3, sel3, oc3, gn3):
    b, tq, _ = qr3.shape
    n_pages = page_table.shape[1]
    page = slc_t.shape[-1]
    wb = wold_t.shape[-1]
    per = lambda r, w: pl.BlockSpec((None, r, w), lambda i, pt: (i, 0, 0))
    pages = [_page_spec(page, functools.partial(lambda k, i, pt: (pt[i, k], 0, 0, 0, 0), k)) for k in range(n_pages)]
    return pl.pallas_call(
        functools.partial(_nsa_sample_body, n_pages=n_pages, past_len=n_pages * page),
        grid_spec=pltpu.PrefetchScalarGridSpec(
            num_scalar_prefetch=1, grid=(b,),
            in_specs=pages + [per(tq, NSA_W), per(tq, KV_ROW), _page_spec(wb, lambda i, pt: (i, 0, 0, 0, 0)),
                              per(tq, KV_ROW), per(tq, LANES), per(tq, NSA_W), per(tq, GN_PAD),
                              pl.BlockSpec((GN_PAD, 3 * NSA_W), lambda i, pt: (0, 0))],
            out_specs=per(tq, NSA_W),
            scratch_shapes=[pltpu.VMEM((tq, NSA_W), F32), pltpu.VMEM((tq, NSA_W), F32)]),
        out_shape=jax.ShapeDtypeStruct((b, tq, NSA_W), F32),
        compiler_params=_cparams(("arbitrary",)),
        name="nsa_attn_sample",
    )(page_table, *([slc_t] * n_pages), qr3, snew3, wold_t, wnew3, sel3, oc3, gn3, _gate_expand())


def _final_body(x_ref, yraw_ref, bonus_ref, g_ref, yn_ref, gates_ref, lnw_ref, lnb_ref, bd_ref,
                wbr_ref, wbn_ref, wo_ref, gffn_ref, wg_ref, wu_ref, wd_ref, gfin_ref, y_ref):
    bd = bd_ref[...]
    y = yraw_ref[...]
    d = y - _dot2(y, bd) * (1.0 / RWKV_HEAD)
    var = _dot2(d * d, bd) * (1.0 / RWKV_HEAD)
    y_r = (d * lax.rsqrt(var + RWKV_GN_EPS) * lnw_ref[...] + lnb_ref[...] + bonus_ref[...]) * g_ref[...]
    merged = (_sigmoid(gates_ref[:, :D_MODEL]) * _dot(y_r, wbr_ref[...])
              + _sigmoid(gates_ref[:, D_MODEL:]) * _dot(yn_ref[...], wbn_ref[...]))
    x1 = x_ref[...] + _dot(merged, wo_ref[...])
    h2 = x1 * lax.rsqrt(jnp.mean(x1 * x1, axis=-1, keepdims=True) + NORM_EPS) * gffn_ref[...]
    hb = h2.astype(BF16)
    up = _silu(jnp.dot(hb, wg_ref[...], preferred_element_type=F32)) * jnp.dot(hb, wu_ref[...], preferred_element_type=F32)
    x2 = x1 + _dot(up, wd_ref[...])
    y_ref[...] = x2 * lax.rsqrt(jnp.mean(x2 * x2, axis=-1, keepdims=True) + NORM_EPS) * gfin_ref[...]


def _final(x2d, yraw, bonus, g, yn, gates, prm, tm):
    m = x2d.shape[0]
    tm = min(tm, m)
    row = lambda w: pl.BlockSpec((tm, w), lambda i: (i, 0))
    return pl.pallas_call(
        _final_body,
        grid=(m // tm,),
        in_specs=[row(D_MODEL), row(RWKV_W), row(RWKV_W), row(RWKV_W), row(NSA_W), row(2 * D_MODEL),
                  _const_spec((1, RWKV_W)), _const_spec((1, RWKV_W)), _const_spec((RWKV_W, RWKV_W)),
                  _const_spec((RWKV_W, D_MODEL)), _const_spec((NSA_W, D_MODEL)),
                  _const_spec((D_MODEL, D_MODEL)), _const_spec((1, D_MODEL)),
                  _const_spec((D_MODEL, D_FF)), _const_spec((D_MODEL, D_FF)), _const_spec((D_FF, D_MODEL)),
                  _const_spec((1, D_MODEL))],
        out_specs=row(D_MODEL),
        out_shape=jax.ShapeDtypeStruct((m, D_MODEL), F32),
        compiler_params=_cparams(("parallel",)),
        name="merge_ffn",
    )(x2d, yraw, bonus, g, yn, gates, prm["ln_w"], prm["ln_b"], prm["bd"], prm["w_br_rwkv"],
      prm["w_br_nsa"], prm["w_out"], prm["g_ffn"], prm["w_ffn_gate"], prm["w_ffn_up"],
      prm["w_ffn_down"], prm["g_final"])


_S = (RWKV_W, RWKV_W + DECAY_RANK, 2 * RWKV_W + DECAY_RANK, 3 * RWKV_W + DECAY_RANK,
      3 * RWKV_W + DECAY_RANK + AAA_RANK)
_RW_PERM = np.concatenate([np.arange(0, _S[0]), np.arange(_S[1], _S[2]), np.arange(_S[2], _S[3]),
                           np.arange(_S[0], _S[1]), np.arange(_S[3], _S[4]), np.arange(_S[4], RWKV_COLS)])
_RW_INV = np.argsort(_RW_PERM)


def _prep_params(l, g_mix, w_in, rwkv_mu, rwkv_w0, rwkv_w_decay, rwkv_a0, rwkv_w_aaa, rwkv_w_gate,
                 rwkv_k_k, rwkv_k_a, rwkv_r_k, rwkv_ln_w, rwkv_ln_b, w_br_rwkv, w_br_nsa, w_out, g_ffn,
                 w_ffn_gate, w_ffn_up, w_ffn_down, g_final):
    w = w_in[l]
    nsa0 = RWKV_COLS
    kv_end = nsa0 + NSA_W + 6 * KV_W
    zc = lambda n: jnp.zeros((D_MODEL, n), F32)
    w_r = jnp.concatenate([w[:, _RW_PERM], zc(RW_PAD - RWKV_COLS), w[:, nsa0:kv_end],
                           w[:, kv_end:nsa0 + NSA_COLS], zc(GN_PAD - 3 * NSA_HEADS),
                           w[:, nsa0 + NSA_COLS:]], axis=1).astype(BF16)
    mu = jnp.concatenate([rwkv_mu[l][_RW_PERM], jnp.zeros((RW_PAD - RWKV_COLS,), F32)])[None]
    wda = jnp.zeros((LANES, 2 * RWKV_W), F32)
    wda = wda.at[:DECAY_RANK, :RWKV_W].set(rwkv_w_decay[l]).at[DECAY_RANK:, RWKV_W:].set(rwkv_w_aaa[l])
    wg = jnp.zeros((RW_PAD - 3 * RWKV_W - LANES, RWKV_W), F32).at[:GATE_RANK].set(rwkv_w_gate[l])
    hid = np.arange(RWKV_W) // RWKV_HEAD
    bd = jnp.asarray(hid[:, None] == hid[None, :], BF16)
    r1 = lambda a: a.reshape(1, -1)
    return dict(
        g_mix=r1(g_mix[l]), w_r=w_r, mu=mu, wda=wda.astype(BF16), w0=r1(rwkv_w0[l]), a0=r1(rwkv_a0[l]),
        wg=wg.astype(BF16), k_k=r1(rwkv_k_k[l]), k_a=r1(rwkv_k_a[l]), r_k=r1(rwkv_r_k[l]), bd=bd,
        ln_w=r1(rwkv_ln_w[l]), ln_b=r1(rwkv_ln_b[l]), w_br_rwkv=w_br_rwkv[l].astype(BF16),
        w_br_nsa=w_br_nsa[l].astype(BF16), w_out=w_out[l].astype(BF16), g_ffn=r1(g_ffn[l]),
        w_ffn_gate=w_ffn_gate[l].astype(BF16), w_ffn_up=w_ffn_up[l].astype(BF16),
        w_ffn_down=w_ffn_down[l].astype(BF16), g_final=r1(g_final))


def _rope_tables(pos):
    half = ROT_DIM // 2
    inv = ROPE_THETA ** (-jnp.arange(half, dtype=F32) / half)
    ang = pos.astype(F32)[:, None] * inv[None, :]
    cos, sin = jnp.cos(ang), jnp.sin(ang)
    n = pos.shape[0]
    one, zero = jnp.ones((n, HEAD_DIM - ROT_DIM), F32), jnp.zeros((n, HEAD_DIM - ROT_DIM), F32)
    z8 = jnp.zeros((n, half), F32)
    c = jnp.concatenate([cos, cos, one], 1)
    s1 = jnp.concatenate([-sin, z8, zero], 1)
    s2 = jnp.concatenate([z8, sin, zero], 1)
    return tuple(jnp.concatenate([a, a], 1) for a in (c, s1, s2))


def _group(x, pos, shift0, s0, prm, cw, past, tm, pre_blk, tq):
    b, t, _ = x.shape
    m = b * t
    x2d = x.reshape(m, D_MODEL)
    p_rw, q, q_r, cmp_rows, slc_rows, win_rows, gn, gates, slc_bf, win_bf = _in_proj(
        x2d, prm["g_mix"], prm["w_r"], pos, tm)

    shift_r = jnp.concatenate([shift0[:, _RW_PERM], jnp.zeros((b, RW_PAD - RWKV_COLS), F32)], 1)[:, None, :]
    *scan_in, v, bonus, g = _rwkv_pre(p_rw.reshape(b, t, RW_PAD), shift_r, prm, *pre_blk)
    y_raw, st = _rwkv_chunk_scan(scan_in, v, _state_to_blockdiag(s0), t)
    s_new = _blockdiag_to_state(st)
    shift_new = p_rw.reshape(b, t, RW_PAD)[:, -1, :RWKV_COLS][:, _RW_INV]

    r3 = lambda a: a.reshape(b, t, a.shape[-1])
    if past is None:
        kcv = _compress_rows(r3(cmp_rows), cw)
        n_c = t // D_CMP - 1
        oc, sel = _cmp_attn(r3(q), kcv, tq, 0, n_c)
        y_n = _nsa_prompt(r3(q_r), r3(slc_bf), r3(win_bf), sel, oc, r3(gn), tq, min(512, t))
        win_state = r3(win_rows)[:, t - min(WINDOW, t):]
    else:
        cmp_cache, slc_cache, win_cache, page_table = past
        rows_last = lambda c: jnp.moveaxis(c, 1, -1)
        past_len = page_table.shape[1] * cmp_cache.shape[1]
        kcv = _compress_paged(rows_last(cmp_cache), page_table, cw)
        n_c = (past_len + t) // D_CMP - 1
        oc, sel = _cmp_attn(r3(q), kcv, tq, past_len, n_c)
        y_n = _nsa_sample(r3(q_r), rows_last(slc_cache), page_table, r3(slc_rows), rows_last(win_cache),
                          r3(win_rows), sel, oc, r3(gn))
        win_all = jnp.concatenate([win_cache.reshape(b, -1, KV_ROW), r3(win_rows)], axis=1)
        win_state = win_all[:, win_all.shape[1] - min(WINDOW, win_all.shape[1]):]

    y = _final(x2d, y_raw, bonus, g, y_n.reshape(m, NSA_W), gates, prm, tm)
    kv6 = lambda a: a.reshape(b, -1, 2, NSA_KV, HEAD_DIM)
    return (y.reshape(b, t, D_MODEL), kv6(r3(cmp_rows)), kv6(r3(slc_rows)), kv6(win_state),
            s_new, shift_new)


def kernel(x_prompt, x_sample, cache_cmp_kv, cache_slc_kv, cache_win_kv, state_rwkv, state_rwkv_shift, page_table, g_mix, w_in, rwkv_mu, rwkv_w0, rwkv_w_decay, rwkv_a0, rwkv_w_aaa, rwkv_w_gate, rwkv_k_k, rwkv_k_a, rwkv_r_k, rwkv_ln_w, rwkv_ln_b, nsa_pe_cmp, nsa_w_cmp1, nsa_w_cmp2, w_br_rwkv, w_br_nsa, w_out, g_ffn, w_ffn_gate, w_ffn_up, w_ffn_down, g_final):
    depth = w_in.shape[0]
    assert depth == 1, "single-layer trunk"
    b, t, _ = x_prompt.shape
    bs, ts, _ = x_sample.shape
    n_pool, page = cache_cmp_kv.shape[1:3]
    past_len = page_table.shape[1] * page
    assert (past_len + ts) // D_CMP == past_len // D_CMP and past_len % D_CMP == 0
    prm = _prep_params(0, g_mix, w_in, rwkv_mu, rwkv_w0, rwkv_w_decay, rwkv_a0, rwkv_w_aaa, rwkv_w_gate,
                       rwkv_k_k, rwkv_k_a, rwkv_r_k, rwkv_ln_w, rwkv_ln_b, w_br_rwkv, w_br_nsa, w_out,
                       g_ffn, w_ffn_gate, w_ffn_up, w_ffn_down, g_final)
    cw = _compress_weights(nsa_pe_cmp[0], nsa_w_cmp1[0], nsa_w_cmp2[0])

    outs_p = _group(x_prompt, jnp.arange(t), jnp.zeros((b, RWKV_COLS), F32),
                    jnp.zeros((b, RWKV_HEADS, RWKV_HEAD, RWKV_HEAD), F32), prm, cw, None,
                    tm=256, pre_blk=(1, min(512, t)), tq=min(128, t))
    outs_s = _group(x_sample, past_len + jnp.arange(ts), state_rwkv_shift[0], state_rwkv[0], prm, cw,
                    (cache_cmp_kv[0], cache_slc_kv[0], cache_win_kv[0], page_table),
                    tm=256, pre_blk=(min(64, bs), ts), tq=ts)
    return (outs_p[0], outs_s[0]) + tuple(o[None] for o in outs_p[1:]) + tuple(o[None] for o in outs_s[1:])
```

```python
import functools

import jax
import jax.numpy as jnp
import numpy as np
from jax import lax
from jax.experimental import pallas as pl
from jax.experimental.pallas import tpu as pltpu

F32 = jnp.float32
BF16 = jnp.bfloat16

D_MODEL = 1024
RWKV_HEADS = 8
RWKV_HEAD = 64
RWKV_W = RWKV_HEADS * RWKV_HEAD
DECAY_RANK = 64
AAA_RANK = 64
GATE_RANK = 160
RWKV_GN_EPS = 64e-5
NSA_HEADS = 8
NSA_KV = 2
HEAD_DIM = 64
HPG = NSA_HEADS // NSA_KV
NSA_W = NSA_HEADS * HEAD_DIM
KV_W = NSA_KV * HEAD_DIM
L_CMP = 32
D_CMP = 16
CMP_HID = 128
L_SEL = 64
N_SEL = 16
WINDOW = 512
ROT_DIM = HEAD_DIM // 4
ROPE_THETA = 500000.0
D_FF = -(-8 * D_MODEL // (3 * 256)) * 256
NORM_EPS = 1e-6
NEG = -1e30
FORCE = 1e6
RWKV_COLS = 3 * RWKV_W + DECAY_RANK + AAA_RANK + GATE_RANK
NSA_COLS = NSA_W + 6 * KV_W + 3 * NSA_HEADS

LANES = 128
RW_PAD = 1920
GN_PAD = LANES
KV_ROW = 2 * KV_W
NBLK_PAD = 64
VMEM_LIMIT = 56 * 1024 * 1024


def _cparams(sem):
    return pltpu.CompilerParams(dimension_semantics=sem, vmem_limit_bytes=VMEM_LIMIT)


def _const_spec(shape):
    nd = len(shape)
    return pl.BlockSpec(shape, lambda *_: (0,) * nd, pipeline_mode=pl.Buffered(1))


def _dot(a, b):
    return jnp.dot(a.astype(BF16), b.astype(BF16), preferred_element_type=F32)


def _dot_nt(a, b):
    return lax.dot_general(a.astype(BF16), b.astype(BF16), (((1,), (1,)), ((), ())),
                           preferred_element_type=F32)


def _dot2(a, b):
    hi = a.astype(BF16)
    lo = (a - hi.astype(F32)).astype(BF16)
    return (jnp.dot(hi, b, preferred_element_type=F32)
            + jnp.dot(lo, b, preferred_element_type=F32))


def _sigmoid(x):
    return 1.0 / (1.0 + jnp.exp(-x))


def _silu(x):
    return x * _sigmoid(x)


def _rope128(x, cos, s1, s2):
    half = ROT_DIM // 2
    return x * cos + pltpu.roll(x, LANES - half, 1) * s1 + pltpu.roll(x, half, 1) * s2


_C_Q = RW_PAD
_C_CMP = _C_Q + NSA_W
_C_SLC = _C_CMP + KV_ROW
_C_WIN = _C_SLC + KV_ROW
_C_GN = _C_WIN + KV_ROW
_C_GATE = _C_GN + GN_PAD
_C_END = _C_GATE + 2 * D_MODEL


def _proj_body(x_ref, g_ref, w_ref, cos_ref, s1_ref, s2_ref,
               prw_ref, q_ref, qr_ref, cmp_ref, slc_ref, win_ref, gn_ref, gates_ref, slcb_ref, winb_ref):
    x = x_ref[...]
    ms = jnp.mean(x * x, axis=-1, keepdims=True)
    h = (x * lax.rsqrt(ms + NORM_EPS) * g_ref[...]).astype(BF16)

    def mm(a, b):
        return jnp.dot(h, w_ref[:, a:b], preferred_element_type=F32)

    cos, s1, s2 = cos_ref[...], s1_ref[...], s2_ref[...]
    prw_ref[...] = mm(0, _C_Q)
    q = mm(_C_Q, _C_CMP)
    q_ref[...] = q
    for c in range(NSA_W // LANES):
        sl = slice(c * LANES, (c + 1) * LANES)
        qr_ref[:, sl] = _rope128(q[:, sl], cos, s1, s2)
    cmp_ref[...] = mm(_C_CMP, _C_SLC)
    for lo, hi, f_ref, h_ref in ((_C_SLC, _C_WIN, slc_ref, slcb_ref), (_C_WIN, _C_GN, win_ref, winb_ref)):
        kv = mm(lo, hi)
        f_ref[:, :KV_W] = _rope128(kv[:, :KV_W], cos, s1, s2)
        f_ref[:, KV_W:] = kv[:, KV_W:]
        h_ref[...] = f_ref[...].astype(BF16)
    gn_ref[...] = mm(_C_GN, _C_GATE)
    gates_ref[...] = mm(_C_GATE, _C_END)


def _in_proj(x2d, g_mix, w_r, pos, tm):
    m = x2d.shape[0]
    tm = min(tm, m)
    period = max(pos.shape[0], tm) // tm
    tabs = _rope_tables(jnp.tile(pos, max(1, tm // pos.shape[0])))
    widths = (RW_PAD, NSA_W, NSA_W, KV_ROW, KV_ROW, KV_ROW, GN_PAD, 2 * D_MODEL)
    row = lambda w: pl.BlockSpec((tm, w), lambda i: (i, 0))
    tab = pl.BlockSpec((tm, LANES), lambda i: (i % period, 0))
    return pl.pallas_call(
        _proj_body,
        grid=(m // tm,),
        in_specs=[row(D_MODEL), _const_spec((1, D_MODEL)), _const_spec((D_MODEL, _C_END)),
                  tab, tab, tab],
        out_specs=[row(w) for w in widths] + [row(KV_ROW)] * 2,
        out_shape=[jax.ShapeDtypeStruct((m, w), F32) for w in widths]
                  + [jax.ShapeDtypeStruct((m, KV_ROW), BF16)] * 2,
        compiler_params=_cparams(("parallel",)),
        name="in_proj",
    )(x2d, g_mix, w_r, *tabs)


def _rwkv_pre_body(p_ref, sh_ref, mu_ref, wda_ref, w0_ref, a0_ref, wg_ref, kk_ref, ka_ref,
                   rk_ref, bd_ref, r_ref, lw_ref, k_ref, a_ref, b_ref, v_ref, bonus_ref,
                   g_ref, carry_ref):
    j = pl.program_id(1)
    bb, tt, c = p_ref.shape
    n = bb * tt
    x3 = p_ref[...]
    x = x3.reshape(n, c)
    prev = jnp.where(j == 0, sh_ref[...], carry_ref[...])
    carry_ref[...] = x3[:, tt - 1:tt, :]
    prev_rows = jnp.broadcast_to(prev, (bb, tt, c)).reshape(n, c)
    row = lax.broadcasted_iota(jnp.int32, (n, c), 0)
    shifted = jnp.where(row % tt == 0, prev_rows, pltpu.roll(x, 1, 0))
    xs = x + (shifted - x) * mu_ref[...]
    r = xs[:, 0:RWKV_W]
    k = xs[:, RWKV_W:2 * RWKV_W]
    v = xs[:, 2 * RWKV_W:3 * RWKV_W]
    wa = xs[:, 3 * RWKV_W:3 * RWKV_W + LANES]
    gl = xs[:, 3 * RWKV_W + LANES:RW_PAD]
    lane = lax.broadcasted_iota(jnp.int32, wa.shape, 1)
    z = _dot(jnp.where(lane < DECAY_RANK, jnp.tanh(wa), wa), wda_ref[...])
    u = -(w0_ref[...] + z[:, :RWKV_W])
    softplus = jnp.maximum(u, 0.0) + jnp.log1p(jnp.exp(-jnp.abs(u)))
    log_decay = -jnp.exp(-softplus - 0.5)
    a = _sigmoid(a0_ref[...] + z[:, RWKV_W:])
    g_ref[...] = _dot(_sigmoid(gl), wg_ref[...])
    bd = bd_ref[...]
    kk = k * kk_ref[...]
    kk = kk * lax.rsqrt(jnp.maximum(_dot2(kk * kk, bd), 1e-24))
    k_h = k * (1.0 + (a - 1.0) * ka_ref[...])
    bonus_ref[...] = _dot2(r * k_h * rk_ref[...], bd) * v
    v_ref[...] = v
    r_ref[...] = r
    lw_ref[...] = log_decay
    k_ref[...] = k_h
    a_ref[...] = -kk
    b_ref[...] = kk * a


def _rwkv_pre(p3, shift0, prm, bb, tt):
    b, t, _ = p3.shape
    m = b * t
    n = bb * tt
    nt = t // tt
    rowm = pl.BlockSpec((n, RWKV_W), lambda i, j: (i * nt + j, 0))
    outs = pl.pallas_call(
        _rwkv_pre_body,
        grid=(b // bb, nt),
        in_specs=[pl.BlockSpec((bb, tt, RW_PAD), lambda i, j: (i, j, 0)),
                  pl.BlockSpec((bb, 1, RW_PAD), lambda i, j: (i, 0, 0)),
                  _const_spec((1, RW_PAD)), _const_spec((LANES, 2 * RWKV_W)),
                  _const_spec((1, RWKV_W)), _const_spec((1, RWKV_W)),
                  _const_spec((RW_PAD - 3 * RWKV_W - LANES, RWKV_W)),
                  _const_spec((1, RWKV_W)), _const_spec((1, RWKV_W)), _const_spec((1, RWKV_W)),
                  _const_spec((RWKV_W, RWKV_W))],
        out_specs=[rowm] * 8,
        out_shape=[jax.ShapeDtypeStruct((m, RWKV_W), F32)] * 8,
        scratch_shapes=[pltpu.VMEM((bb, 1, RW_PAD), F32)],
        compiler_params=_cparams(("parallel", "arbitrary")),
        name="rwkv_pre",
    )(p3, shift0, prm["mu"], prm["wda"], prm["w0"], prm["a0"], prm["wg"], prm["k_k"],
      prm["k_a"], prm["r_k"], prm["bd"])
    return outs


RWKV_CHUNK = 64


def _split3(x):
    x1 = x.astype(BF16)
    r1 = x - x1.astype(F32)
    x2 = r1.astype(BF16)
    return x1, x2, (r1 - x2.astype(F32)).astype(BF16)


def _chunk_scan_body(r_ref, lw_ref, k_ref, a_ref, b_ref, v_ref, s0_ref, tri_ref, y_ref, s_ref):
    nbat, c = r_ref.shape[0], r_ref.shape[1]
    n = 2 * c
    first = pl.program_id(1) == 0
    left = lax.broadcasted_iota(jnp.int32, (c, LANES), 1) < RWKV_HEAD
    stack = lambda x: jnp.concatenate([jnp.where(left, x, 0.0), jnp.where(left, 0.0, x)], axis=0)
    ri = lax.broadcasted_iota(jnp.int32, (n, n), 0)
    ci = lax.broadcasted_iota(jnp.int32, (n, n), 1)
    strict = ((ri < c) == (ci < c)) & (ci < ri)
    eye = jnp.where(ri == ci, 1.0, 0.0)
    ri2 = lax.broadcasted_iota(jnp.int32, (n, 2 * n), 0)
    ci2 = lax.broadcasted_iota(jnp.int32, (n, 2 * n), 1) % n
    incl2 = ((ri2 < c) == (ci2 < c)) & (ci2 <= ri2)
    tri = tri_ref[...]
    chains = [(i, p) for i in range(nbat) for p in range(RWKV_HEADS // 2)]
    pre = []
    for i, p in chains:
        sl = slice(p * LANES, (p + 1) * LANES)
        lw = lw_ref[i, :, sl]
        cs = sum(jnp.dot(tri, part, preferred_element_type=F32) for part in _split3(lw))
        c_end = cs[c - 1:c, :]
        e_neg = jnp.exp(-cs)
        e_hat = jnp.exp(c_end - cs)
        at = stack(a_ref[i, :, sl] * jnp.exp(cs - lw))
        rt = stack(r_ref[i, :, sl] * jnp.exp(cs))
        bt = stack(b_ref[i, :, sl] * e_neg)
        kt = stack(k_ref[i, :, sl] * e_neg)
        bk_hat = jnp.concatenate([stack(b_ref[i, :, sl] * e_hat), stack(k_ref[i, :, sl] * e_hat)], axis=0)
        g = _dot_nt(jnp.concatenate([at, rt], axis=0), jnp.concatenate([bt, kt], axis=0))
        pre.append(dict(sl=sl, e_end=jnp.exp(c_end), at=at, rt=rt, bk_hat=bk_hat.astype(BF16),
                        vs=stack(v_ref[i, :, sl]), low=jnp.where(strict, g[:n, :n], 0.0),
                        g_ak=jnp.where(strict, g[:n, n:], 0.0),
                        g_r=jnp.where(incl2, g[n:, :], 0.0)))
    tinv = [eye + q["low"] for q in pre]
    lp = [q["low"] for q in pre]
    for _ in range(c.bit_length() - 2):
        lp = [_dot(x, x) for x in lp]
        tinv = [t + _dot(t, x) for t, x in zip(tinv, lp)]
    state = [jnp.where(first, s0_ref[i, p], s_ref[i, p]) for i, p in chains]
    rhs = [_dot_nt(q["at"], s) + _dot(q["g_ak"], q["vs"]) for q, s in zip(pre, state)]
    uv = [jnp.concatenate([_dot(t, x), q["vs"]], axis=0) for t, x, q in zip(tinv, rhs, pre)]
    for (i, p), q, s, w in zip(chains, pre, state, uv):
        ys = _dot_nt(q["rt"], s) + _dot(q["g_r"], w)
        y_ref[i, :, q["sl"]] = ys[:c] + ys[c:]
        s_ref[i, p] = s * q["e_end"] + lax.dot_general(
            w.astype(BF16), q["bk_hat"], (((0,), (0,)), ((), ())), preferred_element_type=F32)


SCAN_SEQS = 8


def _rwkv_chunk_scan(rows, v, s0pair, t):
    m = v.shape[0]
    b = s0pair.shape[0]
    c = min(RWKV_CHUNK, t)
    nbat = next(d for d in range(SCAN_SEQS, 0, -1) if b % d == 0)
    as3 = lambda x: x.reshape(b, t, RWKV_W)
    rowm = pl.BlockSpec((nbat, c, RWKV_W), lambda i, j: (i, j, 0))
    sspec = pl.BlockSpec((nbat, RWKV_HEADS // 2, LANES, LANES), lambda i, j: (i, 0, 0, 0))
    tri = jnp.asarray(np.tril(np.ones((c, c), np.float32)), BF16)
    y, st = pl.pallas_call(
        _chunk_scan_body,
        grid=(b // nbat, t // c),
        in_specs=[rowm] * 6 + [sspec, pl.BlockSpec((c, c), lambda i, j: (0, 0))],
        out_specs=[rowm, sspec],
        out_shape=[jax.ShapeDtypeStruct((b, t, RWKV_W), F32), jax.ShapeDtypeStruct(s0pair.shape, F32)],
        compiler_params=_cparams(("parallel", "arbitrary")),
        name="rwkv_chunk_scan",
    )(*[as3(x) for x in rows], as3(v), s0pair, tri)
    return y.reshape(m, RWKV_W), st


def _state_to_blockdiag(s):
    b = s.shape[0]
    sp = s.reshape(b, RWKV_HEADS // 2, 2, RWKV_HEAD, RWKV_HEAD)
    z = jnp.zeros_like(sp[:, :, 0])
    top = jnp.concatenate([sp[:, :, 0], z], axis=-1)
    bot = jnp.concatenate([z, sp[:, :, 1]], axis=-1)
    return jnp.concatenate([top, bot], axis=-2)


def _blockdiag_to_state(sp):
    b = sp.shape[0]
    h0 = sp[:, :, :RWKV_HEAD, :RWKV_HEAD]
    h1 = sp[:, :, RWKV_HEAD:, RWKV_HEAD:]
    return jnp.stack([h0, h1], axis=2).reshape(b, RWKV_HEADS, RWKV_HEAD, RWKV_HEAD)


def _compress_body(*refs, n_src, native):
    if native:
        pe_ref, w1_ref, w2_ref, o_ref, xs_ref = refs[-5:]
        pages = refs[-5 - n_src:-5]
        page = pages[0].shape[-1]
        for k, pg in enumerate(pages):
            for c in range(2):
                xs_ref[c, k * page:(k + 1) * page, :] = pg[c].reshape(KV_W, page).T
        srcs_of = lambda c: [xs_ref.at[c]]
        per = n_src * page // D_CMP
    else:
        pe_ref, w1_ref, w2_ref, o_ref = refs[-4:]
        x_refs = refs[-4 - 2 * n_src:-4]
        srcs_of = lambda c: x_refs[c * n_src:(c + 1) * n_src]
        per = x_refs[0].shape[0] // D_CMP
    nseq, ncs = o_ref.shape[0], o_ref.shape[1]
    for c in range(2):
        srcs = srcs_of(c)
        rows = lambda l: jnp.concatenate([xr[pl.ds(l, per, stride=D_CMP), :] for xr in srcs], axis=0)
        for j in range(D_CMP // 2):
            xl = jnp.concatenate([rows(2 * j), rows(2 * j + 1)], axis=1)
            f = _dot(xl + pe_ref[c, j:j + 1, :], w1_ref[c, j])
            s = _dot(xl + pe_ref[c, D_CMP // 2 + j:D_CMP // 2 + j + 1, :], w1_ref[c, D_CMP // 2 + j])
            hf = f if j == 0 else hf + f
            hs = s if j == 0 else hs + s
        nc = hf.shape[0]
        out = _dot(_silu(hf + pltpu.roll(hs, nc - 1, 0)), w2_ref[c])
        for q in range(nseq):
            o_ref[q, :, c * KV_W:(c + 1) * KV_W] = out[q * ncs:(q + 1) * ncs]


def _compress_weights(pe, w1, w2):
    eye = jnp.eye(NSA_KV, dtype=F32)
    w1b = jnp.einsum("cldh,gy->clgdyh", w1, eye).reshape(2, L_CMP, KV_W, NSA_KV * CMP_HID)
    w2b = jnp.einsum("chd,gy->cghyd", w2, eye).reshape(2, NSA_KV * CMP_HID, KV_W)
    peb = jnp.broadcast_to(pe[:, :, None, :], (2, L_CMP, NSA_KV, HEAD_DIM)).reshape(2, L_CMP, KV_W)
    return (peb.reshape(2, L_CMP // 2, 2 * KV_W),
            w1b.reshape(2, L_CMP // 2, 2 * KV_W, NSA_KV * CMP_HID).astype(BF16), w2b.astype(BF16))


CMP_SEQS = 2


def _compress_call(n_src, nb, nseq, nc, in_specs_x, cw, n_prefetch, args, scratch=()):
    hid = NSA_KV * CMP_HID
    wspecs = [pl.BlockSpec((2, L_CMP // 2, 2 * KV_W), lambda *_: (0, 0, 0)),
              pl.BlockSpec((2, L_CMP // 2, 2 * KV_W, hid), lambda *_: (0, 0, 0, 0)),
              pl.BlockSpec((2, hid, KV_W), lambda *_: (0, 0, 0))]
    return pl.pallas_call(
        functools.partial(_compress_body, n_src=n_src, native=bool(scratch)),
        grid_spec=pltpu.PrefetchScalarGridSpec(
            num_scalar_prefetch=n_prefetch, grid=(nb // nseq,), in_specs=in_specs_x + wspecs,
            out_specs=pl.BlockSpec((nseq, nc, KV_ROW), lambda b, *_: (b, 0, 0)),
            scratch_shapes=list(scratch)),
        out_shape=jax.ShapeDtypeStruct((nb, nc, KV_ROW), F32),
        compiler_params=_cparams(("parallel",)),
        name="nsa_compress",
    )(*args, *cw)


def _compress_rows(rows3, cw):
    b, t, _ = rows3.shape
    xs = [pl.BlockSpec((None, t, KV_W), functools.partial(lambda c, i: (i, 0, c), c)) for c in range(2)]
    return _compress_call(1, b, 1, t // D_CMP, xs, cw, 0, (rows3, rows3))


def _page_spec(page, index_map):
    return pl.BlockSpec((None, 2, NSA_KV, HEAD_DIM, page), index_map)


def _compress_paged(cache_t, page_table, cw):
    b, n_pages = page_table.shape
    page = cache_t.shape[-1]
    nseq = CMP_SEQS if b % CMP_SEQS == 0 else 1
    xs = [_page_spec(page, functools.partial(lambda q, k, i, pt: (pt[i * nseq + q, k], 0, 0, 0, 0), q, k))
          for q in range(nseq) for k in range(n_pages)]
    n_src = nseq * n_pages
    return _compress_call(n_src, b, nseq, n_pages * page // D_CMP, xs, cw, 1,
                          (page_table,) + (cache_t,) * n_src,
                          scratch=[pltpu.VMEM((2, n_src * page, KV_W), F32)])


def _head_to_half(q_ref, h, g):
    c = q_ref[:, (h // 2) * LANES:(h // 2 + 1) * LANES]
    if h % 2 != g:
        c = pltpu.roll(c, HEAD_DIM, 1)
    lane = lax.broadcasted_iota(jnp.int32, c.shape, 1)
    return jnp.where((lane >= g * HEAD_DIM) & (lane < (g + 1) * HEAD_DIM), c, 0.0)


def _halves_to_heads(o_ref, outs, g):
    for pr in range(HPG // 2):
        tiles = []
        for hl in (2 * pr, 2 * pr + 1):
            h = g * HPG + hl
            o = outs[hl]
            if h % 2 != g:
                o = pltpu.roll(o, HEAD_DIM, 1)
            tiles.append(o)
        lane = lax.broadcasted_iota(jnp.int32, tiles[0].shape, 1)
        c = (g * HPG) // 2 + pr
        o_ref[:, c * LANES:(c + 1) * LANES] = jnp.where(lane < HEAD_DIM, tiles[0], tiles[1])


def _cmp_attn_body(q_ref, kcv_ref, ov_ref, oc_ref, sel_ref, *, tq, pos0, n_c):
    qi = pl.program_id(1)
    nb, ncp = kcv_ref.shape[0], kcv_ref.shape[1]
    n = nb * tq
    row = lax.broadcasted_iota(jnp.int32, (tq, ncp), 0)
    col = lax.broadcasted_iota(jnp.int32, (tq, ncp), 1)
    mask = _per_head((col * D_CMP + (L_CMP - 1) <= pos0 + qi * tq + row) & (col < n_c))
    imps = []
    for sb in range(nb):
        kc = kcv_ref[sb, :, :KV_W]
        vc = kcv_ref[sb, :, KV_W:]
        imp = None
        for g in range(NSA_KV):
            q4 = jnp.concatenate([_head_to_half(q_ref.at[sb], g * HPG + hl, g) for hl in range(HPG)], axis=0)
            s = jnp.where(mask, _dot_nt(q4, kc) * HEAD_DIM ** -0.5, NEG)
            e = jnp.exp(s - jnp.max(s, axis=-1, keepdims=True))
            p = jnp.where(mask, e / jnp.sum(e, axis=-1, keepdims=True), 0.0)
            o = _dot(p, vc)
            _halves_to_heads(oc_ref.at[sb], [o[hl * tq:(hl + 1) * tq] for hl in range(HPG)], g)
            psum = sum(p[hl * tq:(hl + 1) * tq] for hl in range(HPG))
            ig = _dot2(psum, ov_ref[g])
            imp = ig if imp is None else imp + ig
        imps.append(imp)
    imp_t = jnp.concatenate(imps, axis=0).T
    jb = lax.broadcasted_iota(jnp.int32, (LANES, n), 0) % NBLK_PAD
    cur = (pos0 + qi * tq + lax.broadcasted_iota(jnp.int32, (LANES, n), 1) % tq) // L_SEL
    forced = (jb == 0) | (jb == cur) | (jb == cur - 1)
    score = jnp.where(forced, FORCE, jnp.where(jb <= cur, imp_t, -1.0))
    sub = lax.broadcasted_iota(jnp.int32, (8, n), 0)
    ranks = []
    for g in range(NSA_KV):
        blocks = [score[g * NBLK_PAD + 8 * k:g * NBLK_PAD + 8 * k + 8, :] for k in range(NBLK_PAD // 8)]
        cnt = [jnp.zeros((8, n), F32) for _ in blocks]
        for jp in range(NBLK_PAD):
            other = jnp.broadcast_to(score[g * NBLK_PAD + jp:g * NBLK_PAD + jp + 1, :], (8, n))
            for k, blk in enumerate(blocks):
                if 8 * k > jp:
                    ahead = other >= blk
                elif 8 * k + 7 <= jp:
                    ahead = other > blk
                else:
                    ahead = (other > blk) | ((other == blk) & (sub > jp - 8 * k))
                cnt[k] = cnt[k] + jnp.where(ahead, 1.0, 0.0)
        ranks += cnt
    rank = jnp.concatenate(ranks, axis=0)
    sel = jnp.where((rank < N_SEL) & (jb <= cur), 1.0, 0.0).T
    for sb in range(nb):
        sel_ref[sb] = sel[sb * tq:(sb + 1) * tq]


def _overlap(ncp, n_c):
    ci = np.arange(ncp)[:, None]
    sj = np.arange(NBLK_PAD)[None, :]
    ov = ((ci * D_CMP < (sj + 1) * L_SEL) & (ci * D_CMP + L_CMP > sj * L_SEL) & (ci < n_c)).astype(np.float32)
    z = np.zeros_like(ov)
    return jnp.asarray(np.stack([np.concatenate([ov, z], 1), np.concatenate([z, ov], 1)]), BF16)


def _cmp_attn(q3, kcv, tq, pos0, n_c):
    b, t, _ = q3.shape
    ncp = kcv.shape[1]
    nb = LANES // tq
    blk = lambda r, w: pl.BlockSpec((nb, r, w), lambda i, j: (i, j, 0))
    return pl.pallas_call(
        functools.partial(_cmp_attn_body, tq=tq, pos0=pos0, n_c=n_c),
        grid=(b // nb, t // tq),
        in_specs=[blk(tq, NSA_W), pl.BlockSpec((nb, ncp, KV_ROW), lambda i, j: (i, 0, 0)),
                  pl.BlockSpec((NSA_KV, ncp, LANES), lambda i, j: (0, 0, 0))],
        out_specs=[blk(tq, NSA_W), blk(tq, LANES)],
        out_shape=[jax.ShapeDtypeStruct((b, t, NSA_W), F32), jax.ShapeDtypeStruct((b, t, LANES), F32)],
        compiler_params=_cparams(("parallel", "parallel")),
        name="nsa_cmp_attn",
    )(q3, kcv, _overlap(ncp, n_c))


def _gate_expand():
    e = np.zeros((GN_PAD, 3 * NSA_W), np.float32)
    for h in range(NSA_HEADS):
        for k in range(3):
            e[h * 3 + k, k * NSA_W + h * HEAD_DIM:k * NSA_W + (h + 1) * HEAD_DIM] = 1.0
    return jnp.asarray(e, BF16)


def _combine(y_ref, gn_ref, ge_ref, oc, osel, owin):
    gates = _dot2(_sigmoid(gn_ref[...]), ge_ref[...])
    y_ref[...] = (gates[:, :NSA_W] * oc + gates[:, NSA_W:2 * NSA_W] * osel + gates[:, 2 * NSA_W:] * owin)


def _per_head(allow):
    f = jnp.where(allow, 1.0, 0.0)
    return jnp.concatenate([f] * HPG, axis=0) > 0.5


def _sel_mask_tile(sel, g, kv0, tk):
    r = lax.broadcasted_iota(jnp.int32, (LANES, tk), 0)
    blk = (kv0 + lax.broadcasted_iota(jnp.int32, (LANES, tk), 1)) // L_SEL
    expand = (r == g * NBLK_PAD + blk).astype(BF16)
    return jnp.dot(sel.astype(BF16), expand, preferred_element_type=F32)


LOG2E = 1.4426950408889634


def _nsa_prompt_body(qr_ref, slc_ref, win_ref, oh_ref, sel_ref, oc_ref, gn_ref, ge_ref, y_ref,
                     os_ref, ow_ref, *, tq, tk, lw):
    qi = pl.program_id(1)
    q0 = qi * tq
    t_all = slc_ref.shape[0]
    nrow = HPG * tq
    heads = lambda a: jnp.concatenate([a] * HPG, axis=0)
    sel = sel_ref[...]
    lane = lax.broadcasted_iota(jnp.int32, (tq, LANES), 1)
    qpos = q0 + lax.broadcasted_iota(jnp.int32, (tq, 1), 0)
    w0 = pl.multiple_of(jnp.clip(q0 + tq - lw, 0, t_all - lw), 16)
    kpos_w = w0 + lax.broadcasted_iota(jnp.int32, (tq, lw), 1)
    wbias = heads(jnp.where((kpos_w <= qpos) & (kpos_w > qpos - WINDOW), 0.0, NEG))
    n_full = q0 // tk
    kv_diag = pl.multiple_of(n_full * tk, tk)
    kpos_d = kv_diag + lax.broadcasted_iota(jnp.int32, (tq, tk), 1)
    cbias = heads(jnp.where(kpos_d <= qpos, 0.0, NEG))
    qst, qaug = [], []
    for g in range(NSA_KV):
        q4 = jnp.concatenate([_head_to_half(qr_ref, g * HPG + hl, g) for hl in range(HPG)], axis=0)
        q4 = q4 * (HEAD_DIM ** -0.5 * LOG2E)
        selg = sel if g == 0 else pltpu.roll(sel, NBLK_PAD, 1)
        sbias = jnp.where(lane < NBLK_PAD, jnp.where(selg > 0.5, 0.0, NEG), 0.0)
        qst.append(q4.astype(BF16))
        qaug.append(jnp.concatenate([q4, heads(sbias)], axis=1).astype(BF16))

    def fold(carry, s, v):
        m, l, acc = carry
        m_new = jnp.maximum(m, jnp.max(s, axis=-1, keepdims=True))
        p = jnp.exp2(s - m_new)
        alpha = jnp.exp2(m - m_new)
        return (m_new, alpha * l + jnp.sum(p, axis=-1, keepdims=True),
                alpha * acc + jnp.dot(p.astype(BF16), v, preferred_element_type=F32))

    def sel_step(kv0, carries, bias):
        kaug = jnp.concatenate([slc_ref[pl.ds(kv0, tk), :KV_W], oh_ref[pl.ds(kv0, tk), :]], axis=1)
        v = slc_ref[pl.ds(kv0, tk), KV_W:]
        out = []
        for g in range(NSA_KV):
            s = lax.dot_general(qaug[g], kaug, (((1,), (1,)), ((), ())), preferred_element_type=F32)
            out.append(fold(carries[g], s if bias is None else s + bias, v))
        return tuple(out)

    init = (jnp.full((nrow, 1), NEG, F32), jnp.zeros((nrow, 1), F32), jnp.zeros((nrow, LANES), F32))
    carries = lax.fori_loop(0, n_full, lambda it, c: sel_step(pl.multiple_of(it * tk, tk), c, None),
                            (init,) * NSA_KV)
    carries = sel_step(kv_diag, carries, cbias)
    kw = win_ref[pl.ds(w0, lw), :KV_W]
    vw = win_ref[pl.ds(w0, lw), KV_W:]
    for g in range(NSA_KV):
        m, l, acc = carries[g]
        o = acc / l
        _halves_to_heads(os_ref, [o[hl * tq:(hl + 1) * tq] for hl in range(HPG)], g)
        s = lax.dot_general(qst[g], kw, (((1,), (1,)), ((), ())), preferred_element_type=F32) + wbias
        e = jnp.exp2(s - jnp.max(s, axis=-1, keepdims=True))
        o = jnp.dot(e.astype(BF16), vw, preferred_element_type=F32) / jnp.sum(e, axis=-1, keepdims=True)
        _halves_to_heads(ow_ref, [o[hl * tq:(hl + 1) * tq] for hl in range(HPG)], g)
    _combine(y_ref, gn_ref, ge_ref, oc_ref[...], os_ref[...], ow_ref[...])


def _nsa_prompt(qr3, slc3, win3, sel3, oc3, gn3, tq, tk):
    b, t, _ = qr3.shape
    lw = min(WINDOW + tq, t)
    qblk = lambda w: pl.BlockSpec((None, tq, w), lambda i, j: (i, j, 0))
    full = pl.BlockSpec((None, t, KV_ROW), lambda i, j: (i, 0, 0))
    onehot = jnp.asarray(np.arange(t)[:, None] // L_SEL == np.arange(LANES)[None, :], BF16)
    return pl.pallas_call(
        functools.partial(_nsa_prompt_body, tq=tq, tk=tk, lw=lw),
        grid=(b, t // tq),
        in_specs=[qblk(NSA_W), full, full, pl.BlockSpec((t, LANES), lambda i, j: (0, 0)),
                  qblk(LANES), qblk(NSA_W), qblk(GN_PAD),
                  pl.BlockSpec((GN_PAD, 3 * NSA_W), lambda i, j: (0, 0))],
        out_specs=qblk(NSA_W),
        out_shape=jax.ShapeDtypeStruct((b, t, NSA_W), F32),
        scratch_shapes=[pltpu.VMEM((tq, NSA_W), F32), pltpu.VMEM((tq, NSA_W), F32)],
        compiler_params=_cparams(("parallel", "arbitrary")),
        name="nsa_attn_prompt",
    )(qr3, slc3, win3, onehot, sel3, oc3, gn3, _gate_expand())


def _nsa_sample_body(*refs, n_pages, past_len):
    page_refs = refs[1:1 + n_pages]
    (qr_ref, snew_ref, wold_ref, wnew_ref, sel_ref, oc_ref, gn_ref, ge_ref, y_ref, os_ref, ow_ref) = refs[1 + n_pages:]
    tq = qr_ref.shape[0]
    page = page_refs[0].shape[-1]
    wb = wold_ref.shape[-1]
    sel = sel_ref[...]
    qpos = past_len + lax.broadcasted_iota(jnp.int32, (tq, 1), 0)

    def head_q(h):
        c = qr_ref[:, (h // 2) * LANES:(h // 2 + 1) * LANES]
        return (pltpu.roll(c, HEAD_DIM, 1) if h % 2 else c)[:, :HEAD_DIM]

    def attend(q, cached, fresh):
        scores = ([jnp.where(allow, _dot(q, kt), NEG) for kt, _, allow in cached]
                  + [jnp.where(allow, _dot_nt(q, k), NEG) for k, _, allow in fresh])
        masks = [p[2] for p in cached + fresh]
        m = functools.reduce(jnp.maximum, [jnp.max(s, axis=-1, keepdims=True) for s in scores])
        es = [jnp.where(allow, jnp.exp(s - m), 0.0) for s, allow in zip(scores, masks)]
        l = sum(jnp.sum(e, axis=-1, keepdims=True) for e in es)
        acc = sum([_dot_nt(e, vt) for e, (_, vt, _) in zip(es, cached)]
                  + [_dot(e, v) for e, (_, v, _) in zip(es[len(cached):], fresh)])
        return acc / l

    def put_heads(o_ref, o, g):
        for pr in range(HPG // 2):
            pair = jnp.concatenate([o[2 * pr * tq:(2 * pr + 1) * tq], o[(2 * pr + 1) * tq:(2 * pr + 2) * tq]], axis=1)
            c = (g * HPG) // 2 + pr
            o_ref[:, c * LANES:(c + 1) * LANES] = pair

    pad = lambda ref: jnp.concatenate([ref[...], jnp.zeros((page - tq, KV_ROW), F32)], axis=0)
    snew, wnew = pad(snew_ref), pad(wnew_ref)
    kpos_new = past_len + lax.broadcasted_iota(jnp.int32, (tq, page), 1)
    causal_new = kpos_new <= qpos
    kpos_old = past_len - wb + lax.broadcasted_iota(jnp.int32, (tq, wb), 1)
    old_ok = _per_head((kpos_old > qpos - WINDOW) & (kpos_old >= 0))
    new_ok = _per_head(causal_new & (kpos_new > qpos - WINDOW))
    for g in range(NSA_KV):
        q = jnp.concatenate([head_q(g * HPG + hl) for hl in range(HPG)], axis=0) * HEAD_DIM ** -0.5
        kcol = slice(g * HEAD_DIM, (g + 1) * HEAD_DIM)
        vcol = slice(KV_W + g * HEAD_DIM, KV_W + (g + 1) * HEAD_DIM)
        cached = [(page_refs[k][0, g], page_refs[k][1, g], _per_head(_sel_mask_tile(sel, g, k * page, page) > 0.5))
                  for k in range(n_pages)]
        fresh = [(snew[:, kcol], snew[:, vcol], _per_head((_sel_mask_tile(sel, g, past_len, page) > 0.5) & causal_new))]
        put_heads(os_ref, attend(q, cached, fresh), g)
        put_heads(ow_ref, attend(q, [(wold_ref[0, g], wold_ref[1, g], old_ok)], [(wnew[:, kcol], wnew[:, vcol], new_ok)]), g)
    _combine(y_ref, gn_ref, ge_ref, oc_ref[...], os_ref[...], ow_ref[...])


def _nsa_sample(qr3, slc_t, page_table, snew3, wold_t, wnew3, sel3, oc3, gn3):
    b, tq, _ = qr3.shape
    n_pages = page_table.shape[1]
    page = slc_t.shape[-1]
    wb = wold_t.shape[-1]
    per = lambda r, w: pl.BlockSpec((None, r, w), lambda i, pt: (i, 0, 0))
    pages = [_page_spec(page, functools.partial(lambda k, i, pt: (pt[i, k], 0, 0, 0, 0), k)) for k in range(n_pages)]
    return pl.pallas_call(
        functools.partial(_nsa_sample_body, n_pages=n_pages, past_len=n_pages * page),
        grid_spec=pltpu.PrefetchScalarGridSpec(
            num_scalar_prefetch=1, grid=(b,),
            in_specs=pages + [per(tq, NSA_W), per(tq, KV_ROW), _page_spec(wb, lambda i, pt: (i, 0, 0, 0, 0)),
                              per(tq, KV_ROW), per(tq, LANES), per(tq, NSA_W), per(tq, GN_PAD),
                              pl.BlockSpec((GN_PAD, 3 * NSA_W), lambda i, pt: (0, 0))],
            out_specs=per(tq, NSA_W),
            scratch_shapes=[pltpu.VMEM((tq, NSA_W), F32), pltpu.VMEM((tq, NSA_W), F32)]),
        out_shape=jax.ShapeDtypeStruct((b, tq, NSA_W), F32),
        compiler_params=_cparams(("arbitrary",)),
        name="nsa_attn_sample",
    )(page_table, *([slc_t] * n_pages), qr3, snew3, wold_t, wnew3, sel3, oc3, gn3, _gate_expand())


def _final_body(x_ref, yraw_ref, bonus_ref, g_ref, yn_ref, gates_ref, lnw_ref, lnb_ref, bd_ref,
                wbr_ref, wbn_ref, wo_ref, gffn_ref, wg_ref, wu_ref, wd_ref, gfin_ref, y_ref):
    bd = bd_ref[...]
    y = yraw_ref[...]
    d = y - _dot2(y, bd) * (1.0 / RWKV_HEAD)
    var = _dot2(d * d, bd) * (1.0 / RWKV_HEAD)
    y_r = (d * lax.rsqrt(var + RWKV_GN_EPS) * lnw_ref[...] + lnb_ref[...] + bonus_ref[...]) * g_ref[...]
    merged = (_sigmoid(gates_ref[:, :D_MODEL]) * _dot(y_r, wbr_ref[...])
              + _sigmoid(gates_ref[:, D_MODEL:]) * _dot(yn_ref[...], wbn_ref[...]))
    x1 = x_ref[...] + _dot(merged, wo_ref[...])
    h2 = x1 * lax.rsqrt(jnp.mean(x1 * x1, axis=-1, keepdims=True) + NORM_EPS) * gffn_ref[...]
    hb = h2.astype(BF16)
    up = _silu(jnp.dot(hb, wg_ref[...], preferred_element_type=F32)) * jnp.dot(hb, wu_ref[...], preferred_element_type=F32)
    x2 = x1 + _dot(up, wd_ref[...])
    y_ref[...] = x2 * lax.rsqrt(jnp.mean(x2 * x2, axis=-1, keepdims=True) + NORM_EPS) * gfin_ref[...]


def _final(x2d, yraw, bonus, g, yn, gates, prm, tm):
    m = x2d.shape[0]
    tm = min(tm, m)
    row = lambda w: pl.BlockSpec((tm, w), lambda i: (i, 0))
    return pl.pallas_call(
        _final_body,
        grid=(m // tm,),
        in_specs=[row(D_MODEL), row(RWKV_W), row(RWKV_W), row(RWKV_W), row(NSA_W), row(2 * D_MODEL),
                  _const_spec((1, RWKV_W)), _const_spec((1, RWKV_W)), _const_spec((RWKV_W, RWKV_W)),
                  _const_spec((RWKV_W, D_MODEL)), _const_spec((NSA_W, D_MODEL)),
                  _const_spec((D_MODEL, D_MODEL)), _const_spec((1, D_MODEL)),
                  _const_spec((D_MODEL, D_FF)), _const_spec((D_MODEL, D_FF)), _const_spec((D_FF, D_MODEL)),
                  _const_spec((1, D_MODEL))],
        out_specs=row(D_MODEL),
        out_shape=jax.ShapeDtypeStruct((m, D_MODEL), F32),
        compiler_params=_cparams(("parallel",)),
        name="merge_ffn",
    )(x2d, yraw, bonus, g, yn, gates, prm["ln_w"], prm["ln_b"], prm["bd"], prm["w_br_rwkv"],
      prm["w_br_nsa"], prm["w_out"], prm["g_ffn"], prm["w_ffn_gate"], prm["w_ffn_up"],
      prm["w_ffn_down"], prm["g_final"])


_S = (RWKV_W, RWKV_W + DECAY_RANK, 2 * RWKV_W + DECAY_RANK, 3 * RWKV_W + DECAY_RANK,
      3 * RWKV_W + DECAY_RANK + AAA_RANK)
_RW_PERM = np.concatenate([np.arange(0, _S[0]), np.arange(_S[1], _S[2]), np.arange(_S[2], _S[3]),
                           np.arange(_S[0], _S[1]), np.arange(_S[3], _S[4]), np.arange(_S[4], RWKV_COLS)])
_RW_INV = np.argsort(_RW_PERM)


def _prep_params(l, g_mix, w_in, rwkv_mu, rwkv_w0, rwkv_w_decay, rwkv_a0, rwkv_w_aaa, rwkv_w_gate,
                 rwkv_k_k, rwkv_k_a, rwkv_r_k, rwkv_ln_w, rwkv_ln_b, w_br_rwkv, w_br_nsa, w_out, g_ffn,
                 w_ffn_gate, w_ffn_up, w_ffn_down, g_final):
    w = w_in[l]
    nsa0 = RWKV_COLS
    kv_end = nsa0 + NSA_W + 6 * KV_W
    zc = lambda n: jnp.zeros((D_MODEL, n), F32)
    w_r = jnp.concatenate([w[:, _RW_PERM], zc(RW_PAD - RWKV_COLS), w[:, nsa0:kv_end],
                           w[:, kv_end:nsa0 + NSA_COLS], zc(GN_PAD - 3 * NSA_HEADS),
                           w[:, nsa0 + NSA_COLS:]], axis=1).astype(BF16)
    mu = jnp.concatenate([rwkv_mu[l][_RW_PERM], jnp.zeros((RW_PAD - RWKV_COLS,), F32)])[None]
    wda = jnp.zeros((LANES, 2 * RWKV_W), F32)
    wda = wda.at[:DECAY_RANK, :RWKV_W].set(rwkv_w_decay[l]).at[DECAY_RANK:, RWKV_W:].set(rwkv_w_aaa[l])
    wg = jnp.zeros((RW_PAD - 3 * RWKV_W - LANES, RWKV_W), F32).at[:GATE_RANK].set(rwkv_w_gate[l])
    hid = np.arange(RWKV_W) // RWKV_HEAD
    bd = jnp.asarray(hid[:, None] == hid[None, :], BF16)
    r1 = lambda a: a.reshape(1, -1)
    return dict(
        g_mix=r1(g_mix[l]), w_r=w_r, mu=mu, wda=wda.astype(BF16), w0=r1(rwkv_w0[l]), a0=r1(rwkv_a0[l]),
        wg=wg.astype(BF16), k_k=r1(rwkv_k_k[l]), k_a=r1(rwkv_k_a[l]), r_k=r1(rwkv_r_k[l]), bd=bd,
        ln_w=r1(rwkv_ln_w[l]), ln_b=r1(rwkv_ln_b[l]), w_br_rwkv=w_br_rwkv[l].astype(BF16),
        w_br_nsa=w_br_nsa[l].astype(BF16), w_out=w_out[l].astype(BF16), g_ffn=r1(g_ffn[l]),
        w_ffn_gate=w_ffn_gate[l].astype(BF16), w_ffn_up=w_ffn_up[l].astype(BF16),
        w_ffn_down=w_ffn_down[l].astype(BF16), g_final=r1(g_final))


def _rope_tables(pos):
    half = ROT_DIM // 2
    inv = ROPE_THETA ** (-jnp.arange(half, dtype=F32) / half)
    ang = pos.astype(F32)[:, None] * inv[None, :]
    cos, sin = jnp.cos(ang), jnp.sin(ang)
    n = pos.shape[0]
    one, zero = jnp.ones((n, HEAD_DIM - ROT_DIM), F32), jnp.zeros((n, HEAD_DIM - ROT_DIM), F32)
    z8 = jnp.zeros((n, half), F32)
    c = jnp.concatenate([cos, cos, one], 1)
    s1 = jnp.concatenate([-sin, z8, zero], 1)
    s2 = jnp.concatenate([z8, sin, zero], 1)
    return tuple(jnp.concatenate([a, a], 1) for a in (c, s1, s2))


def _group(x, pos, shift0, s0, prm, cw, past, tm, pre_blk, tq):
    b, t, _ = x.shape
    m = b * t
    x2d = x.reshape(m, D_MODEL)
    p_rw, q, q_r, cmp_rows, slc_rows, win_rows, gn, gates, slc_bf, win_bf = _in_proj(
        x2d, prm["g_mix"], prm["w_r"], pos, tm)

    shift_r = jnp.concatenate([shift0[:, _RW_PERM], jnp.zeros((b, RW_PAD - RWKV_COLS), F32)], 1)[:, None, :]
    *scan_in, v, bonus, g = _rwkv_pre(p_rw.reshape(b, t, RW_PAD), shift_r, prm, *pre_blk)
    y_raw, st = _rwkv_chunk_scan(scan_in, v, _state_to_blockdiag(s0), t)
    s_new = _blockdiag_to_state(st)
    shift_new = p_rw.reshape(b, t, RW_PAD)[:, -1, :RWKV_COLS][:, _RW_INV]

    r3 = lambda a: a.reshape(b, t, a.shape[-1])
    if past is None:
        kcv = _compress_rows(r3(cmp_rows), cw)
        n_c = t // D_CMP - 1
        oc, sel = _cmp_attn(r3(q), kcv, tq, 0, n_c)
        y_n = _nsa_prompt(r3(q_r), r3(slc_bf), r3(win_bf), sel, oc, r3(gn), min(2 * tq, t), min(512, t))
        win_state = r3(win_rows)[:, t - min(WINDOW, t):]
    else:
        cmp_cache, slc_cache, win_cache, page_table = past
        rows_last = lambda c: jnp.moveaxis(c, 1, -1)
        past_len = page_table.shape[1] * cmp_cache.shape[1]
        kcv = _compress_paged(rows_last(cmp_cache), page_table, cw)
        n_c = (past_len + t) // D_CMP - 1
        oc, sel = _cmp_attn(r3(q), kcv, tq, past_len, n_c)
        y_n = _nsa_sample(r3(q_r), rows_last(slc_cache), page_table, r3(slc_rows), rows_last(win_cache),
                          r3(win_rows), sel, oc, r3(gn))
        win_all = jnp.concatenate([win_cache, r3(win_rows).reshape(b, t, 2, NSA_KV, HEAD_DIM)], axis=1)
        win_state = win_all[:, win_all.shape[1] - min(WINDOW, win_all.shape[1]):]

    y = _final(x2d, y_raw, bonus, g, y_n.reshape(m, NSA_W), gates, prm, tm)
    kv6 = lambda a: a.reshape(b, -1, 2, NSA_KV, HEAD_DIM)
    return (y.reshape(b, t, D_MODEL), kv6(r3(cmp_rows)), kv6(r3(slc_rows)), kv6(win_state),
            s_new, shift_new)


def kernel(x_prompt, x_sample, cache_cmp_kv, cache_slc_kv, cache_win_kv, state_rwkv, state_rwkv_shift, page_table, g_mix, w_in, rwkv_mu, rwkv_w0, rwkv_w_decay, rwkv_a0, rwkv_w_aaa, rwkv_w_gate, rwkv_k_k, rwkv_k_a, rwkv_r_k, rwkv_ln_w, rwkv_ln_b, nsa_pe_cmp, nsa_w_cmp1, nsa_w_cmp2, w_br_rwkv, w_br_nsa, w_out, g_ffn, w_ffn_gate, w_ffn_up, w_ffn_down, g_final):
    depth = w_in.shape[0]
    assert depth == 1, "single-layer trunk"
    b, t, _ = x_prompt.shape
    bs, ts, _ = x_sample.shape
    n_pool, page = cache_cmp_kv.shape[1:3]
    past_len = page_table.shape[1] * page
    assert (past_len + ts) // D_CMP == past_len // D_CMP and past_len % D_CMP == 0
    prm = _prep_params(0, g_mix, w_in, rwkv_mu, rwkv_w0, rwkv_w_decay, rwkv_a0, rwkv_w_aaa, rwkv_w_gate,
                       rwkv_k_k, rwkv_k_a, rwkv_r_k, rwkv_ln_w, rwkv_ln_b, w_br_rwkv, w_br_nsa, w_out,
                       g_ffn, w_ffn_gate, w_ffn_up, w_ffn_down, g_final)
    cw = _compress_weights(nsa_pe_cmp[0], nsa_w_cmp1[0], nsa_w_cmp2[0])

    outs_p = _group(x_prompt, jnp.arange(t), jnp.zeros((b, RWKV_COLS), F32),
                    jnp.zeros((b, RWKV_HEADS, RWKV_HEAD, RWKV_HEAD), F32), prm, cw, None,
                    tm=256, pre_blk=(1, min(512, t)), tq=min(128, t))
    outs_s = _group(x_sample, past_len + jnp.arange(ts), state_rwkv_shift[0], state_rwkv[0], prm, cw,
                    (cache_cmp_kv[0], cache_slc_kv[0], cache_win_kv[0], page_table),
                    tm=256, pre_blk=(min(64, bs), ts), tq=ts)
    return (outs_p[0], outs_s[0]) + tuple(o[None] for o in outs_p[1:]) + tuple(o[None] for o in outs_s[1:])
```

```python
import functools

import jax
import jax.numpy as jnp
import numpy as np
from jax import lax
from jax.experimental import pallas as pl
from jax.experimental.pallas import tpu as pltpu

F32 = jnp.float32
BF16 = jnp.bfloat16

D_MODEL = 1024
RWKV_HEADS = 8
RWKV_HEAD = 64
RWKV_W = RWKV_HEADS * RWKV_HEAD
DECAY_RANK = 64
AAA_RANK = 64
GATE_RANK = 160
RWKV_GN_EPS = 64e-5
NSA_HEADS = 8
NSA_KV = 2
HEAD_DIM = 64
HPG = NSA_HEADS // NSA_KV
NSA_W = NSA_HEADS * HEAD_DIM
KV_W = NSA_KV * HEAD_DIM
L_CMP = 32
D_CMP = 16
CMP_HID = 128
L_SEL = 64
N_SEL = 16
WINDOW = 512
ROT_DIM = HEAD_DIM // 4
ROPE_THETA = 500000.0
D_FF = -(-8 * D_MODEL // (3 * 256)) * 256
NORM_EPS = 1e-6
NEG = -1e30
FORCE = 1e6
RWKV_COLS = 3 * RWKV_W + DECAY_RANK + AAA_RANK + GATE_RANK
NSA_COLS = NSA_W + 6 * KV_W + 3 * NSA_HEADS

LANES = 128
RW_PAD = 1920
GN_PAD = LANES
KV_ROW = 2 * KV_W
NBLK_PAD = 64
VMEM_LIMIT = 56 * 1024 * 1024


def _cparams(sem):
    return pltpu.CompilerParams(dimension_semantics=sem, vmem_limit_bytes=VMEM_LIMIT)


def _const_spec(shape):
    nd = len(shape)
    return pl.BlockSpec(shape, lambda *_: (0,) * nd, pipeline_mode=pl.Buffered(1))


def _dot(a, b):
    return jnp.dot(a.astype(BF16), b.astype(BF16), preferred_element_type=F32)


def _dot_nt(a, b):
    return lax.dot_general(a.astype(BF16), b.astype(BF16), (((1,), (1,)), ((), ())),
                           preferred_element_type=F32)


def _dot2(a, b):
    hi = a.astype(BF16)
    lo = (a - hi.astype(F32)).astype(BF16)
    return (jnp.dot(hi, b, preferred_element_type=F32)
            + jnp.dot(lo, b, preferred_element_type=F32))


def _sigmoid(x):
    return 1.0 / (1.0 + jnp.exp(-x))


def _silu(x):
    return x * _sigmoid(x)


def _rope128(x, cos, s1, s2):
    half = ROT_DIM // 2
    return x * cos + pltpu.roll(x, LANES - half, 1) * s1 + pltpu.roll(x, half, 1) * s2


_C_Q = RW_PAD
_C_CMP = _C_Q + NSA_W
_C_SLC = _C_CMP + KV_ROW
_C_WIN = _C_SLC + KV_ROW
_C_GN = _C_WIN + KV_ROW
_C_GATE = _C_GN + GN_PAD
_C_END = _C_GATE + 2 * D_MODEL


def _proj_body(x_ref, g_ref, w_ref, cos_ref, s1_ref, s2_ref,
               prw_ref, q_ref, qr_ref, cmp_ref, slc_ref, win_ref, gn_ref, gates_ref, slcb_ref, winb_ref,
               *, rows_last):
    x = x_ref[...]
    ms = jnp.mean(x * x, axis=-1, keepdims=True)
    h = (x * lax.rsqrt(ms + NORM_EPS) * g_ref[...]).astype(BF16)

    def mm(a, b):
        return jnp.dot(h, w_ref[:, a:b], preferred_element_type=F32)

    def put_rows(ref, kv):
        ref[...] = kv.T.reshape(ref.shape) if rows_last else kv

    cos, s1, s2 = cos_ref[...], s1_ref[...], s2_ref[...]
    prw_ref[...] = mm(0, _C_Q)
    q = mm(_C_Q, _C_CMP)
    q_ref[...] = q
    for c in range(NSA_W // LANES):
        sl = slice(c * LANES, (c + 1) * LANES)
        qr_ref[:, sl] = _rope128(q[:, sl], cos, s1, s2)
    put_rows(cmp_ref, mm(_C_CMP, _C_SLC))
    for lo, hi, f_ref, h_ref in ((_C_SLC, _C_WIN, slc_ref, slcb_ref), (_C_WIN, _C_GN, win_ref, winb_ref)):
        kv = mm(lo, hi)
        kv = jnp.concatenate([_rope128(kv[:, :KV_W], cos, s1, s2), kv[:, KV_W:]], axis=1)
        put_rows(f_ref, kv)
        h_ref[...] = kv.astype(BF16)
    gn_ref[...] = mm(_C_GN, _C_GATE)
    gates_ref[...] = mm(_C_GATE, _C_END)


def _in_proj(x2d, g_mix, w_r, pos, tm):
    m = x2d.shape[0]
    t = pos.shape[0]
    tm = min(tm, m)
    period = max(t, tm) // tm
    tabs = _rope_tables(jnp.tile(pos, max(1, tm // t)))
    row = lambda w: pl.BlockSpec((tm, w), lambda i: (i, 0))
    tab = pl.BlockSpec((tm, LANES), lambda i: (i % period, 0))
    f32 = lambda w: (row(w), jax.ShapeDtypeStruct((m, w), F32))
    rows_last = t % tm == 0
    if rows_last:
        kv = (pl.BlockSpec((None, 2, NSA_KV, HEAD_DIM, tm), lambda i: (i // period, 0, 0, 0, i % period)),
              jax.ShapeDtypeStruct((m // t, 2, NSA_KV, HEAD_DIM, t), F32))
    else:
        kv = f32(KV_ROW)
    bf = (row(KV_ROW), jax.ShapeDtypeStruct((m, KV_ROW), BF16))
    outs = [f32(RW_PAD), f32(NSA_W), f32(NSA_W), kv, kv, kv, f32(GN_PAD), f32(2 * D_MODEL), bf, bf]
    return pl.pallas_call(
        functools.partial(_proj_body, rows_last=rows_last),
        grid=(m // tm,),
        in_specs=[row(D_MODEL), _const_spec((1, D_MODEL)), _const_spec((D_MODEL, _C_END)),
                  tab, tab, tab],
        out_specs=[o[0] for o in outs],
        out_shape=[o[1] for o in outs],
        compiler_params=_cparams(("parallel",)),
        name="in_proj",
    )(x2d, g_mix, w_r, *tabs)


def _rwkv_pre_body(p_ref, sh_ref, mu_ref, wda_ref, w0_ref, a0_ref, wg_ref, kk_ref, ka_ref,
                   rk_ref, bd_ref, r_ref, lw_ref, k_ref, a_ref, b_ref, v_ref, bonus_ref,
                   g_ref, carry_ref):
    j = pl.program_id(1)
    bb, tt, c = p_ref.shape
    n = bb * tt
    x3 = p_ref[...]
    x = x3.reshape(n, c)
    prev = jnp.where(j == 0, sh_ref[...], carry_ref[...])
    carry_ref[...] = x3[:, tt - 1:tt, :]
    prev_rows = jnp.broadcast_to(prev, (bb, tt, c)).reshape(n, c)
    row = lax.broadcasted_iota(jnp.int32, (n, c), 0)
    shifted = jnp.where(row % tt == 0, prev_rows, pltpu.roll(x, 1, 0))
    xs = x + (shifted - x) * mu_ref[...]
    r = xs[:, 0:RWKV_W]
    k = xs[:, RWKV_W:2 * RWKV_W]
    v = xs[:, 2 * RWKV_W:3 * RWKV_W]
    wa = xs[:, 3 * RWKV_W:3 * RWKV_W + LANES]
    gl = xs[:, 3 * RWKV_W + LANES:RW_PAD]
    lane = lax.broadcasted_iota(jnp.int32, wa.shape, 1)
    z = _dot(jnp.where(lane < DECAY_RANK, jnp.tanh(wa), wa), wda_ref[...])
    u = -(w0_ref[...] + z[:, :RWKV_W])
    softplus = jnp.maximum(u, 0.0) + jnp.log1p(jnp.exp(-jnp.abs(u)))
    log_decay = -jnp.exp(-softplus - 0.5)
    a = _sigmoid(a0_ref[...] + z[:, RWKV_W:])
    g_ref[...] = _dot(_sigmoid(gl), wg_ref[...])
    bd = bd_ref[...]
    kk = k * kk_ref[...]
    kk = kk * lax.rsqrt(jnp.maximum(_dot2(kk * kk, bd), 1e-24))
    k_h = k * (1.0 + (a - 1.0) * ka_ref[...])
    bonus_ref[...] = _dot2(r * k_h * rk_ref[...], bd) * v
    v_ref[...] = v
    r_ref[...] = r
    lw_ref[...] = log_decay
    k_ref[...] = k_h
    a_ref[...] = -kk
    b_ref[...] = kk * a


def _rwkv_pre(p3, shift0, prm, bb, tt):
    b, t, _ = p3.shape
    m = b * t
    n = bb * tt
    nt = t // tt
    rowm = pl.BlockSpec((n, RWKV_W), lambda i, j: (i * nt + j, 0))
    outs = pl.pallas_call(
        _rwkv_pre_body,
        grid=(b // bb, nt),
        in_specs=[pl.BlockSpec((bb, tt, RW_PAD), lambda i, j: (i, j, 0)),
                  pl.BlockSpec((bb, 1, RW_PAD), lambda i, j: (i, 0, 0)),
                  _const_spec((1, RW_PAD)), _const_spec((LANES, 2 * RWKV_W)),
                  _const_spec((1, RWKV_W)), _const_spec((1, RWKV_W)),
                  _const_spec((RW_PAD - 3 * RWKV_W - LANES, RWKV_W)),
                  _const_spec((1, RWKV_W)), _const_spec((1, RWKV_W)), _const_spec((1, RWKV_W)),
                  _const_spec((RWKV_W, RWKV_W))],
        out_specs=[rowm] * 8,
        out_shape=[jax.ShapeDtypeStruct((m, RWKV_W), F32)] * 8,
        scratch_shapes=[pltpu.VMEM((bb, 1, RW_PAD), F32)],
        compiler_params=_cparams(("parallel", "arbitrary")),
        name="rwkv_pre",
    )(p3, shift0, prm["mu"], prm["wda"], prm["w0"], prm["a0"], prm["wg"], prm["k_k"],
      prm["k_a"], prm["r_k"], prm["bd"])
    return outs


RWKV_CHUNK = 64


def _split3(x):
    x1 = x.astype(BF16)
    r1 = x - x1.astype(F32)
    x2 = r1.astype(BF16)
    return x1, x2, (r1 - x2.astype(F32)).astype(BF16)


def _chunk_scan_body(r_ref, lw_ref, k_ref, a_ref, b_ref, v_ref, s0_ref, tri_ref, y_ref, s_ref):
    nbat, c = r_ref.shape[0], r_ref.shape[1]
    n = 2 * c
    first = pl.program_id(1) == 0
    left = lax.broadcasted_iota(jnp.int32, (c, LANES), 1) < RWKV_HEAD
    stack = lambda x: jnp.concatenate([jnp.where(left, x, 0.0), jnp.where(left, 0.0, x)], axis=0)
    ri = lax.broadcasted_iota(jnp.int32, (n, n), 0)
    ci = lax.broadcasted_iota(jnp.int32, (n, n), 1)
    strict = ((ri < c) == (ci < c)) & (ci < ri)
    eye = jnp.where(ri == ci, 1.0, 0.0)
    ri2 = lax.broadcasted_iota(jnp.int32, (n, 2 * n), 0)
    ci2 = lax.broadcasted_iota(jnp.int32, (n, 2 * n), 1) % n
    incl2 = ((ri2 < c) == (ci2 < c)) & (ci2 <= ri2)
    tri = tri_ref[...]
    chains = [(i, p) for i in range(nbat) for p in range(RWKV_HEADS // 2)]
    pre = []
    for i, p in chains:
        sl = slice(p * LANES, (p + 1) * LANES)
        lw = lw_ref[i, :, sl]
        cs = sum(jnp.dot(tri, part, preferred_element_type=F32) for part in _split3(lw))
        c_end = cs[c - 1:c, :]
        e_neg = jnp.exp(-cs)
        e_hat = jnp.exp(c_end - cs)
        at = stack(a_ref[i, :, sl] * jnp.exp(cs - lw))
        rt = stack(r_ref[i, :, sl] * jnp.exp(cs))
        bt = stack(b_ref[i, :, sl] * e_neg)
        kt = stack(k_ref[i, :, sl] * e_neg)
        bk_hat = jnp.concatenate([stack(b_ref[i, :, sl] * e_hat), stack(k_ref[i, :, sl] * e_hat)], axis=0)
        g = _dot_nt(jnp.concatenate([at, rt], axis=0), jnp.concatenate([bt, kt], axis=0))
        pre.append(dict(sl=sl, e_end=jnp.exp(c_end), at=at, rt=rt, bk_hat=bk_hat.astype(BF16),
                        vs=stack(v_ref[i, :, sl]), low=jnp.where(strict, g[:n, :n], 0.0),
                        g_ak=jnp.where(strict, g[:n, n:], 0.0),
                        g_r=jnp.where(incl2, g[n:, :], 0.0)))
    tinv = [eye + q["low"] for q in pre]
    lp = [q["low"] for q in pre]
    for _ in range(c.bit_length() - 2):
        lp = [_dot(x, x) for x in lp]
        tinv = [t + _dot(t, x) for t, x in zip(tinv, lp)]
    state = [jnp.where(first, s0_ref[i, p], s_ref[i, p]) for i, p in chains]
    rhs = [_dot_nt(q["at"], s) + _dot(q["g_ak"], q["vs"]) for q, s in zip(pre, state)]
    uv = [jnp.concatenate([_dot(t, x), q["vs"]], axis=0) for t, x, q in zip(tinv, rhs, pre)]
    for (i, p), q, s, w in zip(chains, pre, state, uv):
        ys = _dot_nt(q["rt"], s) + _dot(q["g_r"], w)
        y_ref[i, :, q["sl"]] = ys[:c] + ys[c:]
        s_ref[i, p] = s * q["e_end"] + lax.dot_general(
            w.astype(BF16), q["bk_hat"], (((0,), (0,)), ((), ())), preferred_element_type=F32)


SCAN_SEQS = 8


def _rwkv_chunk_scan(rows, v, s0pair, t):
    m = v.shape[0]
    b = s0pair.shape[0]
    c = min(RWKV_CHUNK, t)
    nbat = next(d for d in range(SCAN_SEQS, 0, -1) if b % d == 0)
    as3 = lambda x: x.reshape(b, t, RWKV_W)
    rowm = pl.BlockSpec((nbat, c, RWKV_W), lambda i, j: (i, j, 0))
    sspec = pl.BlockSpec((nbat, RWKV_HEADS // 2, LANES, LANES), lambda i, j: (i, 0, 0, 0))
    tri = jnp.asarray(np.tril(np.ones((c, c), np.float32)), BF16)
    y, st = pl.pallas_call(
        _chunk_scan_body,
        grid=(b // nbat, t // c),
        in_specs=[rowm] * 6 + [sspec, pl.BlockSpec((c, c), lambda i, j: (0, 0))],
        out_specs=[rowm, sspec],
        out_shape=[jax.ShapeDtypeStruct((b, t, RWKV_W), F32), jax.ShapeDtypeStruct(s0pair.shape, F32)],
        compiler_params=_cparams(("parallel", "arbitrary")),
        name="rwkv_chunk_scan",
    )(*[as3(x) for x in rows], as3(v), s0pair, tri)
    return y.reshape(m, RWKV_W), st


def _state_to_blockdiag(s):
    b = s.shape[0]
    sp = s.reshape(b, RWKV_HEADS // 2, 2, RWKV_HEAD, RWKV_HEAD)
    z = jnp.zeros_like(sp[:, :, 0])
    top = jnp.concatenate([sp[:, :, 0], z], axis=-1)
    bot = jnp.concatenate([z, sp[:, :, 1]], axis=-1)
    return jnp.concatenate([top, bot], axis=-2)


def _blockdiag_to_state(sp):
    b = sp.shape[0]
    h0 = sp[:, :, :RWKV_HEAD, :RWKV_HEAD]
    h1 = sp[:, :, RWKV_HEAD:, RWKV_HEAD:]
    return jnp.stack([h0, h1], axis=2).reshape(b, RWKV_HEADS, RWKV_HEAD, RWKV_HEAD)


def _compress_body(*refs, n_src):
    pe_ref, w1_ref, w2_ref, o_ref, xs_ref = refs[-5:]
    pages = refs[-5 - n_src:-5]
    page = pages[0].shape[-1]
    for k, pg in enumerate(pages):
        for c in range(2):
            xs_ref[c, k * page:(k + 1) * page, :] = pg[c].reshape(KV_W, page).T
    per = n_src * page // D_CMP
    nseq, ncs = o_ref.shape[0], o_ref.shape[1]
    for c in range(2):
        rows = lambda l: xs_ref[c, pl.ds(l, per, stride=D_CMP), :]
        for j in range(D_CMP // 2):
            xl = jnp.concatenate([rows(2 * j), rows(2 * j + 1)], axis=1)
            f = _dot(xl + pe_ref[c, j:j + 1, :], w1_ref[c, j])
            s = _dot(xl + pe_ref[c, D_CMP // 2 + j:D_CMP // 2 + j + 1, :], w1_ref[c, D_CMP // 2 + j])
            hf = f if j == 0 else hf + f
            hs = s if j == 0 else hs + s
        nc = hf.shape[0]
        out = _dot(_silu(hf + pltpu.roll(hs, nc - 1, 0)), w2_ref[c])
        for q in range(nseq):
            o_ref[q, :, c * KV_W:(c + 1) * KV_W] = out[q * ncs:(q + 1) * ncs]


def _compress_weights(pe, w1, w2):
    eye = jnp.eye(NSA_KV, dtype=F32)
    w1b = jnp.einsum("cldh,gy->clgdyh", w1, eye).reshape(2, L_CMP, KV_W, NSA_KV * CMP_HID)
    w2b = jnp.einsum("chd,gy->cghyd", w2, eye).reshape(2, NSA_KV * CMP_HID, KV_W)
    peb = jnp.broadcast_to(pe[:, :, None, :], (2, L_CMP, NSA_KV, HEAD_DIM)).reshape(2, L_CMP, KV_W)
    return (peb.reshape(2, L_CMP // 2, 2 * KV_W),
            w1b.reshape(2, L_CMP // 2, 2 * KV_W, NSA_KV * CMP_HID).astype(BF16), w2b.astype(BF16))


CMP_SEQS = 2


def _page_spec(page, index_map):
    return pl.BlockSpec((None, 2, NSA_KV, HEAD_DIM, page), index_map)


def _compress_call(rows_t, page_specs, nb, nseq, nc, cw, prefetch):
    hid = NSA_KV * CMP_HID
    n_src = len(page_specs)
    page = page_specs[0].block_shape[-1]
    wspecs = [pl.BlockSpec((2, L_CMP // 2, 2 * KV_W), lambda *_: (0, 0, 0)),
              pl.BlockSpec((2, L_CMP // 2, 2 * KV_W, hid), lambda *_: (0, 0, 0, 0)),
              pl.BlockSpec((2, hid, KV_W), lambda *_: (0, 0, 0))]
    return pl.pallas_call(
        functools.partial(_compress_body, n_src=n_src),
        grid_spec=pltpu.PrefetchScalarGridSpec(
            num_scalar_prefetch=len(prefetch), grid=(nb // nseq,), in_specs=page_specs + wspecs,
            out_specs=pl.BlockSpec((nseq, nc, KV_ROW), lambda b, *_: (b, 0, 0)),
            scratch_shapes=[pltpu.VMEM((2, n_src * page, KV_W), F32)]),
        out_shape=jax.ShapeDtypeStruct((nb, nc, KV_ROW), F32),
        compiler_params=_cparams(("parallel",)),
        name="nsa_compress",
    )(*prefetch, *([rows_t] * n_src), *cw)


def _compress_seq(rows_t, cw):
    b, t = rows_t.shape[0], rows_t.shape[-1]
    specs = [_page_spec(LANES, functools.partial(lambda k, i: (i, 0, 0, 0, k), k)) for k in range(t // LANES)]
    return _compress_call(rows_t, specs, b, 1, t // D_CMP, cw, ())


def _compress_paged(cache_t, page_table, cw):
    b, n_pages = page_table.shape
    page = cache_t.shape[-1]
    nseq = CMP_SEQS if b % CMP_SEQS == 0 else 1
    specs = [_page_spec(page, functools.partial(lambda q, k, i, pt: (pt[i * nseq + q, k], 0, 0, 0, 0), q, k))
             for q in range(nseq) for k in range(n_pages)]
    return _compress_call(cache_t, specs, b, nseq, n_pages * page // D_CMP, cw, (page_table,))


def _head_to_half(q_ref, h, g):
    c = q_ref[:, (h // 2) * LANES:(h // 2 + 1) * LANES]
    if h % 2 != g:
        c = pltpu.roll(c, HEAD_DIM, 1)
    lane = lax.broadcasted_iota(jnp.int32, c.shape, 1)
    return jnp.where((lane >= g * HEAD_DIM) & (lane < (g + 1) * HEAD_DIM), c, 0.0)


def _halves_to_heads(o_ref, outs, g):
    for pr in range(HPG // 2):
        tiles = []
        for hl in (2 * pr, 2 * pr + 1):
            h = g * HPG + hl
            o = outs[hl]
            if h % 2 != g:
                o = pltpu.roll(o, HEAD_DIM, 1)
            tiles.append(o)
        lane = lax.broadcasted_iota(jnp.int32, tiles[0].shape, 1)
        c = (g * HPG) // 2 + pr
        o_ref[:, c * LANES:(c + 1) * LANES] = jnp.where(lane < HEAD_DIM, tiles[0], tiles[1])


def _cmp_attn_body(q_ref, kcv_ref, ov_ref, oc_ref, sel_ref, *, tq, pos0, n_c):
    qi = pl.program_id(1)
    nb, ncp = kcv_ref.shape[0], kcv_ref.shape[1]
    n = nb * tq
    row = lax.broadcasted_iota(jnp.int32, (tq, ncp), 0)
    col = lax.broadcasted_iota(jnp.int32, (tq, ncp), 1)
    mask = _per_head((col * D_CMP + (L_CMP - 1) <= pos0 + qi * tq + row) & (col < n_c))
    imps = []
    for sb in range(nb):
        kc = kcv_ref[sb, :, :KV_W]
        vc = kcv_ref[sb, :, KV_W:]
        imp = None
        for g in range(NSA_KV):
            q4 = jnp.concatenate([_head_to_half(q_ref.at[sb], g * HPG + hl, g) for hl in range(HPG)], axis=0)
            s = jnp.where(mask, _dot_nt(q4, kc) * HEAD_DIM ** -0.5, NEG)
            e = jnp.exp(s - jnp.max(s, axis=-1, keepdims=True))
            p = jnp.where(mask, e / jnp.sum(e, axis=-1, keepdims=True), 0.0)
            o = _dot(p, vc)
            _halves_to_heads(oc_ref.at[sb], [o[hl * tq:(hl + 1) * tq] for hl in range(HPG)], g)
            psum = sum(p[hl * tq:(hl + 1) * tq] for hl in range(HPG))
            ig = _dot2(psum, ov_ref[g])
            imp = ig if imp is None else imp + ig
        imps.append(imp)
    imp_t = jnp.concatenate(imps, axis=0).T
    jb = lax.broadcasted_iota(jnp.int32, (LANES, n), 0) % NBLK_PAD
    cur = (pos0 + qi * tq + lax.broadcasted_iota(jnp.int32, (LANES, n), 1) % tq) // L_SEL
    forced = (jb == 0) | (jb == cur) | (jb == cur - 1)
    score = jnp.where(forced, FORCE, jnp.where(jb <= cur, imp_t, -1.0))
    sub = lax.broadcasted_iota(jnp.int32, (8, n), 0)
    ranks = []
    for g in range(NSA_KV):
        blocks = [score[g * NBLK_PAD + 8 * k:g * NBLK_PAD + 8 * k + 8, :] for k in range(NBLK_PAD // 8)]
        cnt = [jnp.zeros((8, n), F32) for _ in blocks]
        for jp in range(NBLK_PAD):
            other = jnp.broadcast_to(score[g * NBLK_PAD + jp:g * NBLK_PAD + jp + 1, :], (8, n))
            for k, blk in enumerate(blocks):
                if 8 * k > jp:
                    ahead = other >= blk
                elif 8 * k + 7 <= jp:
                    ahead = other > blk
                else:
                    ahead = (other > blk) | ((other == blk) & (sub > jp - 8 * k))
                cnt[k] = cnt[k] + jnp.where(ahead, 1.0, 0.0)
        ranks += cnt
    rank = jnp.concatenate(ranks, axis=0)
    sel = jnp.where((rank < N_SEL) & (jb <= cur), 1.0, 0.0).T
    for sb in range(nb):
        sel_ref[sb] = sel[sb * tq:(sb + 1) * tq]


def _overlap(ncp, n_c):
    ci = np.arange(ncp)[:, None]
    sj = np.arange(NBLK_PAD)[None, :]
    ov = ((ci * D_CMP < (sj + 1) * L_SEL) & (ci * D_CMP + L_CMP > sj * L_SEL) & (ci < n_c)).astype(np.float32)
    z = np.zeros_like(ov)
    return jnp.asarray(np.stack([np.concatenate([ov, z], 1), np.concatenate([z, ov], 1)]), BF16)


def _cmp_attn(q3, kcv, tq, pos0, n_c):
    b, t, _ = q3.shape
    ncp = kcv.shape[1]
    nb = LANES // tq
    blk = lambda r, w: pl.BlockSpec((nb, r, w), lambda i, j: (i, j, 0))
    return pl.pallas_call(
        functools.partial(_cmp_attn_body, tq=tq, pos0=pos0, n_c=n_c),
        grid=(b // nb, t // tq),
        in_specs=[blk(tq, NSA_W), pl.BlockSpec((nb, ncp, KV_ROW), lambda i, j: (i, 0, 0)),
                  pl.BlockSpec((NSA_KV, ncp, LANES), lambda i, j: (0, 0, 0))],
        out_specs=[blk(tq, NSA_W), blk(tq, LANES)],
        out_shape=[jax.ShapeDtypeStruct((b, t, NSA_W), F32), jax.ShapeDtypeStruct((b, t, LANES), F32)],
        compiler_params=_cparams(("parallel", "parallel")),
        name="nsa_cmp_attn",
    )(q3, kcv, _overlap(ncp, n_c))


def _gate_expand():
    e = np.zeros((GN_PAD, 3 * NSA_W), np.float32)
    for h in range(NSA_HEADS):
        for k in range(3):
            e[h * 3 + k, k * NSA_W + h * HEAD_DIM:k * NSA_W + (h + 1) * HEAD_DIM] = 1.0
    return jnp.asarray(e, BF16)


def _combine(y_ref, gn_ref, ge_ref, oc, osel, owin):
    gates = _dot2(_sigmoid(gn_ref[...]), ge_ref[...])
    y_ref[...] = (gates[:, :NSA_W] * oc + gates[:, NSA_W:2 * NSA_W] * osel + gates[:, 2 * NSA_W:] * owin)


def _per_head(allow):
    f = jnp.where(allow, 1.0, 0.0)
    return jnp.concatenate([f] * HPG, axis=0) > 0.5


def _sel_mask_tile(sel, g, kv0, tk):
    r = lax.broadcasted_iota(jnp.int32, (LANES, tk), 0)
    blk = (kv0 + lax.broadcasted_iota(jnp.int32, (LANES, tk), 1)) // L_SEL
    expand = (r == g * NBLK_PAD + blk).astype(BF16)
    return jnp.dot(sel.astype(BF16), expand, preferred_element_type=F32)


LOG2E = 1.4426950408889634


def _nsa_prompt_body(qr_ref, slc_ref, win_ref, oh_ref, sel_ref, oc_ref, gn_ref, ge_ref, y_ref,
                     os_ref, ow_ref, *, tq, tk, lw):
    qi = pl.program_id(1)
    q0 = qi * tq
    t_all = slc_ref.shape[0]
    nrow = HPG * tq
    heads = lambda a: jnp.concatenate([a] * HPG, axis=0)
    sel = sel_ref[...]
    lane = lax.broadcasted_iota(jnp.int32, (tq, LANES), 1)
    qpos = q0 + lax.broadcasted_iota(jnp.int32, (tq, 1), 0)
    w0 = pl.multiple_of(jnp.clip(q0 + tq - lw, 0, t_all - lw), 16)
    kpos_w = w0 + lax.broadcasted_iota(jnp.int32, (tq, lw), 1)
    wbias = heads(jnp.where((kpos_w <= qpos) & (kpos_w > qpos - WINDOW), 0.0, NEG))
    n_full = q0 // tk
    kv_diag = pl.multiple_of(n_full * tk, tk)
    kpos_d = kv_diag + lax.broadcasted_iota(jnp.int32, (tq, tk), 1)
    cbias = heads(jnp.where(kpos_d <= qpos, 0.0, NEG))
    qst, qaug = [], []
    for g in range(NSA_KV):
        q4 = jnp.concatenate([_head_to_half(qr_ref, g * HPG + hl, g) for hl in range(HPG)], axis=0)
        q4 = q4 * (HEAD_DIM ** -0.5 * LOG2E)
        selg = sel if g == 0 else pltpu.roll(sel, NBLK_PAD, 1)
        sbias = jnp.where(lane < NBLK_PAD, jnp.where(selg > 0.5, 0.0, NEG), 0.0)
        qst.append(q4.astype(BF16))
        qaug.append(jnp.concatenate([q4, heads(sbias)], axis=1).astype(BF16))

    def fold(carry, s, v):
        m, l, acc = carry
        m_new = jnp.maximum(m, jnp.max(s, axis=-1, keepdims=True))
        p = jnp.exp2(s - m_new)
        alpha = jnp.exp2(m - m_new)
        return (m_new, alpha * l + jnp.sum(p, axis=-1, keepdims=True),
                alpha * acc + jnp.dot(p.astype(BF16), v, preferred_element_type=F32))

    def sel_step(kv0, carries, bias):
        kaug = jnp.concatenate([slc_ref[pl.ds(kv0, tk), :KV_W], oh_ref[pl.ds(kv0, tk), :]], axis=1)
        v = slc_ref[pl.ds(kv0, tk), KV_W:]
        out = []
        for g in range(NSA_KV):
            s = lax.dot_general(qaug[g], kaug, (((1,), (1,)), ((), ())), preferred_element_type=F32)
            out.append(fold(carries[g], s if bias is None else s + bias, v))
        return tuple(out)

    init = (jnp.full((nrow, 1), NEG, F32), jnp.zeros((nrow, 1), F32), jnp.zeros((nrow, LANES), F32))
    carries = lax.fori_loop(0, n_full, lambda it, c: sel_step(pl.multiple_of(it * tk, tk), c, None),
                            (init,) * NSA_KV)
    carries = sel_step(kv_diag, carries, cbias)
    kw = win_ref[pl.ds(w0, lw), :KV_W]
    vw = win_ref[pl.ds(w0, lw), KV_W:]
    for g in range(NSA_KV):
        m, l, acc = carries[g]
        o = acc / l
        _halves_to_heads(os_ref, [o[hl * tq:(hl + 1) * tq] for hl in range(HPG)], g)
        s = lax.dot_general(qst[g], kw, (((1,), (1,)), ((), ())), preferred_element_type=F32) + wbias
        e = jnp.exp2(s - jnp.max(s, axis=-1, keepdims=True))
        o = jnp.dot(e.astype(BF16), vw, preferred_element_type=F32) / jnp.sum(e, axis=-1, keepdims=True)
        _halves_to_heads(ow_ref, [o[hl * tq:(hl + 1) * tq] for hl in range(HPG)], g)
    _combine(y_ref, gn_ref, ge_ref, oc_ref[...], os_ref[...], ow_ref[...])


def _nsa_prompt(qr3, slc3, win3, sel3, oc3, gn3, tq, tk):
    b, t, _ = qr3.shape
    lw = min(WINDOW + tq, t)
    qblk = lambda w: pl.BlockSpec((None, tq, w), lambda i, j: (i, j, 0))
    full = pl.BlockSpec((None, t, KV_ROW), lambda i, j: (i, 0, 0))
    onehot = jnp.asarray(np.arange(t)[:, None] // L_SEL == np.arange(LANES)[None, :], BF16)
    return pl.pallas_call(
        functools.partial(_nsa_prompt_body, tq=tq, tk=tk, lw=lw),
        grid=(b, t // tq),
        in_specs=[qblk(NSA_W), full, full, pl.BlockSpec((t, LANES), lambda i, j: (0, 0)),
                  qblk(LANES), qblk(NSA_W), qblk(GN_PAD),
                  pl.BlockSpec((GN_PAD, 3 * NSA_W), lambda i, j: (0, 0))],
        out_specs=qblk(NSA_W),
        out_shape=jax.ShapeDtypeStruct((b, t, NSA_W), F32),
        scratch_shapes=[pltpu.VMEM((tq, NSA_W), F32), pltpu.VMEM((tq, NSA_W), F32)],
        compiler_params=_cparams(("parallel", "arbitrary")),
        name="nsa_attn_prompt",
    )(qr3, slc3, win3, onehot, sel3, oc3, gn3, _gate_expand())


def _nsa_sample_body(*refs, n_pages, past_len):
    page_refs = refs[1:1 + n_pages]
    (qr_ref, snew_ref, wold_ref, wnew_ref, sel_ref, oc_ref, gn_ref, ge_ref, y_ref, os_ref, ow_ref) = refs[1 + n_pages:]
    tq = qr_ref.shape[0]
    page = page_refs[0].shape[-1]
    wb = wold_ref.shape[-1]
    sel = sel_ref[...]
    qpos = past_len + lax.broadcasted_iota(jnp.int32, (tq, 1), 0)

    def head_q(h):
        c = qr_ref[:, (h // 2) * LANES:(h // 2 + 1) * LANES]
        return (pltpu.roll(c, HEAD_DIM, 1) if h % 2 else c)[:, :HEAD_DIM]

    def attend(q, cached, fresh):
        scores = ([jnp.where(allow, _dot(q, kt), NEG) for kt, _, allow in cached]
                  + [jnp.where(allow, _dot_nt(q, k), NEG) for k, _, allow in fresh])
        masks = [p[2] for p in cached + fresh]
        m = functools.reduce(jnp.maximum, [jnp.max(s, axis=-1, keepdims=True) for s in scores])
        es = [jnp.where(allow, jnp.exp(s - m), 0.0) for s, allow in zip(scores, masks)]
        l = sum(jnp.sum(e, axis=-1, keepdims=True) for e in es)
        acc = sum([_dot_nt(e, vt) for e, (_, vt, _) in zip(es, cached)]
                  + [_dot(e, v) for e, (_, v, _) in zip(es[len(cached):], fresh)])
        return acc / l

    def put_heads(o_ref, o, g):
        for pr in range(HPG // 2):
            pair = jnp.concatenate([o[2 * pr * tq:(2 * pr + 1) * tq], o[(2 * pr + 1) * tq:(2 * pr + 2) * tq]], axis=1)
            c = (g * HPG) // 2 + pr
            o_ref[:, c * LANES:(c + 1) * LANES] = pair

    pad = lambda ref: jnp.concatenate([ref[...], jnp.zeros((page - tq, KV_ROW), F32)], axis=0)
    snew, wnew = pad(snew_ref), pad(wnew_ref)
    kpos_new = past_len + lax.broadcasted_iota(jnp.int32, (tq, page), 1)
    causal_new = kpos_new <= qpos
    kpos_old = past_len - wb + lax.broadcasted_iota(jnp.int32, (tq, wb), 1)
    old_ok = _per_head((kpos_old > qpos - WINDOW) & (kpos_old >= 0))
    new_ok = _per_head(causal_new & (kpos_new > qpos - WINDOW))
    for g in range(NSA_KV):
        q = jnp.concatenate([head_q(g * HPG + hl) for hl in range(HPG)], axis=0) * HEAD_DIM ** -0.5
        kcol = slice(g * HEAD_DIM, (g + 1) * HEAD_DIM)
        vcol = slice(KV_W + g * HEAD_DIM, KV_W + (g + 1) * HEAD_DIM)
        cached = [(page_refs[k][0, g], page_refs[k][1, g], _per_head(_sel_mask_tile(sel, g, k * page, page) > 0.5))
                  for k in range(n_pages)]
        fresh = [(snew[:, kcol], snew[:, vcol], _per_head((_sel_mask_tile(sel, g, past_len, page) > 0.5) & causal_new))]
        put_heads(os_ref, attend(q, cached, fresh), g)
        put_heads(ow_ref, attend(q, [(wold_ref[0, g], wold_ref[1, g], old_ok)], [(wnew[:, kcol], wnew[:, vcol], new_ok)]), g)
    _combine(y_ref, gn_ref, ge_ref, oc_ref[...], os_ref[...], ow_ref[...])


def _nsa_sample(qr3, slc_t, page_table, snew3, wold_t, wnew3, sel3, oc3, gn3):
    b, tq, _ = qr3.shape
    n_pages = page_table.shape[1]
    page = slc_t.shape[-1]
    wb = wold_t.shape[-1]
    per = lambda r, w: pl.BlockSpec((None, r, w), lambda i, pt: (i, 0, 0))
    pages = [_page_spec(page, functools.partial(lambda k, i, pt: (pt[i, k], 0, 0, 0, 0), k)) for k in range(n_pages)]
    return pl.pallas_call(
        functools.partial(_nsa_sample_body, n_pages=n_pages, past_len=n_pages * page),
        grid_spec=pltpu.PrefetchScalarGridSpec(
            num_scalar_prefetch=1, grid=(b,),
            in_specs=pages + [per(tq, NSA_W), per(tq, KV_ROW), _page_spec(wb, lambda i, pt: (i, 0, 0, 0, 0)),
                              per(tq, KV_ROW), per(tq, LANES), per(tq, NSA_W), per(tq, GN_PAD),
                              pl.BlockSpec((GN_PAD, 3 * NSA_W), lambda i, pt: (0, 0))],
            out_specs=per(tq, NSA_W),
            scratch_shapes=[pltpu.VMEM((tq, NSA_W), F32), pltpu.VMEM((tq, NSA_W), F32)]),
        out_shape=jax.ShapeDtypeStruct((b, tq, NSA_W), F32),
        compiler_params=_cparams(("arbitrary",)),
        name="nsa_attn_sample",
    )(page_table, *([slc_t] * n_pages), qr3, snew3, wold_t, wnew3, sel3, oc3, gn3, _gate_expand())


def _final_body(x_ref, yraw_ref, bonus_ref, g_ref, yn_ref, gates_ref, lnw_ref, lnb_ref, bd_ref,
                wbr_ref, wbn_ref, wo_ref, gffn_ref, wg_ref, wu_ref, wd_ref, gfin_ref, y_ref):
    bd = bd_ref[...]
    y = yraw_ref[...]
    d = y - _dot2(y, bd) * (1.0 / RWKV_HEAD)
    var = _dot2(d * d, bd) * (1.0 / RWKV_HEAD)
    y_r = (d * lax.rsqrt(var + RWKV_GN_EPS) * lnw_ref[...] + lnb_ref[...] + bonus_ref[...]) * g_ref[...]
    merged = (_sigmoid(gates_ref[:, :D_MODEL]) * _dot(y_r, wbr_ref[...])
              + _sigmoid(gates_ref[:, D_MODEL:]) * _dot(yn_ref[...], wbn_ref[...]))
    x1 = x_ref[...] + _dot(merged, wo_ref[...])
    h2 = x1 * lax.rsqrt(jnp.mean(x1 * x1, axis=-1, keepdims=True) + NORM_EPS) * gffn_ref[...]
    hb = h2.astype(BF16)
    up = _silu(jnp.dot(hb, wg_ref[...], preferred_element_type=F32)) * jnp.dot(hb, wu_ref[...], preferred_element_type=F32)
    x2 = x1 + _dot(up, wd_ref[...])
    y_ref[...] = x2 * lax.rsqrt(jnp.mean(x2 * x2, axis=-1, keepdims=True) + NORM_EPS) * gfin_ref[...]


def _final(x2d, yraw, bonus, g, yn, gates, prm, tm):
    m = x2d.shape[0]
    tm = min(tm, m)
    row = lambda w: pl.BlockSpec((tm, w), lambda i: (i, 0))
    return pl.pallas_call(
        _final_body,
        grid=(m // tm,),
        in_specs=[row(D_MODEL), row(RWKV_W), row(RWKV_W), row(RWKV_W), row(NSA_W), row(2 * D_MODEL),
                  _const_spec((1, RWKV_W)), _const_spec((1, RWKV_W)), _const_spec((RWKV_W, RWKV_W)),
                  _const_spec((RWKV_W, D_MODEL)), _const_spec((NSA_W, D_MODEL)),
                  _const_spec((D_MODEL, D_MODEL)), _const_spec((1, D_MODEL)),
                  _const_spec((D_MODEL, D_FF)), _const_spec((D_MODEL, D_FF)), _const_spec((D_FF, D_MODEL)),
                  _const_spec((1, D_MODEL))],
        out_specs=row(D_MODEL),
        out_shape=jax.ShapeDtypeStruct((m, D_MODEL), F32),
        compiler_params=_cparams(("parallel",)),
        name="merge_ffn",
    )(x2d, yraw, bonus, g, yn, gates, prm["ln_w"], prm["ln_b"], prm["bd"], prm["w_br_rwkv"],
      prm["w_br_nsa"], prm["w_out"], prm["g_ffn"], prm["w_ffn_gate"], prm["w_ffn_up"],
      prm["w_ffn_down"], prm["g_final"])


_S = (0, RWKV_W, RWKV_W + DECAY_RANK, 2 * RWKV_W + DECAY_RANK, 3 * RWKV_W + DECAY_RANK,
      3 * RWKV_W + DECAY_RANK + AAA_RANK, RWKV_COLS)
_RW_ORDER = (0, 2, 3, 1, 4, 5)


def _rw_regroup(x):
    return jnp.concatenate([x[..., _S[i]:_S[i + 1]] for i in _RW_ORDER], axis=-1)


def _rw_ungroup(y):
    starts = np.cumsum([0] + [_S[i + 1] - _S[i] for i in _RW_ORDER])
    where = {seg: k for k, seg in enumerate(_RW_ORDER)}
    return jnp.concatenate([y[..., starts[where[i]]:starts[where[i] + 1]] for i in range(len(_RW_ORDER))], axis=-1)


def _prep_params(l, g_mix, w_in, rwkv_mu, rwkv_w0, rwkv_w_decay, rwkv_a0, rwkv_w_aaa, rwkv_w_gate,
                 rwkv_k_k, rwkv_k_a, rwkv_r_k, rwkv_ln_w, rwkv_ln_b, w_br_rwkv, w_br_nsa, w_out, g_ffn,
                 w_ffn_gate, w_ffn_up, w_ffn_down, g_final):
    w = w_in[l]
    nsa0 = RWKV_COLS
    kv_end = nsa0 + NSA_W + 6 * KV_W
    zc = lambda n: jnp.zeros((D_MODEL, n), F32)
    w_r = jnp.concatenate([_rw_regroup(w[:, :RWKV_COLS]), zc(RW_PAD - RWKV_COLS), w[:, nsa0:kv_end],
                           w[:, kv_end:nsa0 + NSA_COLS], zc(GN_PAD - 3 * NSA_HEADS),
                           w[:, nsa0 + NSA_COLS:]], axis=1).astype(BF16)
    mu = jnp.concatenate([_rw_regroup(rwkv_mu[l]), jnp.zeros((RW_PAD - RWKV_COLS,), F32)])[None]
    wda = jnp.zeros((LANES, 2 * RWKV_W), F32)
    wda = wda.at[:DECAY_RANK, :RWKV_W].set(rwkv_w_decay[l]).at[DECAY_RANK:, RWKV_W:].set(rwkv_w_aaa[l])
    wg = jnp.zeros((RW_PAD - 3 * RWKV_W - LANES, RWKV_W), F32).at[:GATE_RANK].set(rwkv_w_gate[l])
    hid = np.arange(RWKV_W) // RWKV_HEAD
    bd = jnp.asarray(hid[:, None] == hid[None, :], BF16)
    r1 = lambda a: a.reshape(1, -1)
    return dict(
        g_mix=r1(g_mix[l]), w_r=w_r, mu=mu, wda=wda.astype(BF16), w0=r1(rwkv_w0[l]), a0=r1(rwkv_a0[l]),
        wg=wg.astype(BF16), k_k=r1(rwkv_k_k[l]), k_a=r1(rwkv_k_a[l]), r_k=r1(rwkv_r_k[l]), bd=bd,
        ln_w=r1(rwkv_ln_w[l]), ln_b=r1(rwkv_ln_b[l]), w_br_rwkv=w_br_rwkv[l].astype(BF16),
        w_br_nsa=w_br_nsa[l].astype(BF16), w_out=w_out[l].astype(BF16), g_ffn=r1(g_ffn[l]),
        w_ffn_gate=w_ffn_gate[l].astype(BF16), w_ffn_up=w_ffn_up[l].astype(BF16),
        w_ffn_down=w_ffn_down[l].astype(BF16), g_final=r1(g_final))


def _rope_tables(pos):
    half = ROT_DIM // 2
    inv = ROPE_THETA ** (-jnp.arange(half, dtype=F32) / half)
    ang = pos.astype(F32)[:, None] * inv[None, :]
    cos, sin = jnp.cos(ang), jnp.sin(ang)
    n = pos.shape[0]
    one, zero = jnp.ones((n, HEAD_DIM - ROT_DIM), F32), jnp.zeros((n, HEAD_DIM - ROT_DIM), F32)
    z8 = jnp.zeros((n, half), F32)
    c = jnp.concatenate([cos, cos, one], 1)
    s1 = jnp.concatenate([-sin, z8, zero], 1)
    s2 = jnp.concatenate([z8, sin, zero], 1)
    return tuple(jnp.concatenate([a, a], 1) for a in (c, s1, s2))


def _group(x, pos, shift0, s0, prm, cw, past, tm, pre_blk, tq):
    b, t, _ = x.shape
    m = b * t
    x2d = x.reshape(m, D_MODEL)
    p_rw, q, q_r, cmp_rows, slc_rows, win_rows, gn, gates, slc_bf, win_bf = _in_proj(
        x2d, prm["g_mix"], prm["w_r"], pos, tm)

    shift_r = jnp.concatenate([_rw_regroup(shift0), jnp.zeros((b, RW_PAD - RWKV_COLS), F32)], 1)[:, None, :]
    *scan_in, v, bonus, g = _rwkv_pre(p_rw.reshape(b, t, RW_PAD), shift_r, prm, *pre_blk)
    y_raw, st = _rwkv_chunk_scan(scan_in, v, _state_to_blockdiag(s0), t)
    s_new = _blockdiag_to_state(st)
    shift_new = _rw_ungroup(p_rw.reshape(b, t, RW_PAD)[:, -1, :RWKV_COLS])

    r3 = lambda a: a.reshape(b, t, a.shape[-1])
    rows_last = lambda c: jnp.moveaxis(c, 1, -1)
    rows_first = lambda c: jnp.moveaxis(c, -1, 1)
    if past is None:
        kcv = _compress_seq(cmp_rows, cw)
        n_c = t // D_CMP - 1
        oc, sel = _cmp_attn(r3(q), kcv, tq, 0, n_c)
        y_n = _nsa_prompt(r3(q_r), r3(slc_bf), r3(win_bf), sel, oc, r3(gn), min(2 * tq, t), min(512, t))
        new_cmp, new_slc = rows_first(cmp_rows), rows_first(slc_rows)
        win_state = rows_first(win_rows[..., t - min(WINDOW, t):])
    else:
        cmp_cache, slc_cache, win_cache, page_table = past
        past_len = page_table.shape[1] * cmp_cache.shape[1]
        kcv = _compress_paged(rows_last(cmp_cache), page_table, cw)
        n_c = (past_len + t) // D_CMP - 1
        oc, sel = _cmp_attn(r3(q), kcv, tq, past_len, n_c)
        y_n = _nsa_sample(r3(q_r), rows_last(slc_cache), page_table, r3(slc_rows), rows_last(win_cache),
                          r3(win_rows), sel, oc, r3(gn))
        kv6 = lambda a: a.reshape(b, t, 2, NSA_KV, HEAD_DIM)
        new_cmp, new_slc = kv6(cmp_rows), kv6(slc_rows)
        win_all = jnp.concatenate([win_cache, kv6(win_rows)], axis=1)
        win_state = win_all[:, win_all.shape[1] - min(WINDOW, win_all.shape[1]):]

    y = _final(x2d, y_raw, bonus, g, y_n.reshape(m, NSA_W), gates, prm, tm)
    return (y.reshape(b, t, D_MODEL), new_cmp, new_slc, win_state, s_new, shift_new)


def kernel(x_prompt, x_sample, cache_cmp_kv, cache_slc_kv, cache_win_kv, state_rwkv, state_rwkv_shift, page_table, g_mix, w_in, rwkv_mu, rwkv_w0, rwkv_w_decay, rwkv_a0, rwkv_w_aaa, rwkv_w_gate, rwkv_k_k, rwkv_k_a, rwkv_r_k, rwkv_ln_w, rwkv_ln_b, nsa_pe_cmp, nsa_w_cmp1, nsa_w_cmp2, w_br_rwkv, w_br_nsa, w_out, g_ffn, w_ffn_gate, w_ffn_up, w_ffn_down, g_final):
    depth = w_in.shape[0]
    assert depth == 1, "single-layer trunk"
    b, t, _ = x_prompt.shape
    bs, ts, _ = x_sample.shape
    n_pool, page = cache_cmp_kv.shape[1:3]
    past_len = page_table.shape[1] * page
    assert (past_len + ts) // D_CMP == past_len // D_CMP and past_len % D_CMP == 0
    prm = _prep_params(0, g_mix, w_in, rwkv_mu, rwkv_w0, rwkv_w_decay, rwkv_a0, rwkv_w_aaa, rwkv_w_gate,
                       rwkv_k_k, rwkv_k_a, rwkv_r_k, rwkv_ln_w, rwkv_ln_b, w_br_rwkv, w_br_nsa, w_out,
                       g_ffn, w_ffn_gate, w_ffn_up, w_ffn_down, g_final)
    cw = _compress_weights(nsa_pe_cmp[0], nsa_w_cmp1[0], nsa_w_cmp2[0])

    outs_p = _group(x_prompt, jnp.arange(t), jnp.zeros((b, RWKV_COLS), F32),
                    jnp.zeros((b, RWKV_HEADS, RWKV_HEAD, RWKV_HEAD), F32), prm, cw, None,
                    tm=256, pre_blk=(1, min(512, t)), tq=min(128, t))
    outs_s = _group(x_sample, past_len + jnp.arange(ts), state_rwkv_shift[0], state_rwkv[0], prm, cw,
                    (cache_cmp_kv[0], cache_slc_kv[0], cache_win_kv[0], page_table),
                    tm=256, pre_blk=(min(64, bs), ts), tq=ts)
    return (outs_p[0], outs_s[0]) + tuple(o[None] for o in outs_p[1:]) + tuple(o[None] for o in outs_s[1:])
```

```python
import functools

import jax
import jax.numpy as jnp
import numpy as np
from jax import lax
from jax.experimental import pallas as pl
from jax.experimental.pallas import tpu as pltpu

F32 = jnp.float32
BF16 = jnp.bfloat16

D_MODEL = 1024
RWKV_HEADS = 8
RWKV_HEAD = 64
RWKV_W = RWKV_HEADS * RWKV_HEAD
DECAY_RANK = 64
AAA_RANK = 64
GATE_RANK = 160
RWKV_GN_EPS = 64e-5
NSA_HEADS = 8
NSA_KV = 2
HEAD_DIM = 64
HPG = NSA_HEADS // NSA_KV
NSA_W = NSA_HEADS * HEAD_DIM
KV_W = NSA_KV * HEAD_DIM
L_CMP = 32
D_CMP = 16
CMP_HID = 128
L_SEL = 64
N_SEL = 16
WINDOW = 512
ROT_DIM = HEAD_DIM // 4
ROPE_THETA = 500000.0
D_FF = -(-8 * D_MODEL // (3 * 256)) * 256
NORM_EPS = 1e-6
NEG = -1e30
FORCE = 1e6
RWKV_COLS = 3 * RWKV_W + DECAY_RANK + AAA_RANK + GATE_RANK
NSA_COLS = NSA_W + 6 * KV_W + 3 * NSA_HEADS

LANES = 128
RW_PAD = 1920
GN_PAD = LANES
KV_ROW = 2 * KV_W
NBLK_PAD = 64
VMEM_LIMIT = 56 * 1024 * 1024


def _cparams(sem):
    return pltpu.CompilerParams(dimension_semantics=sem, vmem_limit_bytes=VMEM_LIMIT)


def _const_spec(shape):
    nd = len(shape)
    return pl.BlockSpec(shape, lambda *_: (0,) * nd, pipeline_mode=pl.Buffered(1))


def _dot(a, b):
    return jnp.dot(a.astype(BF16), b.astype(BF16), preferred_element_type=F32)


def _dot_nt(a, b):
    return lax.dot_general(a.astype(BF16), b.astype(BF16), (((1,), (1,)), ((), ())),
                           preferred_element_type=F32)


def _dot2(a, b):
    hi = a.astype(BF16)
    lo = (a - hi.astype(F32)).astype(BF16)
    return (jnp.dot(hi, b, preferred_element_type=F32)
            + jnp.dot(lo, b, preferred_element_type=F32))


def _sigmoid(x):
    return 1.0 / (1.0 + jnp.exp(-x))


def _silu(x):
    return x * _sigmoid(x)


def _rope128(x, cos, s1, s2):
    half = ROT_DIM // 2
    return x * cos + pltpu.roll(x, LANES - half, 1) * s1 + pltpu.roll(x, half, 1) * s2


_C_Q = RW_PAD
_C_CMP = _C_Q + NSA_W
_C_SLC = _C_CMP + KV_ROW
_C_WIN = _C_SLC + KV_ROW
_C_GN = _C_WIN + KV_ROW
_C_GATE = _C_GN + GN_PAD
_C_END = _C_GATE + 2 * D_MODEL


def _proj_body(x_ref, g_ref, w_ref, cos_ref, s1_ref, s2_ref,
               prw_ref, q_ref, qr_ref, cmp_ref, slc_ref, win_ref, gn_ref, gates_ref, slcb_ref, winb_ref,
               *, rows_last):
    x = x_ref[...]
    ms = jnp.mean(x * x, axis=-1, keepdims=True)
    h = (x * lax.rsqrt(ms + NORM_EPS) * g_ref[...]).astype(BF16)

    def mm(a, b):
        return jnp.dot(h, w_ref[:, a:b], preferred_element_type=F32)

    def put_rows(ref, kv):
        ref[...] = kv.T.reshape(ref.shape) if rows_last else kv

    cos, s1, s2 = cos_ref[...], s1_ref[...], s2_ref[...]
    prw_ref[...] = mm(0, _C_Q)
    q = mm(_C_Q, _C_CMP)
    q_ref[...] = q
    for c in range(NSA_W // LANES):
        sl = slice(c * LANES, (c + 1) * LANES)
        qr_ref[:, sl] = _rope128(q[:, sl], cos, s1, s2)
    put_rows(cmp_ref, mm(_C_CMP, _C_SLC))
    for lo, hi, f_ref, h_ref in ((_C_SLC, _C_WIN, slc_ref, slcb_ref), (_C_WIN, _C_GN, win_ref, winb_ref)):
        kv = mm(lo, hi)
        kv = jnp.concatenate([_rope128(kv[:, :KV_W], cos, s1, s2), kv[:, KV_W:]], axis=1)
        put_rows(f_ref, kv)
        h_ref[...] = kv.astype(BF16)
    gn_ref[...] = mm(_C_GN, _C_GATE)
    gates_ref[...] = mm(_C_GATE, _C_END)


def _in_proj(x2d, g_mix, w_r, pos, tm):
    m = x2d.shape[0]
    t = pos.shape[0]
    tm = min(tm, m)
    period = max(t, tm) // tm
    tabs = _rope_tables(jnp.tile(pos, max(1, tm // t)))
    row = lambda w: pl.BlockSpec((tm, w), lambda i: (i, 0))
    tab = pl.BlockSpec((tm, LANES), lambda i: (i % period, 0))
    f32 = lambda w: (row(w), jax.ShapeDtypeStruct((m, w), F32))
    rows_last = t % tm == 0
    if rows_last:
        kv = (pl.BlockSpec((None, 2, NSA_KV, HEAD_DIM, tm), lambda i: (i // period, 0, 0, 0, i % period)),
              jax.ShapeDtypeStruct((m // t, 2, NSA_KV, HEAD_DIM, t), F32))
    else:
        kv = f32(KV_ROW)
    bf = (row(KV_ROW), jax.ShapeDtypeStruct((m, KV_ROW), BF16))
    outs = [f32(RW_PAD), f32(NSA_W), f32(NSA_W), kv, kv, kv, f32(GN_PAD), f32(2 * D_MODEL), bf, bf]
    return pl.pallas_call(
        functools.partial(_proj_body, rows_last=rows_last),
        grid=(m // tm,),
        in_specs=[row(D_MODEL), _const_spec((1, D_MODEL)), _const_spec((D_MODEL, _C_END)),
                  tab, tab, tab],
        out_specs=[o[0] for o in outs],
        out_shape=[o[1] for o in outs],
        compiler_params=_cparams(("parallel",)),
        name="in_proj",
    )(x2d, g_mix, w_r, *tabs)


def _rwkv_pre_body(p_ref, sh_ref, mu_ref, wda_ref, w0_ref, a0_ref, wg_ref, kk_ref, ka_ref,
                   rk_ref, bd_ref, r_ref, lw_ref, k_ref, a_ref, b_ref, v_ref, bonus_ref,
                   g_ref, carry_ref):
    j = pl.program_id(1)
    bb, tt, c = p_ref.shape
    n = bb * tt
    x3 = p_ref[...]
    x = x3.reshape(n, c)
    prev = jnp.where(j == 0, sh_ref[...], carry_ref[...])
    carry_ref[...] = x3[:, tt - 1:tt, :]
    prev_rows = jnp.broadcast_to(prev, (bb, tt, c)).reshape(n, c)
    row = lax.broadcasted_iota(jnp.int32, (n, c), 0)
    shifted = jnp.where(row % tt == 0, prev_rows, pltpu.roll(x, 1, 0))
    xs = x + (shifted - x) * mu_ref[...]
    r = xs[:, 0:RWKV_W]
    k = xs[:, RWKV_W:2 * RWKV_W]
    v = xs[:, 2 * RWKV_W:3 * RWKV_W]
    wa = xs[:, 3 * RWKV_W:3 * RWKV_W + LANES]
    gl = xs[:, 3 * RWKV_W + LANES:RW_PAD]
    lane = lax.broadcasted_iota(jnp.int32, wa.shape, 1)
    z = _dot(jnp.where(lane < DECAY_RANK, jnp.tanh(wa), wa), wda_ref[...])
    u = -(w0_ref[...] + z[:, :RWKV_W])
    softplus = jnp.maximum(u, 0.0) + jnp.log1p(jnp.exp(-jnp.abs(u)))
    log_decay = -jnp.exp(-softplus - 0.5)
    a = _sigmoid(a0_ref[...] + z[:, RWKV_W:])
    g_ref[...] = _dot(_sigmoid(gl), wg_ref[...])
    bd = bd_ref[...]
    kk = k * kk_ref[...]
    kk = kk * lax.rsqrt(jnp.maximum(_dot2(kk * kk, bd), 1e-24))
    k_h = k * (1.0 + (a - 1.0) * ka_ref[...])
    bonus_ref[...] = _dot2(r * k_h * rk_ref[...], bd) * v
    v_ref[...] = v
    r_ref[...] = r
    lw_ref[...] = log_decay
    k_ref[...] = k_h
    a_ref[...] = -kk
    b_ref[...] = kk * a


def _rwkv_pre(p3, shift0, prm, bb, tt):
    b, t, _ = p3.shape
    m = b * t
    n = bb * tt
    nt = t // tt
    rowm = pl.BlockSpec((n, RWKV_W), lambda i, j: (i * nt + j, 0))
    outs = pl.pallas_call(
        _rwkv_pre_body,
        grid=(b // bb, nt),
        in_specs=[pl.BlockSpec((bb, tt, RW_PAD), lambda i, j: (i, j, 0)),
                  pl.BlockSpec((bb, 1, RW_PAD), lambda i, j: (i, 0, 0)),
                  _const_spec((1, RW_PAD)), _const_spec((LANES, 2 * RWKV_W)),
                  _const_spec((1, RWKV_W)), _const_spec((1, RWKV_W)),
                  _const_spec((RW_PAD - 3 * RWKV_W - LANES, RWKV_W)),
                  _const_spec((1, RWKV_W)), _const_spec((1, RWKV_W)), _const_spec((1, RWKV_W)),
                  _const_spec((RWKV_W, RWKV_W))],
        out_specs=[rowm] * 8,
        out_shape=[jax.ShapeDtypeStruct((m, RWKV_W), F32)] * 8,
        scratch_shapes=[pltpu.VMEM((bb, 1, RW_PAD), F32)],
        compiler_params=_cparams(("parallel", "arbitrary")),
        name="rwkv_pre",
    )(p3, shift0, prm["mu"], prm["wda"], prm["w0"], prm["a0"], prm["wg"], prm["k_k"],
      prm["k_a"], prm["r_k"], prm["bd"])
    return outs


RWKV_CHUNK = 64


def _split3(x):
    x1 = x.astype(BF16)
    r1 = x - x1.astype(F32)
    x2 = r1.astype(BF16)
    return x1, x2, (r1 - x2.astype(F32)).astype(BF16)


def _chunk_scan_body(r_ref, lw_ref, k_ref, a_ref, b_ref, v_ref, s0_ref, tri_ref, y_ref, s_ref):
    nbat, c = r_ref.shape[0], r_ref.shape[1]
    n = 2 * c
    first = pl.program_id(1) == 0
    left = lax.broadcasted_iota(jnp.int32, (c, LANES), 1) < RWKV_HEAD
    stack = lambda x: jnp.concatenate([jnp.where(left, x, 0.0), jnp.where(left, 0.0, x)], axis=0)
    ri = lax.broadcasted_iota(jnp.int32, (n, n), 0)
    ci = lax.broadcasted_iota(jnp.int32, (n, n), 1)
    strict = ((ri < c) == (ci < c)) & (ci < ri)
    eye = jnp.where(ri == ci, 1.0, 0.0)
    ri2 = lax.broadcasted_iota(jnp.int32, (n, 2 * n), 0)
    ci2 = lax.broadcasted_iota(jnp.int32, (n, 2 * n), 1) % n
    incl2 = ((ri2 < c) == (ci2 < c)) & (ci2 <= ri2)
    tri = tri_ref[...]
    chains = [(i, p) for i in range(nbat) for p in range(RWKV_HEADS // 2)]
    pre = []
    for i, p in chains:
        sl = slice(p * LANES, (p + 1) * LANES)
        lw = lw_ref[i, :, sl]
        cs = sum(jnp.dot(tri, part, preferred_element_type=F32) for part in _split3(lw))
        c_end = cs[c - 1:c, :]
        e_neg = jnp.exp(-cs)
        e_hat = jnp.exp(c_end - cs)
        at = stack(a_ref[i, :, sl] * jnp.exp(cs - lw))
        rt = stack(r_ref[i, :, sl] * jnp.exp(cs))
        bt = stack(b_ref[i, :, sl] * e_neg)
        kt = stack(k_ref[i, :, sl] * e_neg)
        bk_hat = jnp.concatenate([stack(b_ref[i, :, sl] * e_hat), stack(k_ref[i, :, sl] * e_hat)], axis=0)
        g = _dot_nt(jnp.concatenate([at, rt], axis=0), jnp.concatenate([bt, kt], axis=0))
        pre.append(dict(sl=sl, e_end=jnp.exp(c_end), at=at, rt=rt, bk_hat=bk_hat.astype(BF16),
                        vs=stack(v_ref[i, :, sl]), low=jnp.where(strict, g[:n, :n], 0.0),
                        g_ak=jnp.where(strict, g[:n, n:], 0.0),
                        g_r=jnp.where(incl2, g[n:, :], 0.0)))
    tinv = [eye + q["low"] for q in pre]
    lp = [q["low"] for q in pre]
    for _ in range(c.bit_length() - 2):
        lp = [_dot(x, x) for x in lp]
        tinv = [t + _dot(t, x) for t, x in zip(tinv, lp)]
    state = [jnp.where(first, s0_ref[i, p], s_ref[i, p]) for i, p in chains]
    rhs = [_dot_nt(q["at"], s) + _dot(q["g_ak"], q["vs"]) for q, s in zip(pre, state)]
    uv = [jnp.concatenate([_dot(t, x), q["vs"]], axis=0) for t, x, q in zip(tinv, rhs, pre)]
    for (i, p), q, s, w in zip(chains, pre, state, uv):
        ys = _dot_nt(q["rt"], s) + _dot(q["g_r"], w)
        y_ref[i, :, q["sl"]] = ys[:c] + ys[c:]
        s_ref[i, p] = s * q["e_end"] + lax.dot_general(
            w.astype(BF16), q["bk_hat"], (((0,), (0,)), ((), ())), preferred_element_type=F32)


SCAN_SEQS = 8


def _rwkv_chunk_scan(rows, v, s0pair, t):
    m = v.shape[0]
    b = s0pair.shape[0]
    c = min(RWKV_CHUNK, t)
    nbat = next(d for d in range(SCAN_SEQS, 0, -1) if b % d == 0)
    as3 = lambda x: x.reshape(b, t, RWKV_W)
    rowm = pl.BlockSpec((nbat, c, RWKV_W), lambda i, j: (i, j, 0))
    sspec = pl.BlockSpec((nbat, RWKV_HEADS // 2, LANES, LANES), lambda i, j: (i, 0, 0, 0))
    tri = jnp.asarray(np.tril(np.ones((c, c), np.float32)), BF16)
    y, st = pl.pallas_call(
        _chunk_scan_body,
        grid=(b // nbat, t // c),
        in_specs=[rowm] * 6 + [sspec, pl.BlockSpec((c, c), lambda i, j: (0, 0))],
        out_specs=[rowm, sspec],
        out_shape=[jax.ShapeDtypeStruct((b, t, RWKV_W), F32), jax.ShapeDtypeStruct(s0pair.shape, F32)],
        compiler_params=_cparams(("parallel", "arbitrary")),
        name="rwkv_chunk_scan",
    )(*[as3(x) for x in rows], as3(v), s0pair, tri)
    return y.reshape(m, RWKV_W), st


def _state_to_blockdiag(s):
    b = s.shape[0]
    sp = s.reshape(b, RWKV_HEADS // 2, 2, RWKV_HEAD, RWKV_HEAD)
    z = jnp.zeros_like(sp[:, :, 0])
    top = jnp.concatenate([sp[:, :, 0], z], axis=-1)
    bot = jnp.concatenate([z, sp[:, :, 1]], axis=-1)
    return jnp.concatenate([top, bot], axis=-2)


def _blockdiag_to_state(sp):
    b = sp.shape[0]
    h0 = sp[:, :, :RWKV_HEAD, :RWKV_HEAD]
    h1 = sp[:, :, RWKV_HEAD:, RWKV_HEAD:]
    return jnp.stack([h0, h1], axis=2).reshape(b, RWKV_HEADS, RWKV_HEAD, RWKV_HEAD)


def _compress_body(*refs, n_src):
    pe_ref, w1_ref, w2_ref, o_ref, xs_ref = refs[-5:]
    pages = refs[-5 - n_src:-5]
    page = pages[0].shape[-1]
    for k, pg in enumerate(pages):
        for c in range(2):
            xs_ref[c, k * page:(k + 1) * page, :] = pg[c].reshape(KV_W, page).T
    per = n_src * page // D_CMP
    nseq, ncs = o_ref.shape[0], o_ref.shape[1]
    for c in range(2):
        rows = lambda l: xs_ref[c, pl.ds(l, per, stride=D_CMP), :]
        for j in range(D_CMP // 2):
            xl = jnp.concatenate([rows(2 * j), rows(2 * j + 1)], axis=1)
            f = _dot(xl + pe_ref[c, j:j + 1, :], w1_ref[c, j])
            s = _dot(xl + pe_ref[c, D_CMP // 2 + j:D_CMP // 2 + j + 1, :], w1_ref[c, D_CMP // 2 + j])
            hf = f if j == 0 else hf + f
            hs = s if j == 0 else hs + s
        nc = hf.shape[0]
        out = _dot(_silu(hf + pltpu.roll(hs, nc - 1, 0)), w2_ref[c])
        for q in range(nseq):
            o_ref[q, :, c * KV_W:(c + 1) * KV_W] = out[q * ncs:(q + 1) * ncs]


def _compress_weights(pe, w1, w2):
    eye = jnp.eye(NSA_KV, dtype=F32)
    w1b = jnp.einsum("cldh,gy->clgdyh", w1, eye).reshape(2, L_CMP, KV_W, NSA_KV * CMP_HID)
    w2b = jnp.einsum("chd,gy->cghyd", w2, eye).reshape(2, NSA_KV * CMP_HID, KV_W)
    peb = jnp.broadcast_to(pe[:, :, None, :], (2, L_CMP, NSA_KV, HEAD_DIM)).reshape(2, L_CMP, KV_W)
    return (peb.reshape(2, L_CMP // 2, 2 * KV_W),
            w1b.reshape(2, L_CMP // 2, 2 * KV_W, NSA_KV * CMP_HID).astype(BF16), w2b.astype(BF16))


CMP_SEQS = 4


def _page_spec(page, index_map):
    return pl.BlockSpec((None, 2, NSA_KV, HEAD_DIM, page), index_map)


def _compress_call(rows_t, page_specs, nb, nseq, nc, cw, prefetch):
    hid = NSA_KV * CMP_HID
    n_src = len(page_specs)
    page = page_specs[0].block_shape[-1]
    wspecs = [pl.BlockSpec((2, L_CMP // 2, 2 * KV_W), lambda *_: (0, 0, 0)),
              pl.BlockSpec((2, L_CMP // 2, 2 * KV_W, hid), lambda *_: (0, 0, 0, 0)),
              pl.BlockSpec((2, hid, KV_W), lambda *_: (0, 0, 0))]
    return pl.pallas_call(
        functools.partial(_compress_body, n_src=n_src),
        grid_spec=pltpu.PrefetchScalarGridSpec(
            num_scalar_prefetch=len(prefetch), grid=(nb // nseq,), in_specs=page_specs + wspecs,
            out_specs=pl.BlockSpec((nseq, nc, KV_ROW), lambda b, *_: (b, 0, 0)),
            scratch_shapes=[pltpu.VMEM((2, n_src * page, KV_W), F32)]),
        out_shape=jax.ShapeDtypeStruct((nb, nc, KV_ROW), F32),
        compiler_params=_cparams(("parallel",)),
        name="nsa_compress",
    )(*prefetch, *([rows_t] * n_src), *cw)


def _compress_seq(rows_t, cw):
    b, t = rows_t.shape[0], rows_t.shape[-1]
    specs = [_page_spec(LANES, functools.partial(lambda k, i: (i, 0, 0, 0, k), k)) for k in range(t // LANES)]
    return _compress_call(rows_t, specs, b, 1, t // D_CMP, cw, ())


def _compress_paged(cache_t, page_table, cw):
    b, n_pages = page_table.shape
    page = cache_t.shape[-1]
    nseq = CMP_SEQS if b % CMP_SEQS == 0 else 1
    specs = [_page_spec(page, functools.partial(lambda q, k, i, pt: (pt[i * nseq + q, k], 0, 0, 0, 0), q, k))
             for q in range(nseq) for k in range(n_pages)]
    return _compress_call(cache_t, specs, b, nseq, n_pages * page // D_CMP, cw, (page_table,))


def _head_to_half(q_ref, h, g):
    c = q_ref[:, (h // 2) * LANES:(h // 2 + 1) * LANES]
    if h % 2 != g:
        c = pltpu.roll(c, HEAD_DIM, 1)
    lane = lax.broadcasted_iota(jnp.int32, c.shape, 1)
    return jnp.where((lane >= g * HEAD_DIM) & (lane < (g + 1) * HEAD_DIM), c, 0.0)


def _halves_to_heads(o_ref, outs, g):
    for pr in range(HPG // 2):
        tiles = []
        for hl in (2 * pr, 2 * pr + 1):
            h = g * HPG + hl
            o = outs[hl]
            if h % 2 != g:
                o = pltpu.roll(o, HEAD_DIM, 1)
            tiles.append(o)
        lane = lax.broadcasted_iota(jnp.int32, tiles[0].shape, 1)
        c = (g * HPG) // 2 + pr
        o_ref[:, c * LANES:(c + 1) * LANES] = jnp.where(lane < HEAD_DIM, tiles[0], tiles[1])


def _cmp_attn_body(q_ref, kcv_ref, ov_ref, oc_ref, sel_ref, *, tq, pos0, n_c):
    qi = pl.program_id(1)
    nb, ncp = kcv_ref.shape[0], kcv_ref.shape[1]
    n = nb * tq
    row = lax.broadcasted_iota(jnp.int32, (tq, ncp), 0)
    col = lax.broadcasted_iota(jnp.int32, (tq, ncp), 1)
    mask = _per_head((col * D_CMP + (L_CMP - 1) <= pos0 + qi * tq + row) & (col < n_c))
    imps = []
    for sb in range(nb):
        kc = kcv_ref[sb, :, :KV_W]
        vc = kcv_ref[sb, :, KV_W:]
        imp = None
        for g in range(NSA_KV):
            q4 = jnp.concatenate([_head_to_half(q_ref.at[sb], g * HPG + hl, g) for hl in range(HPG)], axis=0)
            s = jnp.where(mask, _dot_nt(q4, kc) * HEAD_DIM ** -0.5, NEG)
            e = jnp.exp(s - jnp.max(s, axis=-1, keepdims=True))
            p = jnp.where(mask, e / jnp.sum(e, axis=-1, keepdims=True), 0.0)
            o = _dot(p, vc)
            _halves_to_heads(oc_ref.at[sb], [o[hl * tq:(hl + 1) * tq] for hl in range(HPG)], g)
            psum = sum(p[hl * tq:(hl + 1) * tq] for hl in range(HPG))
            ig = _dot2(psum, ov_ref[g])
            imp = ig if imp is None else imp + ig
        imps.append(imp)
    imp_t = jnp.concatenate(imps, axis=0).T
    jb = lax.broadcasted_iota(jnp.int32, (LANES, n), 0) % NBLK_PAD
    cur = (pos0 + qi * tq + lax.broadcasted_iota(jnp.int32, (LANES, n), 1) % tq) // L_SEL
    forced = (jb == 0) | (jb == cur) | (jb == cur - 1)
    score = jnp.where(forced, FORCE, jnp.where(jb <= cur, imp_t, -1.0))
    sub = lax.broadcasted_iota(jnp.int32, (8, n), 0)
    ranks = []
    for g in range(NSA_KV):
        blocks = [score[g * NBLK_PAD + 8 * k:g * NBLK_PAD + 8 * k + 8, :] for k in range(NBLK_PAD // 8)]
        cnt = [jnp.zeros((8, n), F32) for _ in blocks]
        for jp in range(NBLK_PAD):
            other = jnp.broadcast_to(score[g * NBLK_PAD + jp:g * NBLK_PAD + jp + 1, :], (8, n))
            for k, blk in enumerate(blocks):
                if 8 * k > jp:
                    ahead = other >= blk
                elif 8 * k + 7 <= jp:
                    ahead = other > blk
                else:
                    ahead = (other > blk) | ((other == blk) & (sub > jp - 8 * k))
                cnt[k] = cnt[k] + jnp.where(ahead, 1.0, 0.0)
        ranks += cnt
    rank = jnp.concatenate(ranks, axis=0)
    sel = jnp.where((rank < N_SEL) & (jb <= cur), 1.0, 0.0).T
    for sb in range(nb):
        sel_ref[sb] = sel[sb * tq:(sb + 1) * tq]


def _overlap(ncp, n_c):
    ci = np.arange(ncp)[:, None]
    sj = np.arange(NBLK_PAD)[None, :]
    ov = ((ci * D_CMP < (sj + 1) * L_SEL) & (ci * D_CMP + L_CMP > sj * L_SEL) & (ci < n_c)).astype(np.float32)
    z = np.zeros_like(ov)
    return jnp.asarray(np.stack([np.concatenate([ov, z], 1), np.concatenate([z, ov], 1)]), BF16)


def _cmp_attn(q3, kcv, tq, pos0, n_c):
    b, t, _ = q3.shape
    ncp = kcv.shape[1]
    nb = LANES // tq
    blk = lambda r, w: pl.BlockSpec((nb, r, w), lambda i, j: (i, j, 0))
    return pl.pallas_call(
        functools.partial(_cmp_attn_body, tq=tq, pos0=pos0, n_c=n_c),
        grid=(b // nb, t // tq),
        in_specs=[blk(tq, NSA_W), pl.BlockSpec((nb, ncp, KV_ROW), lambda i, j: (i, 0, 0)),
                  pl.BlockSpec((NSA_KV, ncp, LANES), lambda i, j: (0, 0, 0))],
        out_specs=[blk(tq, NSA_W), blk(tq, LANES)],
        out_shape=[jax.ShapeDtypeStruct((b, t, NSA_W), F32), jax.ShapeDtypeStruct((b, t, LANES), F32)],
        compiler_params=_cparams(("parallel", "parallel")),
        name="nsa_cmp_attn",
    )(q3, kcv, _overlap(ncp, n_c))


def _gate_expand():
    e = np.zeros((GN_PAD, 3 * NSA_W), np.float32)
    for h in range(NSA_HEADS):
        for k in range(3):
            e[h * 3 + k, k * NSA_W + h * HEAD_DIM:k * NSA_W + (h + 1) * HEAD_DIM] = 1.0
    return jnp.asarray(e, BF16)


def _combine(y_ref, gn_ref, ge_ref, oc, osel, owin):
    gates = _dot2(_sigmoid(gn_ref[...]), ge_ref[...])
    y_ref[...] = (gates[:, :NSA_W] * oc + gates[:, NSA_W:2 * NSA_W] * osel + gates[:, 2 * NSA_W:] * owin)


def _per_head(allow):
    f = jnp.where(allow, 1.0, 0.0)
    return jnp.concatenate([f] * HPG, axis=0) > 0.5


def _sel_mask_tile(sel, g, kv0, tk):
    r = lax.broadcasted_iota(jnp.int32, (LANES, tk), 0)
    blk = (kv0 + lax.broadcasted_iota(jnp.int32, (LANES, tk), 1)) // L_SEL
    expand = (r == g * NBLK_PAD + blk).astype(BF16)
    return jnp.dot(sel.astype(BF16), expand, preferred_element_type=F32)


LOG2E = 1.4426950408889634


def _nsa_prompt_body(qr_ref, slc_ref, win_ref, oh_ref, sel_ref, oc_ref, gn_ref, ge_ref, y_ref,
                     os_ref, ow_ref, *, tq, tk, lw):
    qi = pl.program_id(1)
    q0 = qi * tq
    t_all = slc_ref.shape[0]
    nrow = HPG * tq
    heads = lambda a: jnp.concatenate([a] * HPG, axis=0)
    sel = sel_ref[...]
    lane = lax.broadcasted_iota(jnp.int32, (tq, LANES), 1)
    qpos = q0 + lax.broadcasted_iota(jnp.int32, (tq, 1), 0)
    w0 = pl.multiple_of(jnp.clip(q0 + tq - lw, 0, t_all - lw), 16)
    kpos_w = w0 + lax.broadcasted_iota(jnp.int32, (tq, lw), 1)
    wbias = heads(jnp.where((kpos_w <= qpos) & (kpos_w > qpos - WINDOW), 0.0, NEG))
    n_full = q0 // tk
    kv_diag = pl.multiple_of(n_full * tk, tk)
    kpos_d = kv_diag + lax.broadcasted_iota(jnp.int32, (tq, tk), 1)
    cbias = heads(jnp.where(kpos_d <= qpos, 0.0, NEG))
    qst, qaug = [], []
    for g in range(NSA_KV):
        q4 = jnp.concatenate([_head_to_half(qr_ref, g * HPG + hl, g) for hl in range(HPG)], axis=0)
        q4 = q4 * (HEAD_DIM ** -0.5 * LOG2E)
        selg = sel if g == 0 else pltpu.roll(sel, NBLK_PAD, 1)
        sbias = jnp.where(lane < NBLK_PAD, jnp.where(selg > 0.5, 0.0, NEG), 0.0)
        qst.append(q4.astype(BF16))
        qaug.append(jnp.concatenate([q4, heads(sbias)], axis=1).astype(BF16))

    def fold(carry, s, v):
        m, l, acc = carry
        m_new = jnp.maximum(m, jnp.max(s, axis=-1, keepdims=True))
        p = jnp.exp2(s - m_new)
        alpha = jnp.exp2(m - m_new)
        return (m_new, alpha * l + jnp.sum(p, axis=-1, keepdims=True),
                alpha * acc + jnp.dot(p.astype(BF16), v, preferred_element_type=F32))

    def sel_step(kv0, carries, bias):
        kaug = jnp.concatenate([slc_ref[pl.ds(kv0, tk), :KV_W], oh_ref[pl.ds(kv0, tk), :]], axis=1)
        v = slc_ref[pl.ds(kv0, tk), KV_W:]
        out = []
        for g in range(NSA_KV):
            s = lax.dot_general(qaug[g], kaug, (((1,), (1,)), ((), ())), preferred_element_type=F32)
            out.append(fold(carries[g], s if bias is None else s + bias, v))
        return tuple(out)

    init = (jnp.full((nrow, 1), NEG, F32), jnp.zeros((nrow, 1), F32), jnp.zeros((nrow, LANES), F32))
    carries = lax.fori_loop(0, n_full, lambda it, c: sel_step(pl.multiple_of(it * tk, tk), c, None),
                            (init,) * NSA_KV)
    carries = sel_step(kv_diag, carries, cbias)
    kw = win_ref[pl.ds(w0, lw), :KV_W]
    vw = win_ref[pl.ds(w0, lw), KV_W:]
    for g in range(NSA_KV):
        m, l, acc = carries[g]
        o = acc / l
        _halves_to_heads(os_ref, [o[hl * tq:(hl + 1) * tq] for hl in range(HPG)], g)
        s = lax.dot_general(qst[g], kw, (((1,), (1,)), ((), ())), preferred_element_type=F32) + wbias
        e = jnp.exp2(s - jnp.max(s, axis=-1, keepdims=True))
        o = jnp.dot(e.astype(BF16), vw, preferred_element_type=F32) / jnp.sum(e, axis=-1, keepdims=True)
        _halves_to_heads(ow_ref, [o[hl * tq:(hl + 1) * tq] for hl in range(HPG)], g)
    _combine(y_ref, gn_ref, ge_ref, oc_ref[...], os_ref[...], ow_ref[...])


def _nsa_prompt(qr3, slc3, win3, sel3, oc3, gn3, tq, tk):
    b, t, _ = qr3.shape
    lw = min(WINDOW + tq, t)
    qblk = lambda w: pl.BlockSpec((None, tq, w), lambda i, j: (i, j, 0))
    full = pl.BlockSpec((None, t, KV_ROW), lambda i, j: (i, 0, 0))
    onehot = jnp.asarray(np.arange(t)[:, None] // L_SEL == np.arange(LANES)[None, :], BF16)
    return pl.pallas_call(
        functools.partial(_nsa_prompt_body, tq=tq, tk=tk, lw=lw),
        grid=(b, t // tq),
        in_specs=[qblk(NSA_W), full, full, pl.BlockSpec((t, LANES), lambda i, j: (0, 0)),
                  qblk(LANES), qblk(NSA_W), qblk(GN_PAD),
                  pl.BlockSpec((GN_PAD, 3 * NSA_W), lambda i, j: (0, 0))],
        out_specs=qblk(NSA_W),
        out_shape=jax.ShapeDtypeStruct((b, t, NSA_W), F32),
        scratch_shapes=[pltpu.VMEM((tq, NSA_W), F32), pltpu.VMEM((tq, NSA_W), F32)],
        compiler_params=_cparams(("parallel", "arbitrary")),
        name="nsa_attn_prompt",
    )(qr3, slc3, win3, onehot, sel3, oc3, gn3, _gate_expand())


def _nsa_sample_body(*refs, n_pages, nseq, past_len):
    page_refs = refs[1:1 + nseq * n_pages]
    (qr_ref, snew_ref, wold_ref, wnew_ref, sel_ref, oc_ref, gn_ref, ge_ref, y_ref, os_ref, ow_ref) = refs[1 + nseq * n_pages:]
    tq = qr_ref.shape[1]
    page = page_refs[0].shape[-1]
    wb = wold_ref.shape[-1]
    qpos = past_len + lax.broadcasted_iota(jnp.int32, (tq, 1), 0)

    def attend_all(jobs):
        scores = [[jnp.where(allow, _dot(q, kt), NEG) for kt, _, allow in cached]
                  + [jnp.where(allow, _dot_nt(q, k), NEG) for k, _, allow in fresh] for q, cached, fresh in jobs]
        masks = [[p[2] for p in cached + fresh] for _, cached, fresh in jobs]
        ms = [functools.reduce(jnp.maximum, [jnp.max(x, axis=-1, keepdims=True) for x in sc]) for sc in scores]
        es = [[jnp.where(allow, jnp.exp(x - m), 0.0) for x, allow in zip(sc, mk)]
              for sc, mk, m in zip(scores, masks, ms)]
        outs = []
        for e, (_, cached, fresh) in zip(es, jobs):
            l = sum(jnp.sum(x, axis=-1, keepdims=True) for x in e)
            acc = sum([_dot_nt(x, vt) for x, (_, vt, _) in zip(e, cached)]
                      + [_dot(x, v) for x, (_, v, _) in zip(e[len(cached):], fresh)])
            outs.append(acc / l)
        return outs

    def put_heads(o_ref, o, g):
        for pr in range(HPG // 2):
            pair = jnp.concatenate([o[2 * pr * tq:(2 * pr + 1) * tq], o[(2 * pr + 1) * tq:(2 * pr + 2) * tq]], axis=1)
            c = (g * HPG) // 2 + pr
            o_ref[:, c * LANES:(c + 1) * LANES] = pair

    pad = lambda x: jnp.concatenate([x, jnp.zeros((page - tq, KV_ROW), F32)], axis=0)
    kpos_new = past_len + lax.broadcasted_iota(jnp.int32, (tq, page), 1)
    causal_new = kpos_new <= qpos
    kpos_old = past_len - wb + lax.broadcasted_iota(jnp.int32, (tq, wb), 1)
    old_ok = _per_head((kpos_old > qpos - WINDOW) & (kpos_old >= 0))
    new_ok = _per_head(causal_new & (kpos_new > qpos - WINDOW))
    sel_jobs, win_jobs = [], []
    for sb in range(nseq):
        pages = page_refs[sb * n_pages:(sb + 1) * n_pages]
        sel = sel_ref[sb]
        snew, wnew = pad(snew_ref[sb]), pad(wnew_ref[sb])

        def head_q(h):
            c = qr_ref[sb, :, (h // 2) * LANES:(h // 2 + 1) * LANES]
            return (pltpu.roll(c, HEAD_DIM, 1) if h % 2 else c)[:, :HEAD_DIM]

        for g in range(NSA_KV):
            q = jnp.concatenate([head_q(g * HPG + hl) for hl in range(HPG)], axis=0) * HEAD_DIM ** -0.5
            kcol = slice(g * HEAD_DIM, (g + 1) * HEAD_DIM)
            vcol = slice(KV_W + g * HEAD_DIM, KV_W + (g + 1) * HEAD_DIM)
            cached = [(pages[k][0, g], pages[k][1, g], _per_head(_sel_mask_tile(sel, g, k * page, page) > 0.5))
                      for k in range(n_pages)]
            fresh = [(snew[:, kcol], snew[:, vcol],
                      _per_head((_sel_mask_tile(sel, g, past_len, page) > 0.5) & causal_new))]
            sel_jobs.append((q, cached, fresh))
            win_jobs.append((q, [(wold_ref[sb, 0, g], wold_ref[sb, 1, g], old_ok)],
                             [(wnew[:, kcol], wnew[:, vcol], new_ok)]))
    for o_ref, outs in ((os_ref, attend_all(sel_jobs)), (ow_ref, attend_all(win_jobs))):
        for j, o in enumerate(outs):
            put_heads(o_ref.at[j // NSA_KV], o, j % NSA_KV)
    for sb in range(nseq):
        _combine(y_ref.at[sb], gn_ref.at[sb], ge_ref, oc_ref[sb], os_ref[sb], ow_ref[sb])


ATTN_SEQS = 4


def _nsa_sample(qr3, slc_t, page_table, snew3, wold_t, wnew3, sel3, oc3, gn3):
    b, tq, _ = qr3.shape
    n_pages = page_table.shape[1]
    page = slc_t.shape[-1]
    wb = wold_t.shape[-1]
    nseq = next(d for d in range(ATTN_SEQS, 0, -1) if b % d == 0)
    per = lambda r, w: pl.BlockSpec((nseq, r, w), lambda i, pt: (i, 0, 0))
    pages = [_page_spec(page, functools.partial(lambda q, k, i, pt: (pt[i * nseq + q, k], 0, 0, 0, 0), q, k))
             for q in range(nseq) for k in range(n_pages)]
    return pl.pallas_call(
        functools.partial(_nsa_sample_body, n_pages=n_pages, nseq=nseq, past_len=n_pages * page),
        grid_spec=pltpu.PrefetchScalarGridSpec(
            num_scalar_prefetch=1, grid=(b // nseq,),
            in_specs=pages + [per(tq, NSA_W), per(tq, KV_ROW),
                              pl.BlockSpec((nseq, 2, NSA_KV, HEAD_DIM, wb), lambda i, pt: (i, 0, 0, 0, 0)),
                              per(tq, KV_ROW), per(tq, LANES), per(tq, NSA_W), per(tq, GN_PAD),
                              pl.BlockSpec((GN_PAD, 3 * NSA_W), lambda i, pt: (0, 0))],
            out_specs=per(tq, NSA_W),
            scratch_shapes=[pltpu.VMEM((nseq, tq, NSA_W), F32), pltpu.VMEM((nseq, tq, NSA_W), F32)]),
        out_shape=jax.ShapeDtypeStruct((b, tq, NSA_W), F32),
        compiler_params=_cparams(("arbitrary",)),
        name="nsa_attn_sample",
    )(page_table, *([slc_t] * (nseq * n_pages)), qr3, snew3, wold_t, wnew3, sel3, oc3, gn3, _gate_expand())


def _final_body(x_ref, yraw_ref, bonus_ref, g_ref, yn_ref, gates_ref, lnw_ref, lnb_ref, bd_ref,
                wbr_ref, wbn_ref, wo_ref, gffn_ref, wg_ref, wu_ref, wd_ref, gfin_ref, y_ref):
    bd = bd_ref[...]
    y = yraw_ref[...]
    d = y - _dot2(y, bd) * (1.0 / RWKV_HEAD)
    var = _dot2(d * d, bd) * (1.0 / RWKV_HEAD)
    y_r = (d * lax.rsqrt(var + RWKV_GN_EPS) * lnw_ref[...] + lnb_ref[...] + bonus_ref[...]) * g_ref[...]
    merged = (_sigmoid(gates_ref[:, :D_MODEL]) * _dot(y_r, wbr_ref[...])
              + _sigmoid(gates_ref[:, D_MODEL:]) * _dot(yn_ref[...], wbn_ref[...]))
    x1 = x_ref[...] + _dot(merged, wo_ref[...])
    h2 = x1 * lax.rsqrt(jnp.mean(x1 * x1, axis=-1, keepdims=True) + NORM_EPS) * gffn_ref[...]
    hb = h2.astype(BF16)
    up = _silu(jnp.dot(hb, wg_ref[...], preferred_element_type=F32)) * jnp.dot(hb, wu_ref[...], preferred_element_type=F32)
    x2 = x1 + _dot(up, wd_ref[...])
    y_ref[...] = x2 * lax.rsqrt(jnp.mean(x2 * x2, axis=-1, keepdims=True) + NORM_EPS) * gfin_ref[...]


def _final(x2d, yraw, bonus, g, yn, gates, prm, tm):
    m = x2d.shape[0]
    tm = min(tm, m)
    row = lambda w: pl.BlockSpec((tm, w), lambda i: (i, 0))
    return pl.pallas_call(
        _final_body,
        grid=(m // tm,),
        in_specs=[row(D_MODEL), row(RWKV_W), row(RWKV_W), row(RWKV_W), row(NSA_W), row(2 * D_MODEL),
                  _const_spec((1, RWKV_W)), _const_spec((1, RWKV_W)), _const_spec((RWKV_W, RWKV_W)),
                  _const_spec((RWKV_W, D_MODEL)), _const_spec((NSA_W, D_MODEL)),
                  _const_spec((D_MODEL, D_MODEL)), _const_spec((1, D_MODEL)),
                  _const_spec((D_MODEL, D_FF)), _const_spec((D_MODEL, D_FF)), _const_spec((D_FF, D_MODEL)),
                  _const_spec((1, D_MODEL))],
        out_specs=row(D_MODEL),
        out_shape=jax.ShapeDtypeStruct((m, D_MODEL), F32),
        compiler_params=_cparams(("parallel",)),
        name="merge_ffn",
    )(x2d, yraw, bonus, g, yn, gates, prm["ln_w"], prm["ln_b"], prm["bd"], prm["w_br_rwkv"],
      prm["w_br_nsa"], prm["w_out"], prm["g_ffn"], prm["w_ffn_gate"], prm["w_ffn_up"],
      prm["w_ffn_down"], prm["g_final"])


_S = (0, RWKV_W, RWKV_W + DECAY_RANK, 2 * RWKV_W + DECAY_RANK, 3 * RWKV_W + DECAY_RANK,
      3 * RWKV_W + DECAY_RANK + AAA_RANK, RWKV_COLS)
_RW_ORDER = (0, 2, 3, 1, 4, 5)


def _rw_regroup(x):
    return jnp.concatenate([x[..., _S[i]:_S[i + 1]] for i in _RW_ORDER], axis=-1)


def _rw_ungroup(y):
    starts = np.cumsum([0] + [_S[i + 1] - _S[i] for i in _RW_ORDER])
    where = {seg: k for k, seg in enumerate(_RW_ORDER)}
    return jnp.concatenate([y[..., starts[where[i]]:starts[where[i] + 1]] for i in range(len(_RW_ORDER))], axis=-1)


def _prep_params(l, g_mix, w_in, rwkv_mu, rwkv_w0, rwkv_w_decay, rwkv_a0, rwkv_w_aaa, rwkv_w_gate,
                 rwkv_k_k, rwkv_k_a, rwkv_r_k, rwkv_ln_w, rwkv_ln_b, w_br_rwkv, w_br_nsa, w_out, g_ffn,
                 w_ffn_gate, w_ffn_up, w_ffn_down, g_final):
    w = w_in[l]
    nsa0 = RWKV_COLS
    kv_end = nsa0 + NSA_W + 6 * KV_W
    zc = lambda n: jnp.zeros((D_MODEL, n), F32)
    w_r = jnp.concatenate([_rw_regroup(w[:, :RWKV_COLS]), zc(RW_PAD - RWKV_COLS), w[:, nsa0:kv_end],
                           w[:, kv_end:nsa0 + NSA_COLS], zc(GN_PAD - 3 * NSA_HEADS),
                           w[:, nsa0 + NSA_COLS:]], axis=1).astype(BF16)
    mu = jnp.concatenate([_rw_regroup(rwkv_mu[l]), jnp.zeros((RW_PAD - RWKV_COLS,), F32)])[None]
    wda = jnp.zeros((LANES, 2 * RWKV_W), F32)
    wda = wda.at[:DECAY_RANK, :RWKV_W].set(rwkv_w_decay[l]).at[DECAY_RANK:, RWKV_W:].set(rwkv_w_aaa[l])
    wg = jnp.zeros((RW_PAD - 3 * RWKV_W - LANES, RWKV_W), F32).at[:GATE_RANK].set(rwkv_w_gate[l])
    hid = np.arange(RWKV_W) // RWKV_HEAD
    bd = jnp.asarray(hid[:, None] == hid[None, :], BF16)
    r1 = lambda a: a.reshape(1, -1)
    return dict(
        g_mix=r1(g_mix[l]), w_r=w_r, mu=mu, wda=wda.astype(BF16), w0=r1(rwkv_w0[l]), a0=r1(rwkv_a0[l]),
        wg=wg.astype(BF16), k_k=r1(rwkv_k_k[l]), k_a=r1(rwkv_k_a[l]), r_k=r1(rwkv_r_k[l]), bd=bd,
        ln_w=r1(rwkv_ln_w[l]), ln_b=r1(rwkv_ln_b[l]), w_br_rwkv=w_br_rwkv[l].astype(BF16),
        w_br_nsa=w_br_nsa[l].astype(BF16), w_out=w_out[l].astype(BF16), g_ffn=r1(g_ffn[l]),
        w_ffn_gate=w_ffn_gate[l].astype(BF16), w_ffn_up=w_ffn_up[l].astype(BF16),
        w_ffn_down=w_ffn_down[l].astype(BF16), g_final=r1(g_final))


def _rope_tables(pos):
    half = ROT_DIM // 2
    inv = ROPE_THETA ** (-jnp.arange(half, dtype=F32) / half)
    ang = pos.astype(F32)[:, None] * inv[None, :]
    cos, sin = jnp.cos(ang), jnp.sin(ang)
    n = pos.shape[0]
    one, zero = jnp.ones((n, HEAD_DIM - ROT_DIM), F32), jnp.zeros((n, HEAD_DIM - ROT_DIM), F32)
    z8 = jnp.zeros((n, half), F32)
    c = jnp.concatenate([cos, cos, one], 1)
    s1 = jnp.concatenate([-sin, z8, zero], 1)
    s2 = jnp.concatenate([z8, sin, zero], 1)
    return tuple(jnp.concatenate([a, a], 1) for a in (c, s1, s2))


def _group(x, pos, shift0, s0, prm, cw, past, tm, pre_blk, tq):
    b, t, _ = x.shape
    m = b * t
    x2d = x.reshape(m, D_MODEL)
    p_rw, q, q_r, cmp_rows, slc_rows, win_rows, gn, gates, slc_bf, win_bf = _in_proj(
        x2d, prm["g_mix"], prm["w_r"], pos, tm)

    shift_r = jnp.concatenate([_rw_regroup(shift0), jnp.zeros((b, RW_PAD - RWKV_COLS), F32)], 1)[:, None, :]
    *scan_in, v, bonus, g = _rwkv_pre(p_rw.reshape(b, t, RW_PAD), shift_r, prm, *pre_blk)
    y_raw, st = _rwkv_chunk_scan(scan_in, v, _state_to_blockdiag(s0), t)
    s_new = _blockdiag_to_state(st)
    shift_new = _rw_ungroup(p_rw.reshape(b, t, RW_PAD)[:, -1, :RWKV_COLS])

    r3 = lambda a: a.reshape(b, t, a.shape[-1])
    rows_last = lambda c: jnp.moveaxis(c, 1, -1)
    rows_first = lambda c: jnp.moveaxis(c, -1, 1)
    if past is None:
        kcv = _compress_seq(cmp_rows, cw)
        n_c = t // D_CMP - 1
        oc, sel = _cmp_attn(r3(q), kcv, tq, 0, n_c)
        y_n = _nsa_prompt(r3(q_r), r3(slc_bf), r3(win_bf), sel, oc, r3(gn), min(2 * tq, t), min(512, t))
        new_cmp, new_slc = rows_first(cmp_rows), rows_first(slc_rows)
        win_state = rows_first(win_rows[..., t - min(WINDOW, t):])
    else:
        cmp_cache, slc_cache, win_cache, page_table = past
        past_len = page_table.shape[1] * cmp_cache.shape[1]
        kcv = _compress_paged(rows_last(cmp_cache), page_table, cw)
        n_c = (past_len + t) // D_CMP - 1
        oc, sel = _cmp_attn(r3(q), kcv, tq, past_len, n_c)
        y_n = _nsa_sample(r3(q_r), rows_last(slc_cache), page_table, r3(slc_rows), rows_last(win_cache),
                          r3(win_rows), sel, oc, r3(gn))
        kv6 = lambda a: a.reshape(b, t, 2, NSA_KV, HEAD_DIM)
        new_cmp, new_slc = kv6(cmp_rows), kv6(slc_rows)
        win_all = jnp.concatenate([win_cache, kv6(win_rows)], axis=1)
        win_state = win_all[:, win_all.shape[1] - min(WINDOW, win_all.shape[1]):]

    y = _final(x2d, y_raw, bonus, g, y_n.reshape(m, NSA_W), gates, prm, tm)
    return (y.reshape(b, t, D_MODEL), new_cmp, new_slc, win_state, s_new, shift_new)


def kernel(x_prompt, x_sample, cache_cmp_kv, cache_slc_kv, cache_win_kv, state_rwkv, state_rwkv_shift, page_table, g_mix, w_in, rwkv_mu, rwkv_w0, rwkv_w_decay, rwkv_a0, rwkv_w_aaa, rwkv_w_gate, rwkv_k_k, rwkv_k_a, rwkv_r_k, rwkv_ln_w, rwkv_ln_b, nsa_pe_cmp, nsa_w_cmp1, nsa_w_cmp2, w_br_rwkv, w_br_nsa, w_out, g_ffn, w_ffn_gate, w_ffn_up, w_ffn_down, g_final):
    depth = w_in.shape[0]
    assert depth == 1, "single-layer trunk"
    b, t, _ = x_prompt.shape
    bs, ts, _ = x_sample.shape
    n_pool, page = cache_cmp_kv.shape[1:3]
    past_len = page_table.shape[1] * page
    assert (past_len + ts) // D_CMP == past_len // D_CMP and past_len % D_CMP == 0
    prm = _prep_params(0, g_mix, w_in, rwkv_mu, rwkv_w0, rwkv_w_decay, rwkv_a0, rwkv_w_aaa, rwkv_w_gate,
                       rwkv_k_k, rwkv_k_a, rwkv_r_k, rwkv_ln_w, rwkv_ln_b, w_br_rwkv, w_br_nsa, w_out,
                       g_ffn, w_ffn_gate, w_ffn_up, w_ffn_down, g_final)
    cw = _compress_weights(nsa_pe_cmp[0], nsa_w_cmp1[0], nsa_w_cmp2[0])

    outs_p = _group(x_prompt, jnp.arange(t), jnp.zeros((b, RWKV_COLS), F32),
                    jnp.zeros((b, RWKV_HEADS, RWKV_HEAD, RWKV_HEAD), F32), prm, cw, None,
                    tm=256, pre_blk=(1, min(512, t)), tq=min(128, t))
    outs_s = _group(x_sample, past_len + jnp.arange(ts), state_rwkv_shift[0], state_rwkv[0], prm, cw,
                    (cache_cmp_kv[0], cache_slc_kv[0], cache_win_kv[0], page_table),
                    tm=256, pre_blk=(min(64, bs), ts), tq=ts)
    return (outs_p[0], outs_s[0]) + tuple(o[None] for o in outs_p[1:]) + tuple(o[None] for o in outs_s[1:])
```

```python
import functools

import jax
import jax.numpy as jnp
import numpy as np
from jax import lax
from jax.experimental import pallas as pl
from jax.experimental.pallas import tpu as pltpu

F32 = jnp.float32
BF16 = jnp.bfloat16

D_MODEL = 1024
RWKV_HEADS = 8
RWKV_HEAD = 64
RWKV_W = RWKV_HEADS * RWKV_HEAD
DECAY_RANK = 64
AAA_RANK = 64
GATE_RANK = 160
RWKV_GN_EPS = 64e-5
NSA_HEADS = 8
NSA_KV = 2
HEAD_DIM = 64
HPG = NSA_HEADS // NSA_KV
NSA_W = NSA_HEADS * HEAD_DIM
KV_W = NSA_KV * HEAD_DIM
L_CMP = 32
D_CMP = 16
CMP_HID = 128
L_SEL = 64
N_SEL = 16
WINDOW = 512
ROT_DIM = HEAD_DIM // 4
ROPE_THETA = 500000.0
D_FF = -(-8 * D_MODEL // (3 * 256)) * 256
NORM_EPS = 1e-6
NEG = -1e30
FORCE = 1e6
RWKV_COLS = 3 * RWKV_W + DECAY_RANK + AAA_RANK + GATE_RANK
NSA_COLS = NSA_W + 6 * KV_W + 3 * NSA_HEADS

LANES = 128
RW_PAD = 1920
GN_PAD = LANES
KV_ROW = 2 * KV_W
NBLK_PAD = 64
VMEM_LIMIT = 56 * 1024 * 1024


def _cparams(sem):
    return pltpu.CompilerParams(dimension_semantics=sem, vmem_limit_bytes=VMEM_LIMIT)


def _const_spec(shape):
    nd = len(shape)
    return pl.BlockSpec(shape, lambda *_: (0,) * nd, pipeline_mode=pl.Buffered(1))


def _dot(a, b):
    return jnp.dot(a.astype(BF16), b.astype(BF16), preferred_element_type=F32)


def _dot_nt(a, b):
    return lax.dot_general(a.astype(BF16), b.astype(BF16), (((1,), (1,)), ((), ())),
                           preferred_element_type=F32)


def _dot2(a, b):
    hi = a.astype(BF16)
    lo = (a - hi.astype(F32)).astype(BF16)
    return (jnp.dot(hi, b, preferred_element_type=F32)
            + jnp.dot(lo, b, preferred_element_type=F32))


def _sigmoid(x):
    return 1.0 / (1.0 + jnp.exp(-x))


def _silu(x):
    return x * _sigmoid(x)


def _rope128(x, cos, s1, s2):
    half = ROT_DIM // 2
    return x * cos + pltpu.roll(x, LANES - half, 1) * s1 + pltpu.roll(x, half, 1) * s2


_C_Q = RW_PAD
_C_CMP = _C_Q + NSA_W
_C_SLC = _C_CMP + KV_ROW
_C_WIN = _C_SLC + KV_ROW
_C_GN = _C_WIN + KV_ROW
_C_GATE = _C_GN + GN_PAD
_C_END = _C_GATE + 2 * D_MODEL


def _proj_body(x_ref, g_ref, w_ref, cos_ref, s1_ref, s2_ref,
               prw_ref, q_ref, qr_ref, cmp_ref, slc_ref, win_ref, gn_ref, gates_ref, slcb_ref, winb_ref,
               *, rows_last):
    x = x_ref[...]
    ms = jnp.mean(x * x, axis=-1, keepdims=True)
    h = (x * lax.rsqrt(ms + NORM_EPS) * g_ref[...]).astype(BF16)

    def mm(a, b):
        return jnp.dot(h, w_ref[:, a:b], preferred_element_type=F32)

    def put_rows(ref, kv):
        ref[...] = kv.T.reshape(ref.shape) if rows_last else kv

    cos, s1, s2 = cos_ref[...], s1_ref[...], s2_ref[...]
    prw_ref[...] = mm(0, _C_Q)
    q = mm(_C_Q, _C_CMP)
    q_ref[...] = q
    for c in range(NSA_W // LANES):
        sl = slice(c * LANES, (c + 1) * LANES)
        qr_ref[:, sl] = _rope128(q[:, sl], cos, s1, s2)
    put_rows(cmp_ref, mm(_C_CMP, _C_SLC))
    for lo, hi, f_ref, h_ref in ((_C_SLC, _C_WIN, slc_ref, slcb_ref), (_C_WIN, _C_GN, win_ref, winb_ref)):
        kv = mm(lo, hi)
        kv = jnp.concatenate([_rope128(kv[:, :KV_W], cos, s1, s2), kv[:, KV_W:]], axis=1)
        put_rows(f_ref, kv)
        h_ref[...] = kv.astype(BF16)
    gn_ref[...] = mm(_C_GN, _C_GATE)
    gates_ref[...] = mm(_C_GATE, _C_END)


def _in_proj(x2d, g_mix, w_r, pos, tm):
    m = x2d.shape[0]
    t = pos.shape[0]
    tm = min(tm, m)
    period = max(t, tm) // tm
    tabs = _rope_tables(jnp.tile(pos, max(1, tm // t)))
    row = lambda w: pl.BlockSpec((tm, w), lambda i: (i, 0))
    tab = pl.BlockSpec((tm, LANES), lambda i: (i % period, 0))
    f32 = lambda w: (row(w), jax.ShapeDtypeStruct((m, w), F32))
    rows_last = t % tm == 0
    if rows_last:
        kv = (pl.BlockSpec((None, 2, NSA_KV, HEAD_DIM, tm), lambda i: (i // period, 0, 0, 0, i % period)),
              jax.ShapeDtypeStruct((m // t, 2, NSA_KV, HEAD_DIM, t), F32))
    else:
        kv = f32(KV_ROW)
    bf = (row(KV_ROW), jax.ShapeDtypeStruct((m, KV_ROW), BF16))
    outs = [f32(RW_PAD), f32(NSA_W), f32(NSA_W), kv, kv, kv, f32(GN_PAD), f32(2 * D_MODEL), bf, bf]
    return pl.pallas_call(
        functools.partial(_proj_body, rows_last=rows_last),
        grid=(m // tm,),
        in_specs=[row(D_MODEL), _const_spec((1, D_MODEL)), _const_spec((D_MODEL, _C_END)),
                  tab, tab, tab],
        out_specs=[o[0] for o in outs],
        out_shape=[o[1] for o in outs],
        compiler_params=_cparams(("parallel",)),
        name="in_proj",
    )(x2d, g_mix, w_r, *tabs)


def _rwkv_pre_body(p_ref, sh_ref, mu_ref, wda_ref, w0_ref, a0_ref, wg_ref, kk_ref, ka_ref,
                   rk_ref, bd_ref, r_ref, lw_ref, k_ref, a_ref, b_ref, v_ref, bonus_ref,
                   g_ref, carry_ref):
    j = pl.program_id(1)
    bb, tt, c = p_ref.shape
    n = bb * tt
    x3 = p_ref[...]
    x = x3.reshape(n, c)
    prev = jnp.where(j == 0, sh_ref[...], carry_ref[...])
    carry_ref[...] = x3[:, tt - 1:tt, :]
    prev_rows = jnp.broadcast_to(prev, (bb, tt, c)).reshape(n, c)
    row = lax.broadcasted_iota(jnp.int32, (n, c), 0)
    shifted = jnp.where(row % tt == 0, prev_rows, pltpu.roll(x, 1, 0))
    xs = x + (shifted - x) * mu_ref[...]
    r = xs[:, 0:RWKV_W]
    k = xs[:, RWKV_W:2 * RWKV_W]
    v = xs[:, 2 * RWKV_W:3 * RWKV_W]
    wa = xs[:, 3 * RWKV_W:3 * RWKV_W + LANES]
    gl = xs[:, 3 * RWKV_W + LANES:RW_PAD]
    lane = lax.broadcasted_iota(jnp.int32, wa.shape, 1)
    z = _dot(jnp.where(lane < DECAY_RANK, jnp.tanh(wa), wa), wda_ref[...])
    u = -(w0_ref[...] + z[:, :RWKV_W])
    softplus = jnp.maximum(u, 0.0) + jnp.log1p(jnp.exp(-jnp.abs(u)))
    log_decay = -jnp.exp(-softplus - 0.5)
    a = _sigmoid(a0_ref[...] + z[:, RWKV_W:])
    g_ref[...] = _dot(_sigmoid(gl), wg_ref[...])
    bd = bd_ref[...]
    kk = k * kk_ref[...]
    kk = kk * lax.rsqrt(jnp.maximum(_dot2(kk * kk, bd), 1e-24))
    k_h = k * (1.0 + (a - 1.0) * ka_ref[...])
    bonus_ref[...] = _dot2(r * k_h * rk_ref[...], bd) * v
    v_ref[...] = v
    r_ref[...] = r
    lw_ref[...] = log_decay
    k_ref[...] = k_h
    a_ref[...] = -kk
    b_ref[...] = kk * a


def _rwkv_pre(p3, shift0, prm, bb, tt):
    b, t, _ = p3.shape
    m = b * t
    n = bb * tt
    nt = t // tt
    rowm = pl.BlockSpec((n, RWKV_W), lambda i, j: (i * nt + j, 0))
    outs = pl.pallas_call(
        _rwkv_pre_body,
        grid=(b // bb, nt),
        in_specs=[pl.BlockSpec((bb, tt, RW_PAD), lambda i, j: (i, j, 0)),
                  pl.BlockSpec((bb, 1, RW_PAD), lambda i, j: (i, 0, 0)),
                  _const_spec((1, RW_PAD)), _const_spec((LANES, 2 * RWKV_W)),
                  _const_spec((1, RWKV_W)), _const_spec((1, RWKV_W)),
                  _const_spec((RW_PAD - 3 * RWKV_W - LANES, RWKV_W)),
                  _const_spec((1, RWKV_W)), _const_spec((1, RWKV_W)), _const_spec((1, RWKV_W)),
                  _const_spec((RWKV_W, RWKV_W))],
        out_specs=[rowm] * 8,
        out_shape=[jax.ShapeDtypeStruct((m, RWKV_W), F32)] * 8,
        scratch_shapes=[pltpu.VMEM((bb, 1, RW_PAD), F32)],
        compiler_params=_cparams(("parallel", "arbitrary")),
        name="rwkv_pre",
    )(p3, shift0, prm["mu"], prm["wda"], prm["w0"], prm["a0"], prm["wg"], prm["k_k"],
      prm["k_a"], prm["r_k"], prm["bd"])
    return outs


RWKV_CHUNK = 64


def _split3(x):
    x1 = x.astype(BF16)
    r1 = x - x1.astype(F32)
    x2 = r1.astype(BF16)
    return x1, x2, (r1 - x2.astype(F32)).astype(BF16)


def _chunk_scan_body(r_ref, lw_ref, k_ref, a_ref, b_ref, v_ref, s0_ref, tri_ref, y_ref, s_ref):
    nbat, c = r_ref.shape[0], r_ref.shape[1]
    n = 2 * c
    first = pl.program_id(1) == 0
    left = lax.broadcasted_iota(jnp.int32, (c, LANES), 1) < RWKV_HEAD
    stack = lambda x: jnp.concatenate([jnp.where(left, x, 0.0), jnp.where(left, 0.0, x)], axis=0)
    ri = lax.broadcasted_iota(jnp.int32, (n, n), 0)
    ci = lax.broadcasted_iota(jnp.int32, (n, n), 1)
    strict = ((ri < c) == (ci < c)) & (ci < ri)
    eye = jnp.where(ri == ci, 1.0, 0.0)
    ri2 = lax.broadcasted_iota(jnp.int32, (n, 2 * n), 0)
    ci2 = lax.broadcasted_iota(jnp.int32, (n, 2 * n), 1) % n
    incl2 = ((ri2 < c) == (ci2 < c)) & (ci2 <= ri2)
    tri = tri_ref[...]
    chains = [(i, p) for i in range(nbat) for p in range(RWKV_HEADS // 2)]
    pre = []
    for i, p in chains:
        sl = slice(p * LANES, (p + 1) * LANES)
        lw = lw_ref[i, :, sl]
        cs = sum(jnp.dot(tri, part, preferred_element_type=F32) for part in _split3(lw))
        c_end = cs[c - 1:c, :]
        e_neg = jnp.exp(-cs)
        e_hat = jnp.exp(c_end - cs)
        at = stack(a_ref[i, :, sl] * jnp.exp(cs - lw))
        rt = stack(r_ref[i, :, sl] * jnp.exp(cs))
        bt = stack(b_ref[i, :, sl] * e_neg)
        kt = stack(k_ref[i, :, sl] * e_neg)
        bk_hat = jnp.concatenate([stack(b_ref[i, :, sl] * e_hat), stack(k_ref[i, :, sl] * e_hat)], axis=0)
        g = _dot_nt(jnp.concatenate([at, rt], axis=0), jnp.concatenate([bt, kt], axis=0))
        pre.append(dict(sl=sl, e_end=jnp.exp(c_end), at=at, rt=rt, bk_hat=bk_hat.astype(BF16),
                        vs=stack(v_ref[i, :, sl]), low=jnp.where(strict, g[:n, :n], 0.0),
                        g_ak=jnp.where(strict, g[:n, n:], 0.0),
                        g_r=jnp.where(incl2, g[n:, :], 0.0)))
    tinv = [eye + q["low"] for q in pre]
    lp = [q["low"] for q in pre]
    for _ in range(c.bit_length() - 2):
        lp = [_dot(x, x) for x in lp]
        tinv = [t + _dot(t, x) for t, x in zip(tinv, lp)]
    state = [jnp.where(first, s0_ref[i, p], s_ref[i, p]) for i, p in chains]
    rhs = [_dot_nt(q["at"], s) + _dot(q["g_ak"], q["vs"]) for q, s in zip(pre, state)]
    uv = [jnp.concatenate([_dot(t, x), q["vs"]], axis=0) for t, x, q in zip(tinv, rhs, pre)]
    for (i, p), q, s, w in zip(chains, pre, state, uv):
        ys = _dot_nt(q["rt"], s) + _dot(q["g_r"], w)
        y_ref[i, :, q["sl"]] = ys[:c] + ys[c:]
        s_ref[i, p] = s * q["e_end"] + lax.dot_general(
            w.astype(BF16), q["bk_hat"], (((0,), (0,)), ((), ())), preferred_element_type=F32)


SCAN_SEQS = 8


def _rwkv_chunk_scan(rows, v, s0pair, t):
    m = v.shape[0]
    b = s0pair.shape[0]
    c = min(RWKV_CHUNK, t)
    nbat = next(d for d in range(SCAN_SEQS, 0, -1) if b % d == 0)
    as3 = lambda x: x.reshape(b, t, RWKV_W)
    rowm = pl.BlockSpec((nbat, c, RWKV_W), lambda i, j: (i, j, 0))
    sspec = pl.BlockSpec((nbat, RWKV_HEADS // 2, LANES, LANES), lambda i, j: (i, 0, 0, 0))
    tri = jnp.asarray(np.tril(np.ones((c, c), np.float32)), BF16)
    y, st = pl.pallas_call(
        _chunk_scan_body,
        grid=(b // nbat, t // c),
        in_specs=[rowm] * 6 + [sspec, pl.BlockSpec((c, c), lambda i, j: (0, 0))],
        out_specs=[rowm, sspec],
        out_shape=[jax.ShapeDtypeStruct((b, t, RWKV_W), F32), jax.ShapeDtypeStruct(s0pair.shape, F32)],
        compiler_params=_cparams(("parallel", "arbitrary")),
        name="rwkv_chunk_scan",
    )(*[as3(x) for x in rows], as3(v), s0pair, tri)
    return y.reshape(m, RWKV_W), st


def _state_to_blockdiag(s):
    b = s.shape[0]
    sp = s.reshape(b, RWKV_HEADS // 2, 2, RWKV_HEAD, RWKV_HEAD)
    z = jnp.zeros_like(sp[:, :, 0])
    top = jnp.concatenate([sp[:, :, 0], z], axis=-1)
    bot = jnp.concatenate([z, sp[:, :, 1]], axis=-1)
    return jnp.concatenate([top, bot], axis=-2)


def _blockdiag_to_state(sp):
    b = sp.shape[0]
    h0 = sp[:, :, :RWKV_HEAD, :RWKV_HEAD]
    h1 = sp[:, :, RWKV_HEAD:, RWKV_HEAD:]
    return jnp.stack([h0, h1], axis=2).reshape(b, RWKV_HEADS, RWKV_HEAD, RWKV_HEAD)


def _compress_body(*refs, n_src):
    pe_ref, w1_ref, w2_ref, o_ref, xs_ref = refs[-5:]
    pages = refs[-5 - n_src:-5]
    page = pages[0].shape[-1]
    for k, pg in enumerate(pages):
        for c in range(2):
            xs_ref[c, k * page:(k + 1) * page, :] = pg[c].reshape(KV_W, page).T
    per = n_src * page // D_CMP
    nseq, ncs = o_ref.shape[0], o_ref.shape[1]
    for c in range(2):
        rows = lambda l: xs_ref[c, pl.ds(l, per, stride=D_CMP), :]
        for j in range(D_CMP // 2):
            xl = jnp.concatenate([rows(2 * j), rows(2 * j + 1)], axis=1)
            f = _dot(xl + pe_ref[c, j:j + 1, :], w1_ref[c, j])
            s = _dot(xl + pe_ref[c, D_CMP // 2 + j:D_CMP // 2 + j + 1, :], w1_ref[c, D_CMP // 2 + j])
            hf = f if j == 0 else hf + f
            hs = s if j == 0 else hs + s
        nc = hf.shape[0]
        out = _dot(_silu(hf + pltpu.roll(hs, nc - 1, 0)), w2_ref[c])
        for q in range(nseq):
            o_ref[q, :, c * KV_W:(c + 1) * KV_W] = out[q * ncs:(q + 1) * ncs]


def _compress_weights(pe, w1, w2):
    eye = jnp.eye(NSA_KV, dtype=F32)
    w1b = jnp.einsum("cldh,gy->clgdyh", w1, eye).reshape(2, L_CMP, KV_W, NSA_KV * CMP_HID)
    w2b = jnp.einsum("chd,gy->cghyd", w2, eye).reshape(2, NSA_KV * CMP_HID, KV_W)
    peb = jnp.broadcast_to(pe[:, :, None, :], (2, L_CMP, NSA_KV, HEAD_DIM)).reshape(2, L_CMP, KV_W)
    return (peb.reshape(2, L_CMP // 2, 2 * KV_W),
            w1b.reshape(2, L_CMP // 2, 2 * KV_W, NSA_KV * CMP_HID).astype(BF16), w2b.astype(BF16))


CMP_SEQS = 4


def _page_spec(page, index_map):
    return pl.BlockSpec((None, 2, NSA_KV, HEAD_DIM, page), index_map)


def _compress_call(rows_t, page_specs, nb, nseq, nc, cw, prefetch):
    hid = NSA_KV * CMP_HID
    n_src = len(page_specs)
    page = page_specs[0].block_shape[-1]
    wspecs = [pl.BlockSpec((2, L_CMP // 2, 2 * KV_W), lambda *_: (0, 0, 0)),
              pl.BlockSpec((2, L_CMP // 2, 2 * KV_W, hid), lambda *_: (0, 0, 0, 0)),
              pl.BlockSpec((2, hid, KV_W), lambda *_: (0, 0, 0))]
    return pl.pallas_call(
        functools.partial(_compress_body, n_src=n_src),
        grid_spec=pltpu.PrefetchScalarGridSpec(
            num_scalar_prefetch=len(prefetch), grid=(nb // nseq,), in_specs=page_specs + wspecs,
            out_specs=pl.BlockSpec((nseq, nc, KV_ROW), lambda b, *_: (b, 0, 0)),
            scratch_shapes=[pltpu.VMEM((2, n_src * page, KV_W), F32)]),
        out_shape=jax.ShapeDtypeStruct((nb, nc, KV_ROW), F32),
        compiler_params=_cparams(("parallel",)),
        name="nsa_compress",
    )(*prefetch, *([rows_t] * n_src), *cw)


def _compress_seq(rows_t, cw):
    b, t = rows_t.shape[0], rows_t.shape[-1]
    specs = [_page_spec(LANES, functools.partial(lambda k, i: (i, 0, 0, 0, k), k)) for k in range(t // LANES)]
    return _compress_call(rows_t, specs, b, 1, t // D_CMP, cw, ())


def _compress_paged(cache_t, page_table, cw):
    b, n_pages = page_table.shape
    page = cache_t.shape[-1]
    nseq = CMP_SEQS if b % CMP_SEQS == 0 else 1
    specs = [_page_spec(page, functools.partial(lambda q, k, i, pt: (pt[i * nseq + q, k], 0, 0, 0, 0), q, k))
             for q in range(nseq) for k in range(n_pages)]
    return _compress_call(cache_t, specs, b, nseq, n_pages * page // D_CMP, cw, (page_table,))


def _head_to_half(q_ref, h, g):
    c = q_ref[:, (h // 2) * LANES:(h // 2 + 1) * LANES]
    if h % 2 != g:
        c = pltpu.roll(c, HEAD_DIM, 1)
    lane = lax.broadcasted_iota(jnp.int32, c.shape, 1)
    return jnp.where((lane >= g * HEAD_DIM) & (lane < (g + 1) * HEAD_DIM), c, 0.0)


def _halves_to_heads(o_ref, outs, g):
    for pr in range(HPG // 2):
        tiles = []
        for hl in (2 * pr, 2 * pr + 1):
            h = g * HPG + hl
            o = outs[hl]
            if h % 2 != g:
                o = pltpu.roll(o, HEAD_DIM, 1)
            tiles.append(o)
        lane = lax.broadcasted_iota(jnp.int32, tiles[0].shape, 1)
        c = (g * HPG) // 2 + pr
        o_ref[:, c * LANES:(c + 1) * LANES] = jnp.where(lane < HEAD_DIM, tiles[0], tiles[1])


def _cmp_attn_body(q_ref, kcv_ref, ov_ref, oc_ref, sel_ref, *, tq, pos0, n_c):
    qi = pl.program_id(1)
    nb, ncp = kcv_ref.shape[0], kcv_ref.shape[1]
    n = nb * tq
    row = lax.broadcasted_iota(jnp.int32, (tq, ncp), 0)
    col = lax.broadcasted_iota(jnp.int32, (tq, ncp), 1)
    mask = _per_head((col * D_CMP + (L_CMP - 1) <= pos0 + qi * tq + row) & (col < n_c))
    imps = []
    for sb in range(nb):
        kc = kcv_ref[sb, :, :KV_W]
        vc = kcv_ref[sb, :, KV_W:]
        imp = None
        for g in range(NSA_KV):
            q4 = jnp.concatenate([_head_to_half(q_ref.at[sb], g * HPG + hl, g) for hl in range(HPG)], axis=0)
            s = jnp.where(mask, _dot_nt(q4, kc) * HEAD_DIM ** -0.5, NEG)
            e = jnp.exp(s - jnp.max(s, axis=-1, keepdims=True))
            p = jnp.where(mask, e / jnp.sum(e, axis=-1, keepdims=True), 0.0)
            o = _dot(p, vc)
            _halves_to_heads(oc_ref.at[sb], [o[hl * tq:(hl + 1) * tq] for hl in range(HPG)], g)
            psum = sum(p[hl * tq:(hl + 1) * tq] for hl in range(HPG))
            ig = _dot2(psum, ov_ref[g])
            imp = ig if imp is None else imp + ig
        imps.append(imp)
    imp_t = jnp.concatenate(imps, axis=0).T
    jb = lax.broadcasted_iota(jnp.int32, (LANES, n), 0) % NBLK_PAD
    cur = (pos0 + qi * tq + lax.broadcasted_iota(jnp.int32, (LANES, n), 1) % tq) // L_SEL
    forced = (jb == 0) | (jb == cur) | (jb == cur - 1)
    score = jnp.where(forced, FORCE, jnp.where(jb <= cur, imp_t, -1.0))
    sub = lax.broadcasted_iota(jnp.int32, (8, n), 0)
    ranks = []
    for g in range(NSA_KV):
        blocks = [score[g * NBLK_PAD + 8 * k:g * NBLK_PAD + 8 * k + 8, :] for k in range(NBLK_PAD // 8)]
        cnt = [jnp.zeros((8, n), F32) for _ in blocks]
        for jp in range(NBLK_PAD):
            other = jnp.broadcast_to(score[g * NBLK_PAD + jp:g * NBLK_PAD + jp + 1, :], (8, n))
            for k, blk in enumerate(blocks):
                if 8 * k > jp:
                    ahead = other >= blk
                elif 8 * k + 7 <= jp:
                    ahead = other > blk
                else:
                    ahead = (other > blk) | ((other == blk) & (sub > jp - 8 * k))
                cnt[k] = cnt[k] + jnp.where(ahead, 1.0, 0.0)
        ranks += cnt
    rank = jnp.concatenate(ranks, axis=0)
    sel = jnp.where((rank < N_SEL) & (jb <= cur), 1.0, 0.0).T
    for sb in range(nb):
        sel_ref[sb] = sel[sb * tq:(sb + 1) * tq]


def _overlap(ncp, n_c):
    ci = np.arange(ncp)[:, None]
    sj = np.arange(NBLK_PAD)[None, :]
    ov = ((ci * D_CMP < (sj + 1) * L_SEL) & (ci * D_CMP + L_CMP > sj * L_SEL) & (ci < n_c)).astype(np.float32)
    z = np.zeros_like(ov)
    return jnp.asarray(np.stack([np.concatenate([ov, z], 1), np.concatenate([z, ov], 1)]), BF16)


def _cmp_attn(q3, kcv, tq, pos0, n_c):
    b, t, _ = q3.shape
    ncp = kcv.shape[1]
    nb = LANES // tq
    blk = lambda r, w: pl.BlockSpec((nb, r, w), lambda i, j: (i, j, 0))
    return pl.pallas_call(
        functools.partial(_cmp_attn_body, tq=tq, pos0=pos0, n_c=n_c),
        grid=(b // nb, t // tq),
        in_specs=[blk(tq, NSA_W), pl.BlockSpec((nb, ncp, KV_ROW), lambda i, j: (i, 0, 0)),
                  pl.BlockSpec((NSA_KV, ncp, LANES), lambda i, j: (0, 0, 0))],
        out_specs=[blk(tq, NSA_W), blk(tq, LANES)],
        out_shape=[jax.ShapeDtypeStruct((b, t, NSA_W), F32), jax.ShapeDtypeStruct((b, t, LANES), F32)],
        compiler_params=_cparams(("parallel", "parallel")),
        name="nsa_cmp_attn",
    )(q3, kcv, _overlap(ncp, n_c))


def _gate_expand():
    e = np.zeros((GN_PAD, 3 * NSA_W), np.float32)
    for h in range(NSA_HEADS):
        for k in range(3):
            e[h * 3 + k, k * NSA_W + h * HEAD_DIM:k * NSA_W + (h + 1) * HEAD_DIM] = 1.0
    return jnp.asarray(e, BF16)


def _combine(y_ref, gn_ref, ge_ref, oc, osel, owin):
    gates = _dot2(_sigmoid(gn_ref[...]), ge_ref[...])
    y_ref[...] = (gates[:, :NSA_W] * oc + gates[:, NSA_W:2 * NSA_W] * osel + gates[:, 2 * NSA_W:] * owin)


def _per_head(allow):
    f = jnp.where(allow, 1.0, 0.0)
    return jnp.concatenate([f] * HPG, axis=0) > 0.5


def _sel_mask_tile(sel, g, kv0, tk):
    r = lax.broadcasted_iota(jnp.int32, (LANES, tk), 0)
    blk = (kv0 + lax.broadcasted_iota(jnp.int32, (LANES, tk), 1)) // L_SEL
    expand = (r == g * NBLK_PAD + blk).astype(BF16)
    return jnp.dot(sel.astype(BF16), expand, preferred_element_type=F32)


LOG2E = 1.4426950408889634


def _nsa_prompt_body(qr_ref, slc_ref, win_ref, oh_ref, sel_ref, oc_ref, gn_ref, ge_ref, y_ref,
                     os_ref, ow_ref, *, tq, tk, lw):
    qi = pl.program_id(1)
    q0 = qi * tq
    t_all = slc_ref.shape[0]
    nrow = HPG * tq
    heads = lambda a: jnp.concatenate([a] * HPG, axis=0)
    sel = sel_ref[...]
    lane = lax.broadcasted_iota(jnp.int32, (tq, LANES), 1)
    qpos = q0 + lax.broadcasted_iota(jnp.int32, (tq, 1), 0)
    w0 = pl.multiple_of(jnp.clip(q0 + tq - lw, 0, t_all - lw), 16)
    kpos_w = w0 + lax.broadcasted_iota(jnp.int32, (tq, lw), 1)
    wbias = heads(jnp.where((kpos_w <= qpos) & (kpos_w > qpos - WINDOW), 0.0, NEG))
    n_full = q0 // tk
    kv_diag = pl.multiple_of(n_full * tk, tk)
    kpos_d = kv_diag + lax.broadcasted_iota(jnp.int32, (tq, tk), 1)
    cbias = heads(jnp.where(kpos_d <= qpos, 0.0, NEG))
    qst, qaug = [], []
    for g in range(NSA_KV):
        q4 = jnp.concatenate([_head_to_half(qr_ref, g * HPG + hl, g) for hl in range(HPG)], axis=0)
        q4 = q4 * (HEAD_DIM ** -0.5 * LOG2E)
        selg = sel if g == 0 else pltpu.roll(sel, NBLK_PAD, 1)
        sbias = jnp.where(lane < NBLK_PAD, jnp.where(selg > 0.5, 0.0, NEG), 0.0)
        qst.append(q4.astype(BF16))
        qaug.append(jnp.concatenate([q4, heads(sbias)], axis=1).astype(BF16))

    def fold(carry, s, v):
        m, acc = carry
        m_new = jnp.maximum(m, jnp.max(s, axis=-1, keepdims=True))
        p = jnp.exp2(s - m_new)
        return m_new, jnp.exp2(m - m_new) * acc + jnp.dot(p.astype(BF16), v, preferred_element_type=F32)

    vlane = lax.broadcasted_iota(jnp.int32, (tk, LANES), 1)

    def sel_step(kv0, carries, bias):
        kaug = jnp.concatenate([slc_ref[pl.ds(kv0, tk), :KV_W], oh_ref[pl.ds(kv0, tk), :]], axis=1)
        v = slc_ref[pl.ds(kv0, tk), KV_W:]
        out = []
        for g in range(NSA_KV):
            s = lax.dot_general(qaug[g], kaug, (((1,), (1,)), ((), ())), preferred_element_type=F32)
            vg = jnp.where((vlane < HEAD_DIM) == (g == 0), v, jnp.ones_like(v))
            out.append(fold(carries[g], s if bias is None else s + bias, vg))
        return tuple(out)

    init = (jnp.full((nrow, 1), NEG, F32), jnp.zeros((nrow, LANES), F32))
    carries = lax.fori_loop(0, n_full, lambda it, c: sel_step(pl.multiple_of(it * tk, tk), c, None),
                            (init,) * NSA_KV)
    carries = sel_step(kv_diag, carries, cbias)
    kw = win_ref[pl.ds(w0, lw), :KV_W]
    vw = win_ref[pl.ds(w0, lw), KV_W:]
    wlane = lax.broadcasted_iota(jnp.int32, (lw, LANES), 1)
    for g in range(NSA_KV):
        rowsum = slice((1 - g) * HEAD_DIM, (1 - g) * HEAD_DIM + 1)
        acc = carries[g][1]
        o = acc / acc[:, rowsum]
        _halves_to_heads(os_ref, [o[hl * tq:(hl + 1) * tq] for hl in range(HPG)], g)
        s = lax.dot_general(qst[g], kw, (((1,), (1,)), ((), ())), preferred_element_type=F32) + wbias
        e = jnp.exp2(s - jnp.max(s, axis=-1, keepdims=True))
        vg = jnp.where((wlane < HEAD_DIM) == (g == 0), vw, jnp.ones_like(vw))
        acc = jnp.dot(e.astype(BF16), vg, preferred_element_type=F32)
        o = acc / acc[:, rowsum]
        _halves_to_heads(ow_ref, [o[hl * tq:(hl + 1) * tq] for hl in range(HPG)], g)
    _combine(y_ref, gn_ref, ge_ref, oc_ref[...], os_ref[...], ow_ref[...])


def _nsa_prompt(qr3, slc3, win3, sel3, oc3, gn3, tq, tk):
    b, t, _ = qr3.shape
    lw = min(WINDOW + tq, t)
    qblk = lambda w: pl.BlockSpec((None, tq, w), lambda i, j: (i, j, 0))
    full = pl.BlockSpec((None, t, KV_ROW), lambda i, j: (i, 0, 0))
    onehot = jnp.asarray(np.arange(t)[:, None] // L_SEL == np.arange(LANES)[None, :], BF16)
    return pl.pallas_call(
        functools.partial(_nsa_prompt_body, tq=tq, tk=tk, lw=lw),
        grid=(b, t // tq),
        in_specs=[qblk(NSA_W), full, full, pl.BlockSpec((t, LANES), lambda i, j: (0, 0)),
                  qblk(LANES), qblk(NSA_W), qblk(GN_PAD),
                  pl.BlockSpec((GN_PAD, 3 * NSA_W), lambda i, j: (0, 0))],
        out_specs=qblk(NSA_W),
        out_shape=jax.ShapeDtypeStruct((b, t, NSA_W), F32),
        scratch_shapes=[pltpu.VMEM((tq, NSA_W), F32), pltpu.VMEM((tq, NSA_W), F32)],
        compiler_params=_cparams(("parallel", "arbitrary")),
        name="nsa_attn_prompt",
    )(qr3, slc3, win3, onehot, sel3, oc3, gn3, _gate_expand())


def _nsa_sample_body(*refs, n_pages, nseq, past_len):
    page_refs = refs[1:1 + nseq * n_pages]
    (qr_ref, snew_ref, wold_ref, wnew_ref, sel_ref, oc_ref, gn_ref, ge_ref, y_ref, os_ref, ow_ref) = refs[1 + nseq * n_pages:]
    tq = qr_ref.shape[1]
    page = page_refs[0].shape[-1]
    wb = wold_ref.shape[-1]
    qpos = past_len + lax.broadcasted_iota(jnp.int32, (tq, 1), 0)

    def attend_all(jobs):
        scores = [[jnp.where(allow, _dot(q, kt), NEG) for kt, _, allow in cached]
                  + [jnp.where(allow, _dot_nt(q, k), NEG) for k, _, allow in fresh] for q, cached, fresh in jobs]
        masks = [[p[2] for p in cached + fresh] for _, cached, fresh in jobs]
        ms = [functools.reduce(jnp.maximum, [jnp.max(x, axis=-1, keepdims=True) for x in sc]) for sc in scores]
        es = [[jnp.where(allow, jnp.exp(x - m), 0.0) for x, allow in zip(sc, mk)]
              for sc, mk, m in zip(scores, masks, ms)]
        outs = []
        for e, (_, cached, fresh) in zip(es, jobs):
            l = sum(jnp.sum(x, axis=-1, keepdims=True) for x in e)
            acc = sum([_dot_nt(x, vt) for x, (_, vt, _) in zip(e, cached)]
                      + [_dot(x, v) for x, (_, v, _) in zip(e[len(cached):], fresh)])
            outs.append(acc / l)
        return outs

    def put_heads(o_ref, o, g):
        for pr in range(HPG // 2):
            pair = jnp.concatenate([o[2 * pr * tq:(2 * pr + 1) * tq], o[(2 * pr + 1) * tq:(2 * pr + 2) * tq]], axis=1)
            c = (g * HPG) // 2 + pr
            o_ref[:, c * LANES:(c + 1) * LANES] = pair

    pad = lambda x: jnp.concatenate([x, jnp.zeros((page - tq, KV_ROW), F32)], axis=0)
    kpos_new = past_len + lax.broadcasted_iota(jnp.int32, (tq, page), 1)
    causal_new = kpos_new <= qpos
    kpos_old = past_len - wb + lax.broadcasted_iota(jnp.int32, (tq, wb), 1)
    old_ok = _per_head((kpos_old > qpos - WINDOW) & (kpos_old >= 0))
    new_ok = _per_head(causal_new & (kpos_new > qpos - WINDOW))
    sel_jobs, win_jobs = [], []
    for sb in range(nseq):
        pages = page_refs[sb * n_pages:(sb + 1) * n_pages]
        sel = sel_ref[sb]
        snew, wnew = pad(snew_ref[sb]), pad(wnew_ref[sb])

        def head_q(h):
            c = qr_ref[sb, :, (h // 2) * LANES:(h // 2 + 1) * LANES]
            return (pltpu.roll(c, HEAD_DIM, 1) if h % 2 else c)[:, :HEAD_DIM]

        for g in range(NSA_KV):
            q = jnp.concatenate([head_q(g * HPG + hl) for hl in range(HPG)], axis=0) * HEAD_DIM ** -0.5
            kcol = slice(g * HEAD_DIM, (g + 1) * HEAD_DIM)
            vcol = slice(KV_W + g * HEAD_DIM, KV_W + (g + 1) * HEAD_DIM)
            cached = [(pages[k][0, g], pages[k][1, g], _per_head(_sel_mask_tile(sel, g, k * page, page) > 0.5))
                      for k in range(n_pages)]
            fresh = [(snew[:, kcol], snew[:, vcol],
                      _per_head((_sel_mask_tile(sel, g, past_len, page) > 0.5) & causal_new))]
            sel_jobs.append((q, cached, fresh))
            win_jobs.append((q, [(wold_ref[sb, 0, g], wold_ref[sb, 1, g], old_ok)],
                             [(wnew[:, kcol], wnew[:, vcol], new_ok)]))
    for o_ref, outs in ((os_ref, attend_all(sel_jobs)), (ow_ref, attend_all(win_jobs))):
        for j, o in enumerate(outs):
            put_heads(o_ref.at[j // NSA_KV], o, j % NSA_KV)
    for sb in range(nseq):
        _combine(y_ref.at[sb], gn_ref.at[sb], ge_ref, oc_ref[sb], os_ref[sb], ow_ref[sb])


ATTN_SEQS = 4


def _nsa_sample(qr3, slc_t, page_table, snew3, wold_t, wnew3, sel3, oc3, gn3):
    b, tq, _ = qr3.shape
    n_pages = page_table.shape[1]
    page = slc_t.shape[-1]
    wb = wold_t.shape[-1]
    nseq = next(d for d in range(ATTN_SEQS, 0, -1) if b % d == 0)
    per = lambda r, w: pl.BlockSpec((nseq, r, w), lambda i, pt: (i, 0, 0))
    pages = [_page_spec(page, functools.partial(lambda q, k, i, pt: (pt[i * nseq + q, k], 0, 0, 0, 0), q, k))
             for q in range(nseq) for k in range(n_pages)]
    return pl.pallas_call(
        functools.partial(_nsa_sample_body, n_pages=n_pages, nseq=nseq, past_len=n_pages * page),
        grid_spec=pltpu.PrefetchScalarGridSpec(
            num_scalar_prefetch=1, grid=(b // nseq,),
            in_specs=pages + [per(tq, NSA_W), per(tq, KV_ROW),
                              pl.BlockSpec((nseq, 2, NSA_KV, HEAD_DIM, wb), lambda i, pt: (i, 0, 0, 0, 0)),
                              per(tq, KV_ROW), per(tq, LANES), per(tq, NSA_W), per(tq, GN_PAD),
                              pl.BlockSpec((GN_PAD, 3 * NSA_W), lambda i, pt: (0, 0))],
            out_specs=per(tq, NSA_W),
            scratch_shapes=[pltpu.VMEM((nseq, tq, NSA_W), F32), pltpu.VMEM((nseq, tq, NSA_W), F32)]),
        out_shape=jax.ShapeDtypeStruct((b, tq, NSA_W), F32),
        compiler_params=_cparams(("arbitrary",)),
        name="nsa_attn_sample",
    )(page_table, *([slc_t] * (nseq * n_pages)), qr3, snew3, wold_t, wnew3, sel3, oc3, gn3, _gate_expand())


def _final_body(x_ref, yraw_ref, bonus_ref, g_ref, yn_ref, gates_ref, lnw_ref, lnb_ref, bd_ref,
                wbr_ref, wbn_ref, wo_ref, gffn_ref, wg_ref, wu_ref, wd_ref, gfin_ref, y_ref):
    bd = bd_ref[...]
    y = yraw_ref[...]
    d = y - _dot2(y, bd) * (1.0 / RWKV_HEAD)
    var = _dot2(d * d, bd) * (1.0 / RWKV_HEAD)
    y_r = (d * lax.rsqrt(var + RWKV_GN_EPS) * lnw_ref[...] + lnb_ref[...] + bonus_ref[...]) * g_ref[...]
    merged = (_sigmoid(gates_ref[:, :D_MODEL]) * _dot(y_r, wbr_ref[...])
              + _sigmoid(gates_ref[:, D_MODEL:]) * _dot(yn_ref[...], wbn_ref[...]))
    x1 = x_ref[...] + _dot(merged, wo_ref[...])
    h2 = x1 * lax.rsqrt(jnp.mean(x1 * x1, axis=-1, keepdims=True) + NORM_EPS) * gffn_ref[...]
    hb = h2.astype(BF16)
    up = _silu(jnp.dot(hb, wg_ref[...], preferred_element_type=F32)) * jnp.dot(hb, wu_ref[...], preferred_element_type=F32)
    x2 = x1 + _dot(up, wd_ref[...])
    y_ref[...] = x2 * lax.rsqrt(jnp.mean(x2 * x2, axis=-1, keepdims=True) + NORM_EPS) * gfin_ref[...]


def _final(x2d, yraw, bonus, g, yn, gates, prm, tm):
    m = x2d.shape[0]
    tm = min(tm, m)
    row = lambda w: pl.BlockSpec((tm, w), lambda i: (i, 0))
    return pl.pallas_call(
        _final_body,
        grid=(m // tm,),
        in_specs=[row(D_MODEL), row(RWKV_W), row(RWKV_W), row(RWKV_W), row(NSA_W), row(2 * D_MODEL),
                  _const_spec((1, RWKV_W)), _const_spec((1, RWKV_W)), _const_spec((RWKV_W, RWKV_W)),
                  _const_spec((RWKV_W, D_MODEL)), _const_spec((NSA_W, D_MODEL)),
                  _const_spec((D_MODEL, D_MODEL)), _const_spec((1, D_MODEL)),
                  _const_spec((D_MODEL, D_FF)), _const_spec((D_MODEL, D_FF)), _const_spec((D_FF, D_MODEL)),
                  _const_spec((1, D_MODEL))],
        out_specs=row(D_MODEL),
        out_shape=jax.ShapeDtypeStruct((m, D_MODEL), F32),
        compiler_params=_cparams(("parallel",)),
        name="merge_ffn",
    )(x2d, yraw, bonus, g, yn, gates, prm["ln_w"], prm["ln_b"], prm["bd"], prm["w_br_rwkv"],
      prm["w_br_nsa"], prm["w_out"], prm["g_ffn"], prm["w_ffn_gate"], prm["w_ffn_up"],
      prm["w_ffn_down"], prm["g_final"])


_S = (0, RWKV_W, RWKV_W + DECAY_RANK, 2 * RWKV_W + DECAY_RANK, 3 * RWKV_W + DECAY_RANK,
      3 * RWKV_W + DECAY_RANK + AAA_RANK, RWKV_COLS)
_RW_ORDER = (0, 2, 3, 1, 4, 5)


def _rw_regroup(x):
    return jnp.concatenate([x[..., _S[i]:_S[i + 1]] for i in _RW_ORDER], axis=-1)


def _rw_ungroup(y):
    starts = np.cumsum([0] + [_S[i + 1] - _S[i] for i in _RW_ORDER])
    where = {seg: k for k, seg in enumerate(_RW_ORDER)}
    return jnp.concatenate([y[..., starts[where[i]]:starts[where[i] + 1]] for i in range(len(_RW_ORDER))], axis=-1)


def _prep_params(l, g_mix, w_in, rwkv_mu, rwkv_w0, rwkv_w_decay, rwkv_a0, rwkv_w_aaa, rwkv_w_gate,
                 rwkv_k_k, rwkv_k_a, rwkv_r_k, rwkv_ln_w, rwkv_ln_b, w_br_rwkv, w_br_nsa, w_out, g_ffn,
                 w_ffn_gate, w_ffn_up, w_ffn_down, g_final):
    w = w_in[l]
    nsa0 = RWKV_COLS
    kv_end = nsa0 + NSA_W + 6 * KV_W
    zc = lambda n: jnp.zeros((D_MODEL, n), F32)
    w_r = jnp.concatenate([_rw_regroup(w[:, :RWKV_COLS]), zc(RW_PAD - RWKV_COLS), w[:, nsa0:kv_end],
                           w[:, kv_end:nsa0 + NSA_COLS], zc(GN_PAD - 3 * NSA_HEADS),
                           w[:, nsa0 + NSA_COLS:]], axis=1).astype(BF16)
    mu = jnp.concatenate([_rw_regroup(rwkv_mu[l]), jnp.zeros((RW_PAD - RWKV_COLS,), F32)])[None]
    wda = jnp.zeros((LANES, 2 * RWKV_W), F32)
    wda = wda.at[:DECAY_RANK, :RWKV_W].set(rwkv_w_decay[l]).at[DECAY_RANK:, RWKV_W:].set(rwkv_w_aaa[l])
    wg = jnp.zeros((RW_PAD - 3 * RWKV_W - LANES, RWKV_W), F32).at[:GATE_RANK].set(rwkv_w_gate[l])
    hid = np.arange(RWKV_W) // RWKV_HEAD
    bd = jnp.asarray(hid[:, None] == hid[None, :], BF16)
    r1 = lambda a: a.reshape(1, -1)
    return dict(
        g_mix=r1(g_mix[l]), w_r=w_r, mu=mu, wda=wda.astype(BF16), w0=r1(rwkv_w0[l]), a0=r1(rwkv_a0[l]),
        wg=wg.astype(BF16), k_k=r1(rwkv_k_k[l]), k_a=r1(rwkv_k_a[l]), r_k=r1(rwkv_r_k[l]), bd=bd,
        ln_w=r1(rwkv_ln_w[l]), ln_b=r1(rwkv_ln_b[l]), w_br_rwkv=w_br_rwkv[l].astype(BF16),
        w_br_nsa=w_br_nsa[l].astype(BF16), w_out=w_out[l].astype(BF16), g_ffn=r1(g_ffn[l]),
        w_ffn_gate=w_ffn_gate[l].astype(BF16), w_ffn_up=w_ffn_up[l].astype(BF16),
        w_ffn_down=w_ffn_down[l].astype(BF16), g_final=r1(g_final))


def _rope_tables(pos):
    half = ROT_DIM // 2
    inv = ROPE_THETA ** (-jnp.arange(half, dtype=F32) / half)
    ang = pos.astype(F32)[:, None] * inv[None, :]
    cos, sin = jnp.cos(ang), jnp.sin(ang)
    n = pos.shape[0]
    one, zero = jnp.ones((n, HEAD_DIM - ROT_DIM), F32), jnp.zeros((n, HEAD_DIM - ROT_DIM), F32)
    z8 = jnp.zeros((n, half), F32)
    c = jnp.concatenate([cos, cos, one], 1)
    s1 = jnp.concatenate([-sin, z8, zero], 1)
    s2 = jnp.concatenate([z8, sin, zero], 1)
    return tuple(jnp.concatenate([a, a], 1) for a in (c, s1, s2))


def _group(x, pos, shift0, s0, prm, cw, past, tm, pre_blk, tq):
    b, t, _ = x.shape
    m = b * t
    x2d = x.reshape(m, D_MODEL)
    p_rw, q, q_r, cmp_rows, slc_rows, win_rows, gn, gates, slc_bf, win_bf = _in_proj(
        x2d, prm["g_mix"], prm["w_r"], pos, tm)

    shift_r = jnp.concatenate([_rw_regroup(shift0), jnp.zeros((b, RW_PAD - RWKV_COLS), F32)], 1)[:, None, :]
    *scan_in, v, bonus, g = _rwkv_pre(p_rw.reshape(b, t, RW_PAD), shift_r, prm, *pre_blk)
    y_raw, st = _rwkv_chunk_scan(scan_in, v, _state_to_blockdiag(s0), t)
    s_new = _blockdiag_to_state(st)
    shift_new = _rw_ungroup(p_rw.reshape(b, t, RW_PAD)[:, -1, :RWKV_COLS])

    r3 = lambda a: a.reshape(b, t, a.shape[-1])
    rows_last = lambda c: jnp.moveaxis(c, 1, -1)
    rows_first = lambda c: jnp.moveaxis(c, -1, 1)
    if past is None:
        kcv = _compress_seq(cmp_rows, cw)
        n_c = t // D_CMP - 1
        oc, sel = _cmp_attn(r3(q), kcv, tq, 0, n_c)
        y_n = _nsa_prompt(r3(q_r), r3(slc_bf), r3(win_bf), sel, oc, r3(gn), min(2 * tq, t), min(512, t))
        new_cmp, new_slc = rows_first(cmp_rows), rows_first(slc_rows)
        win_state = rows_first(win_rows[..., t - min(WINDOW, t):])
    else:
        cmp_cache, slc_cache, win_cache, page_table = past
        past_len = page_table.shape[1] * cmp_cache.shape[1]
        kcv = _compress_paged(rows_last(cmp_cache), page_table, cw)
        n_c = (past_len + t) // D_CMP - 1
        oc, sel = _cmp_attn(r3(q), kcv, tq, past_len, n_c)
        y_n = _nsa_sample(r3(q_r), rows_last(slc_cache), page_table, r3(slc_rows), rows_last(win_cache),
                          r3(win_rows), sel, oc, r3(gn))
        kv6 = lambda a: a.reshape(b, t, 2, NSA_KV, HEAD_DIM)
        new_cmp, new_slc = kv6(cmp_rows), kv6(slc_rows)
        win_all = jnp.concatenate([win_cache, kv6(win_rows)], axis=1)
        win_state = win_all[:, win_all.shape[1] - min(WINDOW, win_all.shape[1]):]

    y = _final(x2d, y_raw, bonus, g, y_n.reshape(m, NSA_W), gates, prm, tm)
    return (y.reshape(b, t, D_MODEL), new_cmp, new_slc, win_state, s_new, shift_new)


def kernel(x_prompt, x_sample, cache_cmp_kv, cache_slc_kv, cache_win_kv, state_rwkv, state_rwkv_shift, page_table, g_mix, w_in, rwkv_mu, rwkv_w0, rwkv_w_decay, rwkv_a0, rwkv_w_aaa, rwkv_w_gate, rwkv_k_k, rwkv_k_a, rwkv_r_k, rwkv_ln_w, rwkv_ln_b, nsa_pe_cmp, nsa_w_cmp1, nsa_w_cmp2, w_br_rwkv, w_br_nsa, w_out, g_ffn, w_ffn_gate, w_ffn_up, w_ffn_down, g_final):
    depth = w_in.shape[0]
    assert depth == 1, "single-layer trunk"
    b, t, _ = x_prompt.shape
    bs, ts, _ = x_sample.shape
    n_pool, page = cache_cmp_kv.shape[1:3]
    past_len = page_table.shape[1] * page
    assert (past_len + ts) // D_CMP == past_len // D_CMP and past_len % D_CMP == 0
    prm = _prep_params(0, g_mix, w_in, rwkv_mu, rwkv_w0, rwkv_w_decay, rwkv_a0, rwkv_w_aaa, rwkv_w_gate,
                       rwkv_k_k, rwkv_k_a, rwkv_r_k, rwkv_ln_w, rwkv_ln_b, w_br_rwkv, w_br_nsa, w_out,
                       g_ffn, w_ffn_gate, w_ffn_up, w_ffn_down, g_final)
    cw = _compress_weights(nsa_pe_cmp[0], nsa_w_cmp1[0], nsa_w_cmp2[0])

    outs_p = _group(x_prompt, jnp.arange(t), jnp.zeros((b, RWKV_COLS), F32),
                    jnp.zeros((b, RWKV_HEADS, RWKV_HEAD, RWKV_HEAD), F32), prm, cw, None,
                    tm=256, pre_blk=(1, min(512, t)), tq=min(128, t))
    outs_s = _group(x_sample, past_len + jnp.arange(ts), state_rwkv_shift[0], state_rwkv[0], prm, cw,
                    (cache_cmp_kv[0], cache_slc_kv[0], cache_win_kv[0], page_table),
                    tm=256, pre_blk=(min(64, bs), ts), tq=ts)
    return (outs_p[0], outs_s[0]) + tuple(o[None] for o in outs_p[1:]) + tuple(o[None] for o in outs_s[1:])
```

```python
import functools

import jax
import jax.numpy as jnp
import numpy as np
from jax import lax
from jax.experimental import pallas as pl
from jax.experimental.pallas import tpu as pltpu

F32 = jnp.float32
BF16 = jnp.bfloat16

D_MODEL = 1024
RWKV_HEADS = 8
RWKV_HEAD = 64
RWKV_W = RWKV_HEADS * RWKV_HEAD
DECAY_RANK = 64
AAA_RANK = 64
GATE_RANK = 160
RWKV_GN_EPS = 64e-5
NSA_HEADS = 8
NSA_KV = 2
HEAD_DIM = 64
HPG = NSA_HEADS // NSA_KV
NSA_W = NSA_HEADS * HEAD_DIM
KV_W = NSA_KV * HEAD_DIM
L_CMP = 32
D_CMP = 16
CMP_HID = 128
L_SEL = 64
N_SEL = 16
WINDOW = 512
ROT_DIM = HEAD_DIM // 4
ROPE_THETA = 500000.0
D_FF = -(-8 * D_MODEL // (3 * 256)) * 256
NORM_EPS = 1e-6
NEG = -1e30
FORCE = 1e6
RWKV_COLS = 3 * RWKV_W + DECAY_RANK + AAA_RANK + GATE_RANK
NSA_COLS = NSA_W + 6 * KV_W + 3 * NSA_HEADS

LANES = 128
RW_PAD = 1920
GN_PAD = LANES
KV_ROW = 2 * KV_W
NBLK_PAD = 64
VMEM_LIMIT = 56 * 1024 * 1024


def _cparams(sem):
    return pltpu.CompilerParams(dimension_semantics=sem, vmem_limit_bytes=VMEM_LIMIT)


def _const_spec(shape):
    nd = len(shape)
    return pl.BlockSpec(shape, lambda *_: (0,) * nd, pipeline_mode=pl.Buffered(1))


def _dot(a, b):
    return jnp.dot(a.astype(BF16), b.astype(BF16), preferred_element_type=F32)


def _dot_nt(a, b):
    return lax.dot_general(a.astype(BF16), b.astype(BF16), (((1,), (1,)), ((), ())),
                           preferred_element_type=F32)


def _dot2(a, b):
    hi = a.astype(BF16)
    lo = (a - hi.astype(F32)).astype(BF16)
    return (jnp.dot(hi, b, preferred_element_type=F32)
            + jnp.dot(lo, b, preferred_element_type=F32))


def _sigmoid(x):
    return 1.0 / (1.0 + jnp.exp(-x))


def _silu(x):
    return x * _sigmoid(x)


def _rope128(x, cos, s1, s2):
    half = ROT_DIM // 2
    return x * cos + pltpu.roll(x, LANES - half, 1) * s1 + pltpu.roll(x, half, 1) * s2


_C_Q = RW_PAD
_C_CMP = _C_Q + NSA_W
_C_SLC = _C_CMP + KV_ROW
_C_WIN = _C_SLC + KV_ROW
_C_GN = _C_WIN + KV_ROW
_C_GATE = _C_GN + GN_PAD
_C_END = _C_GATE + 2 * D_MODEL


def _rwkv_vectors(x, prev_rows, tt, mu, wda, w0, a0, wg, k_k, k_a, r_k, bd):
    row = lax.broadcasted_iota(jnp.int32, x.shape, 0)
    shifted = jnp.where(row % tt == 0, prev_rows, pltpu.roll(x, 1, 0))
    xs = x + (shifted - x) * mu
    r = xs[:, 0:RWKV_W]
    k = xs[:, RWKV_W:2 * RWKV_W]
    v = xs[:, 2 * RWKV_W:3 * RWKV_W]
    wa = xs[:, 3 * RWKV_W:3 * RWKV_W + LANES]
    gl = xs[:, 3 * RWKV_W + LANES:RW_PAD]
    lane = lax.broadcasted_iota(jnp.int32, wa.shape, 1)
    z = _dot(jnp.where(lane < DECAY_RANK, jnp.tanh(wa), wa), wda)
    u = -(w0 + z[:, :RWKV_W])
    softplus = jnp.maximum(u, 0.0) + jnp.log1p(jnp.exp(-jnp.abs(u)))
    log_decay = -jnp.exp(-softplus - 0.5)
    a = _sigmoid(a0 + z[:, RWKV_W:])
    g = _dot(_sigmoid(gl), wg)
    kk = k * k_k
    kk = kk * lax.rsqrt(jnp.maximum(_dot2(kk * kk, bd), 1e-24))
    k_h = k * (1.0 + (a - 1.0) * k_a)
    bonus = _dot2(r * k_h * r_k, bd) * v
    return r, log_decay, k_h, -kk, kk * a, v, bonus, g


def _proj_body(x_ref, g_ref, w_ref, cos_ref, s1_ref, s2_ref, sh_ref, mu_ref, wda_ref, w0_ref, a0_ref, wg_ref,
               kk_ref, ka_ref, rk_ref, bd_ref,
               plast_ref, r_ref, lw_ref, k_ref, a_ref, b_ref, v_ref, bonus_ref, gg_ref,
               q_ref, qr_ref, cmp_ref, slc_ref, win_ref, gn_ref, gates_ref, slcb_ref, winb_ref, carry_ref,
               *, rows_last, period):
    x = x_ref[...]
    ms = jnp.mean(x * x, axis=-1, keepdims=True)
    h = (x * lax.rsqrt(ms + NORM_EPS) * g_ref[...]).astype(BF16)

    def mm(a, b):
        return jnp.dot(h, w_ref[:, a:b], preferred_element_type=F32)

    def put_rows(ref, kv):
        ref[...] = kv.T.reshape(ref.shape) if rows_last else kv

    prw = mm(0, _C_Q)
    bb = sh_ref.shape[0]
    tt = prw.shape[0] // bb
    prev = jnp.where(pl.program_id(0) % period == 0, sh_ref[...], carry_ref[...])
    last = prw.reshape(bb, tt, RW_PAD)[:, tt - 1:tt, :]
    carry_ref[...] = last
    plast_ref[...] = last
    prev_rows = jnp.broadcast_to(prev, (bb, tt, RW_PAD)).reshape(bb * tt, RW_PAD)
    outs = _rwkv_vectors(prw, prev_rows, tt, mu_ref[...], wda_ref[...], w0_ref[...], a0_ref[...], wg_ref[...],
                         kk_ref[...], ka_ref[...], rk_ref[...], bd_ref[...])
    for ref, val in zip((r_ref, lw_ref, k_ref, a_ref, b_ref, v_ref, bonus_ref, gg_ref), outs):
        ref[...] = val

    cos, s1, s2 = cos_ref[...], s1_ref[...], s2_ref[...]
    q = mm(_C_Q, _C_CMP)
    q_ref[...] = q
    for c in range(NSA_W // LANES):
        sl = slice(c * LANES, (c + 1) * LANES)
        qr_ref[:, sl] = _rope128(q[:, sl], cos, s1, s2)
    put_rows(cmp_ref, mm(_C_CMP, _C_SLC))
    for lo, hi, f_ref, h_ref in ((_C_SLC, _C_WIN, slc_ref, slcb_ref), (_C_WIN, _C_GN, win_ref, winb_ref)):
        kv = mm(lo, hi)
        kv = jnp.concatenate([_rope128(kv[:, :KV_W], cos, s1, s2), kv[:, KV_W:]], axis=1)
        put_rows(f_ref, kv)
        h_ref[...] = kv.astype(BF16)
    gn_ref[...] = mm(_C_GN, _C_GATE)
    gates_ref[...] = mm(_C_GATE, _C_END)


def _in_proj(x2d, prm, pos, shift0, tm):
    m = x2d.shape[0]
    t = pos.shape[0]
    nseq = m // t
    tm = min(tm, m)
    period = max(t, tm) // tm
    bb = max(1, tm // t)
    tabs = _rope_tables(jnp.tile(pos, bb))
    row = lambda w: pl.BlockSpec((tm, w), lambda i: (i, 0))
    tab = pl.BlockSpec((tm, LANES), lambda i: (i % period, 0))
    per_seq = pl.BlockSpec((bb, 1, RW_PAD), lambda i: (i // period, 0, 0))
    f32 = lambda w: (row(w), jax.ShapeDtypeStruct((m, w), F32))
    rows_last = t % tm == 0
    if rows_last:
        kv = (pl.BlockSpec((None, 2, NSA_KV, HEAD_DIM, tm), lambda i: (i // period, 0, 0, 0, i % period)),
              jax.ShapeDtypeStruct((nseq, 2, NSA_KV, HEAD_DIM, t), F32))
    else:
        kv = f32(KV_ROW)
    bf = (row(KV_ROW), jax.ShapeDtypeStruct((m, KV_ROW), BF16))
    outs = ([(per_seq, jax.ShapeDtypeStruct((nseq, 1, RW_PAD), F32))] + [f32(RWKV_W)] * 8
            + [f32(NSA_W), f32(NSA_W), kv, kv, kv, f32(GN_PAD), f32(2 * D_MODEL), bf, bf])
    return pl.pallas_call(
        functools.partial(_proj_body, rows_last=rows_last, period=period),
        grid=(m // tm,),
        in_specs=[row(D_MODEL), _const_spec((1, D_MODEL)), _const_spec((D_MODEL, _C_END)),
                  tab, tab, tab, per_seq,
                  _const_spec((1, RW_PAD)), _const_spec((LANES, 2 * RWKV_W)),
                  _const_spec((1, RWKV_W)), _const_spec((1, RWKV_W)),
                  _const_spec((RW_PAD - 3 * RWKV_W - LANES, RWKV_W)),
                  _const_spec((1, RWKV_W)), _const_spec((1, RWKV_W)), _const_spec((1, RWKV_W)),
                  _const_spec((RWKV_W, RWKV_W))],
        out_specs=[o[0] for o in outs],
        out_shape=[o[1] for o in outs],
        scratch_shapes=[pltpu.VMEM((bb, 1, RW_PAD), F32)],
        compiler_params=_cparams(("arbitrary",)),
        name="in_proj",
    )(x2d, prm["g_mix"], prm["w_r"], *tabs, shift0, prm["mu"], prm["wda"], prm["w0"], prm["a0"], prm["wg"],
      prm["k_k"], prm["k_a"], prm["r_k"], prm["bd"])


RWKV_CHUNK = 64


def _split3(x):
    x1 = x.astype(BF16)
    r1 = x - x1.astype(F32)
    x2 = r1.astype(BF16)
    return x1, x2, (r1 - x2.astype(F32)).astype(BF16)


def _chunk_scan_body(r_ref, lw_ref, k_ref, a_ref, b_ref, v_ref, s0_ref, tri_ref, y_ref, s_ref):
    nbat, c = r_ref.shape[0], r_ref.shape[1]
    n = 2 * c
    first = pl.program_id(1) == 0
    left = lax.broadcasted_iota(jnp.int32, (c, LANES), 1) < RWKV_HEAD
    stack = lambda x: jnp.concatenate([jnp.where(left, x, 0.0), jnp.where(left, 0.0, x)], axis=0)
    ri = lax.broadcasted_iota(jnp.int32, (n, n), 0)
    ci = lax.broadcasted_iota(jnp.int32, (n, n), 1)
    strict = ((ri < c) == (ci < c)) & (ci < ri)
    eye = jnp.where(ri == ci, 1.0, 0.0)
    ri2 = lax.broadcasted_iota(jnp.int32, (n, 2 * n), 0)
    ci2 = lax.broadcasted_iota(jnp.int32, (n, 2 * n), 1) % n
    incl2 = ((ri2 < c) == (ci2 < c)) & (ci2 <= ri2)
    tri = tri_ref[...]
    chains = [(i, p) for i in range(nbat) for p in range(RWKV_HEADS // 2)]
    pre = []
    for i, p in chains:
        sl = slice(p * LANES, (p + 1) * LANES)
        lw = lw_ref[i, :, sl]
        cs = sum(jnp.dot(tri, part, preferred_element_type=F32) for part in _split3(lw))
        c_end = cs[c - 1:c, :]
        e_neg = jnp.exp(-cs)
        e_hat = jnp.exp(c_end - cs)
        at = stack(a_ref[i, :, sl] * jnp.exp(cs - lw))
        rt = stack(r_ref[i, :, sl] * jnp.exp(cs))
        bt = stack(b_ref[i, :, sl] * e_neg)
        kt = stack(k_ref[i, :, sl] * e_neg)
        bk_hat = jnp.concatenate([stack(b_ref[i, :, sl] * e_hat), stack(k_ref[i, :, sl] * e_hat)], axis=0)
        g = _dot_nt(jnp.concatenate([at, rt], axis=0), jnp.concatenate([bt, kt], axis=0))
        pre.append(dict(sl=sl, e_end=jnp.exp(c_end), at=at, rt=rt, bk_hat=bk_hat.astype(BF16),
                        vs=stack(v_ref[i, :, sl]), low=jnp.where(strict, g[:n, :n], 0.0),
                        g_ak=jnp.where(strict, g[:n, n:], 0.0),
                        g_r=jnp.where(incl2, g[n:, :], 0.0)))
    tinv = [eye + q["low"] for q in pre]
    lp = [q["low"] for q in pre]
    for _ in range(c.bit_length() - 2):
        lp = [_dot(x, x) for x in lp]
        tinv = [t + _dot(t, x) for t, x in zip(tinv, lp)]
    state = [jnp.where(first, s0_ref[i, p], s_ref[i, p]) for i, p in chains]
    rhs = [_dot_nt(q["at"], s) + _dot(q["g_ak"], q["vs"]) for q, s in zip(pre, state)]
    uv = [jnp.concatenate([_dot(t, x), q["vs"]], axis=0) for t, x, q in zip(tinv, rhs, pre)]
    for (i, p), q, s, w in zip(chains, pre, state, uv):
        ys = _dot_nt(q["rt"], s) + _dot(q["g_r"], w)
        y_ref[i, :, q["sl"]] = ys[:c] + ys[c:]
        s_ref[i, p] = s * q["e_end"] + lax.dot_general(
            w.astype(BF16), q["bk_hat"], (((0,), (0,)), ((), ())), preferred_element_type=F32)


SCAN_SEQS = 8


def _rwkv_chunk_scan(rows, v, s0pair, t):
    m = v.shape[0]
    b = s0pair.shape[0]
    c = min(RWKV_CHUNK, t)
    nbat = next(d for d in range(SCAN_SEQS, 0, -1) if b % d == 0)
    as3 = lambda x: x.reshape(b, t, RWKV_W)
    rowm = pl.BlockSpec((nbat, c, RWKV_W), lambda i, j: (i, j, 0))
    sspec = pl.BlockSpec((nbat, RWKV_HEADS // 2, LANES, LANES), lambda i, j: (i, 0, 0, 0))
    tri = jnp.asarray(np.tril(np.ones((c, c), np.float32)), BF16)
    y, st = pl.pallas_call(
        _chunk_scan_body,
        grid=(b // nbat, t // c),
        in_specs=[rowm] * 6 + [sspec, pl.BlockSpec((c, c), lambda i, j: (0, 0))],
        out_specs=[rowm, sspec],
        out_shape=[jax.ShapeDtypeStruct((b, t, RWKV_W), F32), jax.ShapeDtypeStruct(s0pair.shape, F32)],
        compiler_params=_cparams(("parallel", "arbitrary")),
        name="rwkv_chunk_scan",
    )(*[as3(x) for x in rows], as3(v), s0pair, tri)
    return y.reshape(m, RWKV_W), st


def _state_to_blockdiag(s):
    b = s.shape[0]
    sp = s.reshape(b, RWKV_HEADS // 2, 2, RWKV_HEAD, RWKV_HEAD)
    z = jnp.zeros_like(sp[:, :, 0])
    top = jnp.concatenate([sp[:, :, 0], z], axis=-1)
    bot = jnp.concatenate([z, sp[:, :, 1]], axis=-1)
    return jnp.concatenate([top, bot], axis=-2)


def _blockdiag_to_state(sp):
    b = sp.shape[0]
    h0 = sp[:, :, :RWKV_HEAD, :RWKV_HEAD]
    h1 = sp[:, :, RWKV_HEAD:, RWKV_HEAD:]
    return jnp.stack([h0, h1], axis=2).reshape(b, RWKV_HEADS, RWKV_HEAD, RWKV_HEAD)


def _compress_body(*refs, n_src):
    pe_ref, w1_ref, w2_ref, o_ref, xs_ref = refs[-5:]
    pages = refs[-5 - n_src:-5]
    page = pages[0].shape[-1]
    for k, pg in enumerate(pages):
        for c in range(2):
            xs_ref[c, k * page:(k + 1) * page, :] = pg[c].reshape(KV_W, page).T
    per = n_src * page // D_CMP
    nseq, ncs = o_ref.shape[0], o_ref.shape[1]
    for c in range(2):
        rows = lambda l: xs_ref[c, pl.ds(l, per, stride=D_CMP), :]
        for j in range(D_CMP // 2):
            xl = jnp.concatenate([rows(2 * j), rows(2 * j + 1)], axis=1)
            f = _dot(xl + pe_ref[c, j:j + 1, :], w1_ref[c, j])
            s = _dot(xl + pe_ref[c, D_CMP // 2 + j:D_CMP // 2 + j + 1, :], w1_ref[c, D_CMP // 2 + j])
            hf = f if j == 0 else hf + f
            hs = s if j == 0 else hs + s
        nc = hf.shape[0]
        out = _dot(_silu(hf + pltpu.roll(hs, nc - 1, 0)), w2_ref[c])
        for q in range(nseq):
            o_ref[q, :, c * KV_W:(c + 1) * KV_W] = out[q * ncs:(q + 1) * ncs]


def _compress_weights(pe, w1, w2):
    eye = jnp.eye(NSA_KV, dtype=F32)
    w1b = jnp.einsum("cldh,gy->clgdyh", w1, eye).reshape(2, L_CMP, KV_W, NSA_KV * CMP_HID)
    w2b = jnp.einsum("chd,gy->cghyd", w2, eye).reshape(2, NSA_KV * CMP_HID, KV_W)
    peb = jnp.broadcast_to(pe[:, :, None, :], (2, L_CMP, NSA_KV, HEAD_DIM)).reshape(2, L_CMP, KV_W)
    return (peb.reshape(2, L_CMP // 2, 2 * KV_W),
            w1b.reshape(2, L_CMP // 2, 2 * KV_W, NSA_KV * CMP_HID).astype(BF16), w2b.astype(BF16))


CMP_SEQS = 4


def _page_spec(page, index_map):
    return pl.BlockSpec((None, 2, NSA_KV, HEAD_DIM, page), index_map)


def _compress_call(rows_t, page_specs, nb, nseq, nc, cw, prefetch):
    hid = NSA_KV * CMP_HID
    n_src = len(page_specs)
    page = page_specs[0].block_shape[-1]
    wspecs = [pl.BlockSpec((2, L_CMP // 2, 2 * KV_W), lambda *_: (0, 0, 0)),
              pl.BlockSpec((2, L_CMP // 2, 2 * KV_W, hid), lambda *_: (0, 0, 0, 0)),
              pl.BlockSpec((2, hid, KV_W), lambda *_: (0, 0, 0))]
    return pl.pallas_call(
        functools.partial(_compress_body, n_src=n_src),
        grid_spec=pltpu.PrefetchScalarGridSpec(
            num_scalar_prefetch=len(prefetch), grid=(nb // nseq,), in_specs=page_specs + wspecs,
            out_specs=pl.BlockSpec((nseq, nc, KV_ROW), lambda b, *_: (b, 0, 0)),
            scratch_shapes=[pltpu.VMEM((2, n_src * page, KV_W), F32)]),
        out_shape=jax.ShapeDtypeStruct((nb, nc, KV_ROW), F32),
        compiler_params=_cparams(("parallel",)),
        name="nsa_compress",
    )(*prefetch, *([rows_t] * n_src), *cw)


def _compress_seq(rows_t, cw):
    b, t = rows_t.shape[0], rows_t.shape[-1]
    specs = [_page_spec(LANES, functools.partial(lambda k, i: (i, 0, 0, 0, k), k)) for k in range(t // LANES)]
    return _compress_call(rows_t, specs, b, 1, t // D_CMP, cw, ())


def _compress_paged(cache_t, page_table, cw):
    b, n_pages = page_table.shape
    page = cache_t.shape[-1]
    nseq = CMP_SEQS if b % CMP_SEQS == 0 else 1
    specs = [_page_spec(page, functools.partial(lambda q, k, i, pt: (pt[i * nseq + q, k], 0, 0, 0, 0), q, k))
             for q in range(nseq) for k in range(n_pages)]
    return _compress_call(cache_t, specs, b, nseq, n_pages * page // D_CMP, cw, (page_table,))


def _head_to_half(q_ref, h, g):
    c = q_ref[:, (h // 2) * LANES:(h // 2 + 1) * LANES]
    if h % 2 != g:
        c = pltpu.roll(c, HEAD_DIM, 1)
    lane = lax.broadcasted_iota(jnp.int32, c.shape, 1)
    return jnp.where((lane >= g * HEAD_DIM) & (lane < (g + 1) * HEAD_DIM), c, 0.0)


def _halves_to_heads(o_ref, outs, g):
    for pr in range(HPG // 2):
        tiles = []
        for hl in (2 * pr, 2 * pr + 1):
            h = g * HPG + hl
            o = outs[hl]
            if h % 2 != g:
                o = pltpu.roll(o, HEAD_DIM, 1)
            tiles.append(o)
        lane = lax.broadcasted_iota(jnp.int32, tiles[0].shape, 1)
        c = (g * HPG) // 2 + pr
        o_ref[:, c * LANES:(c + 1) * LANES] = jnp.where(lane < HEAD_DIM, tiles[0], tiles[1])


def _cmp_attn_body(q_ref, kcv_ref, ov_ref, oc_ref, sel_ref, *, tq, pos0, n_c):
    qi = pl.program_id(1)
    nb, ncp = kcv_ref.shape[0], kcv_ref.shape[1]
    n = nb * tq
    row = lax.broadcasted_iota(jnp.int32, (tq, ncp), 0)
    col = lax.broadcasted_iota(jnp.int32, (tq, ncp), 1)
    mask = _per_head((col * D_CMP + (L_CMP - 1) <= pos0 + qi * tq + row) & (col < n_c))
    imps = []
    for sb in range(nb):
        kc = kcv_ref[sb, :, :KV_W]
        vc = kcv_ref[sb, :, KV_W:]
        imp = None
        for g in range(NSA_KV):
            q4 = jnp.concatenate([_head_to_half(q_ref.at[sb], g * HPG + hl, g) for hl in range(HPG)], axis=0)
            s = jnp.where(mask, _dot_nt(q4, kc) * HEAD_DIM ** -0.5, NEG)
            e = jnp.exp(s - jnp.max(s, axis=-1, keepdims=True))
            p = jnp.where(mask, e / jnp.sum(e, axis=-1, keepdims=True), 0.0)
            o = _dot(p, vc)
            _halves_to_heads(oc_ref.at[sb], [o[hl * tq:(hl + 1) * tq] for hl in range(HPG)], g)
            psum = sum(p[hl * tq:(hl + 1) * tq] for hl in range(HPG))
            ig = _dot2(psum, ov_ref[g])
            imp = ig if imp is None else imp + ig
        imps.append(imp)
    imp_t = jnp.concatenate(imps, axis=0).T
    jb = lax.broadcasted_iota(jnp.int32, (LANES, n), 0) % NBLK_PAD
    cur = (pos0 + qi * tq + lax.broadcasted_iota(jnp.int32, (LANES, n), 1) % tq) // L_SEL
    forced = (jb == 0) | (jb == cur) | (jb == cur - 1)
    score = jnp.where(forced, FORCE, jnp.where(jb <= cur, imp_t, -1.0))
    sub = lax.broadcasted_iota(jnp.int32, (8, n), 0)
    ranks = []
    for g in range(NSA_KV):
        blocks = [score[g * NBLK_PAD + 8 * k:g * NBLK_PAD + 8 * k + 8, :] for k in range(NBLK_PAD // 8)]
        cnt = [jnp.zeros((8, n), F32) for _ in blocks]
        for jp in range(NBLK_PAD):
            other = jnp.broadcast_to(score[g * NBLK_PAD + jp:g * NBLK_PAD + jp + 1, :], (8, n))
            for k, blk in enumerate(blocks):
                if 8 * k > jp:
                    ahead = other >= blk
                elif 8 * k + 7 <= jp:
                    ahead = other > blk
                else:
                    ahead = (other > blk) | ((other == blk) & (sub > jp - 8 * k))
                cnt[k] = cnt[k] + jnp.where(ahead, 1.0, 0.0)
        ranks += cnt
    rank = jnp.concatenate(ranks, axis=0)
    sel = jnp.where((rank < N_SEL) & (jb <= cur), 1.0, 0.0).T
    for sb in range(nb):
        sel_ref[sb] = sel[sb * tq:(sb + 1) * tq]


def _overlap(ncp, n_c):
    ci = np.arange(ncp)[:, None]
    sj = np.arange(NBLK_PAD)[None, :]
    ov = ((ci * D_CMP < (sj + 1) * L_SEL) & (ci * D_CMP + L_CMP > sj * L_SEL) & (ci < n_c)).astype(np.float32)
    z = np.zeros_like(ov)
    return jnp.asarray(np.stack([np.concatenate([ov, z], 1), np.concatenate([z, ov], 1)]), BF16)


def _cmp_attn(q3, kcv, tq, pos0, n_c):
    b, t, _ = q3.shape
    ncp = kcv.shape[1]
    nb = LANES // tq
    blk = lambda r, w: pl.BlockSpec((nb, r, w), lambda i, j: (i, j, 0))
    return pl.pallas_call(
        functools.partial(_cmp_attn_body, tq=tq, pos0=pos0, n_c=n_c),
        grid=(b // nb, t // tq),
        in_specs=[blk(tq, NSA_W), pl.BlockSpec((nb, ncp, KV_ROW), lambda i, j: (i, 0, 0)),
                  pl.BlockSpec((NSA_KV, ncp, LANES), lambda i, j: (0, 0, 0))],
        out_specs=[blk(tq, NSA_W), blk(tq, LANES)],
        out_shape=[jax.ShapeDtypeStruct((b, t, NSA_W), F32), jax.ShapeDtypeStruct((b, t, LANES), F32)],
        compiler_params=_cparams(("parallel", "parallel")),
        name="nsa_cmp_attn",
    )(q3, kcv, _overlap(ncp, n_c))


def _gate_expand():
    e = np.zeros((GN_PAD, 3 * NSA_W), np.float32)
    for h in range(NSA_HEADS):
        for k in range(3):
            e[h * 3 + k, k * NSA_W + h * HEAD_DIM:k * NSA_W + (h + 1) * HEAD_DIM] = 1.0
    return jnp.asarray(e, BF16)


def _combine(y_ref, gn_ref, ge_ref, oc, osel, owin):
    gates = _dot2(_sigmoid(gn_ref[...]), ge_ref[...])
    y_ref[...] = (gates[:, :NSA_W] * oc + gates[:, NSA_W:2 * NSA_W] * osel + gates[:, 2 * NSA_W:] * owin)


def _per_head(allow):
    f = jnp.where(allow, 1.0, 0.0)
    return jnp.concatenate([f] * HPG, axis=0) > 0.5


def _sel_mask_tile(sel, g, kv0, tk):
    r = lax.broadcasted_iota(jnp.int32, (LANES, tk), 0)
    blk = (kv0 + lax.broadcasted_iota(jnp.int32, (LANES, tk), 1)) // L_SEL
    expand = (r == g * NBLK_PAD + blk).astype(BF16)
    return jnp.dot(sel.astype(BF16), expand, preferred_element_type=F32)


LOG2E = 1.4426950408889634


def _nsa_prompt_body(qr_ref, slc_ref, win_ref, oh_ref, sel_ref, oc_ref, gn_ref, ge_ref, y_ref,
                     os_ref, ow_ref, *, tq, tk, lw):
    qi = pl.program_id(1)
    q0 = qi * tq
    t_all = slc_ref.shape[0]
    nrow = HPG * tq
    heads = lambda a: jnp.concatenate([a] * HPG, axis=0)
    sel = sel_ref[...]
    lane = lax.broadcasted_iota(jnp.int32, (tq, LANES), 1)
    qpos = q0 + lax.broadcasted_iota(jnp.int32, (tq, 1), 0)
    w0 = pl.multiple_of(jnp.clip(q0 + tq - lw, 0, t_all - lw), 16)
    kpos_w = w0 + lax.broadcasted_iota(jnp.int32, (tq, lw), 1)
    wbias = heads(jnp.where((kpos_w <= qpos) & (kpos_w > qpos - WINDOW), 0.0, NEG))
    n_full = q0 // tk
    kv_diag = pl.multiple_of(n_full * tk, tk)
    kpos_d = kv_diag + lax.broadcasted_iota(jnp.int32, (tq, tk), 1)
    cbias = heads(jnp.where(kpos_d <= qpos, 0.0, NEG))
    qst, qaug = [], []
    for g in range(NSA_KV):
        q4 = jnp.concatenate([_head_to_half(qr_ref, g * HPG + hl, g) for hl in range(HPG)], axis=0)
        q4 = q4 * (HEAD_DIM ** -0.5 * LOG2E)
        selg = sel if g == 0 else pltpu.roll(sel, NBLK_PAD, 1)
        sbias = jnp.where(lane < NBLK_PAD, jnp.where(selg > 0.5, 0.0, NEG), 0.0)
        qst.append(q4.astype(BF16))
        qaug.append(jnp.concatenate([q4, heads(sbias)], axis=1).astype(BF16))

    def fold(carry, s, v):
        m, acc = carry
        m_new = jnp.maximum(m, jnp.max(s, axis=-1, keepdims=True))
        p = jnp.exp2(s - m_new)
        return m_new, jnp.exp2(m - m_new) * acc + jnp.dot(p.astype(BF16), v, preferred_element_type=F32)

    vlane = lax.broadcasted_iota(jnp.int32, (tk, LANES), 1)

    def sel_step(kv0, carries, bias):
        kaug = jnp.concatenate([slc_ref[pl.ds(kv0, tk), :KV_W], oh_ref[pl.ds(kv0, tk), :]], axis=1)
        v = slc_ref[pl.ds(kv0, tk), KV_W:]
        out = []
        for g in range(NSA_KV):
            s = lax.dot_general(qaug[g], kaug, (((1,), (1,)), ((), ())), preferred_element_type=F32)
            vg = jnp.where((vlane < HEAD_DIM) == (g == 0), v, jnp.ones_like(v))
            out.append(fold(carries[g], s if bias is None else s + bias, vg))
        return tuple(out)

    init = (jnp.full((nrow, 1), NEG, F32), jnp.zeros((nrow, LANES), F32))
    carries = lax.fori_loop(0, n_full, lambda it, c: sel_step(pl.multiple_of(it * tk, tk), c, None),
                            (init,) * NSA_KV)
    carries = sel_step(kv_diag, carries, cbias)
    kw = win_ref[pl.ds(w0, lw), :KV_W]
    vw = win_ref[pl.ds(w0, lw), KV_W:]
    wlane = lax.broadcasted_iota(jnp.int32, (lw, LANES), 1)
    for g in range(NSA_KV):
        rowsum = slice((1 - g) * HEAD_DIM, (1 - g) * HEAD_DIM + 1)
        acc = carries[g][1]
        o = acc / acc[:, rowsum]
        _halves_to_heads(os_ref, [o[hl * tq:(hl + 1) * tq] for hl in range(HPG)], g)
        s = lax.dot_general(qst[g], kw, (((1,), (1,)), ((), ())), preferred_element_type=F32) + wbias
        e = jnp.exp2(s - jnp.max(s, axis=-1, keepdims=True))
        vg = jnp.where((wlane < HEAD_DIM) == (g == 0), vw, jnp.ones_like(vw))
        acc = jnp.dot(e.astype(BF16), vg, preferred_element_type=F32)
        o = acc / acc[:, rowsum]
        _halves_to_heads(ow_ref, [o[hl * tq:(hl + 1) * tq] for hl in range(HPG)], g)
    _combine(y_ref, gn_ref, ge_ref, oc_ref[...], os_ref[...], ow_ref[...])


def _nsa_prompt(qr3, slc3, win3, sel3, oc3, gn3, tq, tk):
    b, t, _ = qr3.shape
    lw = min(WINDOW + tq, t)
    qblk = lambda w: pl.BlockSpec((None, tq, w), lambda i, j: (i, j, 0))
    full = pl.BlockSpec((None, t, KV_ROW), lambda i, j: (i, 0, 0))
    onehot = jnp.asarray(np.arange(t)[:, None] // L_SEL == np.arange(LANES)[None, :], BF16)
    return pl.pallas_call(
        functools.partial(_nsa_prompt_body, tq=tq, tk=tk, lw=lw),
        grid=(b, t // tq),
        in_specs=[qblk(NSA_W), full, full, pl.BlockSpec((t, LANES), lambda i, j: (0, 0)),
                  qblk(LANES), qblk(NSA_W), qblk(GN_PAD),
                  pl.BlockSpec((GN_PAD, 3 * NSA_W), lambda i, j: (0, 0))],
        out_specs=qblk(NSA_W),
        out_shape=jax.ShapeDtypeStruct((b, t, NSA_W), F32),
        scratch_shapes=[pltpu.VMEM((tq, NSA_W), F32), pltpu.VMEM((tq, NSA_W), F32)],
        compiler_params=_cparams(("parallel", "arbitrary")),
        name="nsa_attn_prompt",
    )(qr3, slc3, win3, onehot, sel3, oc3, gn3, _gate_expand())


def _nsa_sample_body(*refs, n_pages, nseq, past_len):
    page_refs = refs[1:1 + nseq * n_pages]
    (qr_ref, snew_ref, wold_ref, wnew_ref, sel_ref, oc_ref, gn_ref, ge_ref, y_ref, os_ref, ow_ref) = refs[1 + nseq * n_pages:]
    tq = qr_ref.shape[1]
    page = page_refs[0].shape[-1]
    wb = wold_ref.shape[-1]
    qpos = past_len + lax.broadcasted_iota(jnp.int32, (tq, 1), 0)

    def attend_all(jobs):
        scores = [[jnp.where(allow, _dot(q, kt), NEG) for kt, _, allow in cached]
                  + [jnp.where(allow, _dot_nt(q, k), NEG) for k, _, allow in fresh] for q, cached, fresh in jobs]
        masks = [[p[2] for p in cached + fresh] for _, cached, fresh in jobs]
        ms = [functools.reduce(jnp.maximum, [jnp.max(x, axis=-1, keepdims=True) for x in sc]) for sc in scores]
        es = [[jnp.where(allow, jnp.exp(x - m), 0.0) for x, allow in zip(sc, mk)]
              for sc, mk, m in zip(scores, masks, ms)]
        outs = []
        for e, (_, cached, fresh) in zip(es, jobs):
            l = sum(jnp.sum(x, axis=-1, keepdims=True) for x in e)
            acc = sum([_dot_nt(x, vt) for x, (_, vt, _) in zip(e, cached)]
                      + [_dot(x, v) for x, (_, v, _) in zip(e[len(cached):], fresh)])
            outs.append(acc / l)
        return outs

    def put_heads(o_ref, o, g):
        for pr in range(HPG // 2):
            pair = jnp.concatenate([o[2 * pr * tq:(2 * pr + 1) * tq], o[(2 * pr + 1) * tq:(2 * pr + 2) * tq]], axis=1)
            c = (g * HPG) // 2 + pr
            o_ref[:, c * LANES:(c + 1) * LANES] = pair

    pad = lambda x: jnp.concatenate([x, jnp.zeros((page - tq, KV_ROW), F32)], axis=0)
    kpos_new = past_len + lax.broadcasted_iota(jnp.int32, (tq, page), 1)
    causal_new = kpos_new <= qpos
    kpos_old = past_len - wb + lax.broadcasted_iota(jnp.int32, (tq, wb), 1)
    old_ok = _per_head((kpos_old > qpos - WINDOW) & (kpos_old >= 0))
    new_ok = _per_head(causal_new & (kpos_new > qpos - WINDOW))
    sel_jobs, win_jobs = [], []
    for sb in range(nseq):
        pages = page_refs[sb * n_pages:(sb + 1) * n_pages]
        sel = sel_ref[sb]
        snew, wnew = pad(snew_ref[sb]), pad(wnew_ref[sb])

        def head_q(h):
            c = qr_ref[sb, :, (h // 2) * LANES:(h // 2 + 1) * LANES]
            return (pltpu.roll(c, HEAD_DIM, 1) if h % 2 else c)[:, :HEAD_DIM]

        for g in range(NSA_KV):
            q = jnp.concatenate([head_q(g * HPG + hl) for hl in range(HPG)], axis=0) * HEAD_DIM ** -0.5
            kcol = slice(g * HEAD_DIM, (g + 1) * HEAD_DIM)
            vcol = slice(KV_W + g * HEAD_DIM, KV_W + (g + 1) * HEAD_DIM)
            cached = [(pages[k][0, g], pages[k][1, g], _per_head(_sel_mask_tile(sel, g, k * page, page) > 0.5))
                      for k in range(n_pages)]
            fresh = [(snew[:, kcol], snew[:, vcol],
                      _per_head((_sel_mask_tile(sel, g, past_len, page) > 0.5) & causal_new))]
            sel_jobs.append((q, cached, fresh))
            win_jobs.append((q, [(wold_ref[sb, 0, g], wold_ref[sb, 1, g], old_ok)],
                             [(wnew[:, kcol], wnew[:, vcol], new_ok)]))
    for o_ref, outs in ((os_ref, attend_all(sel_jobs)), (ow_ref, attend_all(win_jobs))):
        for j, o in enumerate(outs):
            put_heads(o_ref.at[j // NSA_KV], o, j % NSA_KV)
    for sb in range(nseq):
        _combine(y_ref.at[sb], gn_ref.at[sb], ge_ref, oc_ref[sb], os_ref[sb], ow_ref[sb])


ATTN_SEQS = 4


def _nsa_sample(qr3, slc_t, page_table, snew3, wold_t, wnew3, sel3, oc3, gn3):
    b, tq, _ = qr3.shape
    n_pages = page_table.shape[1]
    page = slc_t.shape[-1]
    wb = wold_t.shape[-1]
    nseq = next(d for d in range(ATTN_SEQS, 0, -1) if b % d == 0)
    per = lambda r, w: pl.BlockSpec((nseq, r, w), lambda i, pt: (i, 0, 0))
    pages = [_page_spec(page, functools.partial(lambda q, k, i, pt: (pt[i * nseq + q, k], 0, 0, 0, 0), q, k))
             for q in range(nseq) for k in range(n_pages)]
    return pl.pallas_call(
        functools.partial(_nsa_sample_body, n_pages=n_pages, nseq=nseq, past_len=n_pages * page),
        grid_spec=pltpu.PrefetchScalarGridSpec(
            num_scalar_prefetch=1, grid=(b // nseq,),
            in_specs=pages + [per(tq, NSA_W), per(tq, KV_ROW),
                              pl.BlockSpec((nseq, 2, NSA_KV, HEAD_DIM, wb), lambda i, pt: (i, 0, 0, 0, 0)),
                              per(tq, KV_ROW), per(tq, LANES), per(tq, NSA_W), per(tq, GN_PAD),
                              pl.BlockSpec((GN_PAD, 3 * NSA_W), lambda i, pt: (0, 0))],
            out_specs=per(tq, NSA_W),
            scratch_shapes=[pltpu.VMEM((nseq, tq, NSA_W), F32), pltpu.VMEM((nseq, tq, NSA_W), F32)]),
        out_shape=jax.ShapeDtypeStruct((b, tq, NSA_W), F32),
        compiler_params=_cparams(("arbitrary",)),
        name="nsa_attn_sample",
    )(page_table, *([slc_t] * (nseq * n_pages)), qr3, snew3, wold_t, wnew3, sel3, oc3, gn3, _gate_expand())


def _final_body(x_ref, yraw_ref, bonus_ref, g_ref, yn_ref, gates_ref, lnw_ref, lnb_ref, bd_ref,
                wbr_ref, wbn_ref, wo_ref, gffn_ref, wg_ref, wu_ref, wd_ref, gfin_ref, y_ref):
    bd = bd_ref[...]
    y = yraw_ref[...]
    d = y - _dot2(y, bd) * (1.0 / RWKV_HEAD)
    var = _dot2(d * d, bd) * (1.0 / RWKV_HEAD)
    y_r = (d * lax.rsqrt(var + RWKV_GN_EPS) * lnw_ref[...] + lnb_ref[...] + bonus_ref[...]) * g_ref[...]
    merged = (_sigmoid(gates_ref[:, :D_MODEL]) * _dot(y_r, wbr_ref[...])
              + _sigmoid(gates_ref[:, D_MODEL:]) * _dot(yn_ref[...], wbn_ref[...]))
    x1 = x_ref[...] + _dot(merged, wo_ref[...])
    h2 = x1 * lax.rsqrt(jnp.mean(x1 * x1, axis=-1, keepdims=True) + NORM_EPS) * gffn_ref[...]
    hb = h2.astype(BF16)
    up = _silu(jnp.dot(hb, wg_ref[...], preferred_element_type=F32)) * jnp.dot(hb, wu_ref[...], preferred_element_type=F32)
    x2 = x1 + _dot(up, wd_ref[...])
    y_ref[...] = x2 * lax.rsqrt(jnp.mean(x2 * x2, axis=-1, keepdims=True) + NORM_EPS) * gfin_ref[...]


def _final(x2d, yraw, bonus, g, yn, gates, prm, tm):
    m = x2d.shape[0]
    tm = min(tm, m)
    row = lambda w: pl.BlockSpec((tm, w), lambda i: (i, 0))
    return pl.pallas_call(
        _final_body,
        grid=(m // tm,),
        in_specs=[row(D_MODEL), row(RWKV_W), row(RWKV_W), row(RWKV_W), row(NSA_W), row(2 * D_MODEL),
                  _const_spec((1, RWKV_W)), _const_spec((1, RWKV_W)), _const_spec((RWKV_W, RWKV_W)),
                  _const_spec((RWKV_W, D_MODEL)), _const_spec((NSA_W, D_MODEL)),
                  _const_spec((D_MODEL, D_MODEL)), _const_spec((1, D_MODEL)),
                  _const_spec((D_MODEL, D_FF)), _const_spec((D_MODEL, D_FF)), _const_spec((D_FF, D_MODEL)),
                  _const_spec((1, D_MODEL))],
        out_specs=row(D_MODEL),
        out_shape=jax.ShapeDtypeStruct((m, D_MODEL), F32),
        compiler_params=_cparams(("parallel",)),
        name="merge_ffn",
    )(x2d, yraw, bonus, g, yn, gates, prm["ln_w"], prm["ln_b"], prm["bd"], prm["w_br_rwkv"],
      prm["w_br_nsa"], prm["w_out"], prm["g_ffn"], prm["w_ffn_gate"], prm["w_ffn_up"],
      prm["w_ffn_down"], prm["g_final"])


_S = (0, RWKV_W, RWKV_W + DECAY_RANK, 2 * RWKV_W + DECAY_RANK, 3 * RWKV_W + DECAY_RANK,
      3 * RWKV_W + DECAY_RANK + AAA_RANK, RWKV_COLS)
_RW_ORDER = (0, 2, 3, 1, 4, 5)


def _rw_regroup(x):
    return jnp.concatenate([x[..., _S[i]:_S[i + 1]] for i in _RW_ORDER], axis=-1)


def _rw_ungroup(y):
    starts = np.cumsum([0] + [_S[i + 1] - _S[i] for i in _RW_ORDER])
    where = {seg: k for k, seg in enumerate(_RW_ORDER)}
    return jnp.concatenate([y[..., starts[where[i]]:starts[where[i] + 1]] for i in range(len(_RW_ORDER))], axis=-1)


def _prep_params(l, g_mix, w_in, rwkv_mu, rwkv_w0, rwkv_w_decay, rwkv_a0, rwkv_w_aaa, rwkv_w_gate,
                 rwkv_k_k, rwkv_k_a, rwkv_r_k, rwkv_ln_w, rwkv_ln_b, w_br_rwkv, w_br_nsa, w_out, g_ffn,
                 w_ffn_gate, w_ffn_up, w_ffn_down, g_final):
    w = w_in[l]
    nsa0 = RWKV_COLS
    kv_end = nsa0 + NSA_W + 6 * KV_W
    zc = lambda n: jnp.zeros((D_MODEL, n), F32)
    w_r = jnp.concatenate([_rw_regroup(w[:, :RWKV_COLS]), zc(RW_PAD - RWKV_COLS), w[:, nsa0:kv_end],
                           w[:, kv_end:nsa0 + NSA_COLS], zc(GN_PAD - 3 * NSA_HEADS),
                           w[:, nsa0 + NSA_COLS:]], axis=1).astype(BF16)
    mu = jnp.concatenate([_rw_regroup(rwkv_mu[l]), jnp.zeros((RW_PAD - RWKV_COLS,), F32)])[None]
    wda = jnp.zeros((LANES, 2 * RWKV_W), F32)
    wda = wda.at[:DECAY_RANK, :RWKV_W].set(rwkv_w_decay[l]).at[DECAY_RANK:, RWKV_W:].set(rwkv_w_aaa[l])
    wg = jnp.zeros((RW_PAD - 3 * RWKV_W - LANES, RWKV_W), F32).at[:GATE_RANK].set(rwkv_w_gate[l])
    hid = np.arange(RWKV_W) // RWKV_HEAD
    bd = jnp.asarray(hid[:, None] == hid[None, :], BF16)
    r1 = lambda a: a.reshape(1, -1)
    return dict(
        g_mix=r1(g_mix[l]), w_r=w_r, mu=mu, wda=wda.astype(BF16), w0=r1(rwkv_w0[l]), a0=r1(rwkv_a0[l]),
        wg=wg.astype(BF16), k_k=r1(rwkv_k_k[l]), k_a=r1(rwkv_k_a[l]), r_k=r1(rwkv_r_k[l]), bd=bd,
        ln_w=r1(rwkv_ln_w[l]), ln_b=r1(rwkv_ln_b[l]), w_br_rwkv=w_br_rwkv[l].astype(BF16),
        w_br_nsa=w_br_nsa[l].astype(BF16), w_out=w_out[l].astype(BF16), g_ffn=r1(g_ffn[l]),
        w_ffn_gate=w_ffn_gate[l].astype(BF16), w_ffn_up=w_ffn_up[l].astype(BF16),
        w_ffn_down=w_ffn_down[l].astype(BF16), g_final=r1(g_final))


def _rope_tables(pos):
    half = ROT_DIM // 2
    inv = ROPE_THETA ** (-jnp.arange(half, dtype=F32) / half)
    ang = pos.astype(F32)[:, None] * inv[None, :]
    cos, sin = jnp.cos(ang), jnp.sin(ang)
    n = pos.shape[0]
    one, zero = jnp.ones((n, HEAD_DIM - ROT_DIM), F32), jnp.zeros((n, HEAD_DIM - ROT_DIM), F32)
    z8 = jnp.zeros((n, half), F32)
    c = jnp.concatenate([cos, cos, one], 1)
    s1 = jnp.concatenate([-sin, z8, zero], 1)
    s2 = jnp.concatenate([z8, sin, zero], 1)
    return tuple(jnp.concatenate([a, a], 1) for a in (c, s1, s2))


def _group(x, pos, shift0, s0, prm, cw, past, tm, tq):
    b, t, _ = x.shape
    m = b * t
    x2d = x.reshape(m, D_MODEL)
    shift_r = jnp.concatenate([_rw_regroup(shift0), jnp.zeros((b, RW_PAD - RWKV_COLS), F32)], 1)[:, None, :]
    (p_last, *scan_in, v, bonus, g, q, q_r, cmp_rows, slc_rows, win_rows, gn, gates, slc_bf, win_bf) = _in_proj(
        x2d, prm, pos, shift_r, tm)
    y_raw, st = _rwkv_chunk_scan(scan_in, v, _state_to_blockdiag(s0), t)
    s_new = _blockdiag_to_state(st)
    shift_new = _rw_ungroup(p_last[:, 0, :RWKV_COLS])

    r3 = lambda a: a.reshape(b, t, a.shape[-1])
    rows_last = lambda c: jnp.moveaxis(c, 1, -1)
    rows_first = lambda c: jnp.moveaxis(c, -1, 1)
    if past is None:
        kcv = _compress_seq(cmp_rows, cw)
        n_c = t // D_CMP - 1
        oc, sel = _cmp_attn(r3(q), kcv, tq, 0, n_c)
        y_n = _nsa_prompt(r3(q_r), r3(slc_bf), r3(win_bf), sel, oc, r3(gn), min(2 * tq, t), min(512, t))
        new_cmp, new_slc = rows_first(cmp_rows), rows_first(slc_rows)
        win_state = rows_first(win_rows[..., t - min(WINDOW, t):])
    else:
        cmp_cache, slc_cache, win_cache, page_table = past
        past_len = page_table.shape[1] * cmp_cache.shape[1]
        kcv = _compress_paged(rows_last(cmp_cache), page_table, cw)
        n_c = (past_len + t) // D_CMP - 1
        oc, sel = _cmp_attn(r3(q), kcv, tq, past_len, n_c)
        y_n = _nsa_sample(r3(q_r), rows_last(slc_cache), page_table, r3(slc_rows), rows_last(win_cache),
                          r3(win_rows), sel, oc, r3(gn))
        kv6 = lambda a: a.reshape(b, t, 2, NSA_KV, HEAD_DIM)
        new_cmp, new_slc = kv6(cmp_rows), kv6(slc_rows)
        win_all = jnp.concatenate([win_cache, kv6(win_rows)], axis=1)
        win_state = win_all[:, win_all.shape[1] - min(WINDOW, win_all.shape[1]):]

    y = _final(x2d, y_raw, bonus, g, y_n.reshape(m, NSA_W), gates, prm, tm)
    return (y.reshape(b, t, D_MODEL), new_cmp, new_slc, win_state, s_new, shift_new)


def kernel(x_prompt, x_sample, cache_cmp_kv, cache_slc_kv, cache_win_kv, state_rwkv, state_rwkv_shift, page_table, g_mix, w_in, rwkv_mu, rwkv_w0, rwkv_w_decay, rwkv_a0, rwkv_w_aaa, rwkv_w_gate, rwkv_k_k, rwkv_k_a, rwkv_r_k, rwkv_ln_w, rwkv_ln_b, nsa_pe_cmp, nsa_w_cmp1, nsa_w_cmp2, w_br_rwkv, w_br_nsa, w_out, g_ffn, w_ffn_gate, w_ffn_up, w_ffn_down, g_final):
    depth = w_in.shape[0]
    assert depth == 1, "single-layer trunk"
    b, t, _ = x_prompt.shape
    bs, ts, _ = x_sample.shape
    n_pool, page = cache_cmp_kv.shape[1:3]
    past_len = page_table.shape[1] * page
    assert (past_len + ts) // D_CMP == past_len // D_CMP and past_len % D_CMP == 0
    prm = _prep_params(0, g_mix, w_in, rwkv_mu, rwkv_w0, rwkv_w_decay, rwkv_a0, rwkv_w_aaa, rwkv_w_gate,
                       rwkv_k_k, rwkv_k_a, rwkv_r_k, rwkv_ln_w, rwkv_ln_b, w_br_rwkv, w_br_nsa, w_out,
                       g_ffn, w_ffn_gate, w_ffn_up, w_ffn_down, g_final)
    cw = _compress_weights(nsa_pe_cmp[0], nsa_w_cmp1[0], nsa_w_cmp2[0])

    outs_p = _group(x_prompt, jnp.arange(t), jnp.zeros((b, RWKV_COLS), F32),
                    jnp.zeros((b, RWKV_HEADS, RWKV_HEAD, RWKV_HEAD), F32), prm, cw, None,
                    tm=256, tq=min(128, t))
    outs_s = _group(x_sample, past_len + jnp.arange(ts), state_rwkv_shift[0], state_rwkv[0], prm, cw,
                    (cache_cmp_kv[0], cache_slc_kv[0], cache_win_kv[0], page_table),
                    tm=256, tq=ts)
    return (outs_p[0], outs_s[0]) + tuple(o[None] for o in outs_p[1:]) + tuple(o[None] for o in outs_s[1:])
```
